```python
import math
import jax, jax.numpy as jnp
from jax import lax
import numpy as np

D_MODEL = 1024
BATCH = 2
SEQ = 16384
DEPTH = 2

N_META = 16
NORM_EPS = 1e-5
CONV_WIDTH = 4
LRU_WIDTH = D_MODEL // 2
LRU_HEADS = 8
LRU_HEAD_DIM = LRU_WIDTH // LRU_HEADS
LRU_C = 8.0
S5_WIDTH = D_MODEL // 2
S5_GROUP = 16
S5_GROUPS = S5_WIDTH // S5_GROUP
S5_STATE = 64
EVEN_IN = 2 * LRU_WIDTH + S5_WIDTH
EVEN_OUT = LRU_WIDTH + S5_WIDTH
SSD_INNER = 2 * D_MODEL
SSD_HEAD_DIM = 64
SSD_HEADS = SSD_INNER // SSD_HEAD_DIM
SSD_GROUPS = 8
SSD_HPG = SSD_HEADS // SSD_GROUPS
SSD_STATE = 128
SSD_CHUNK = 128
SSD_CONV_DIM = SSD_INNER + 2 * SSD_GROUPS * SSD_STATE
SSD_IN = SSD_INNER + SSD_CONV_DIM + SSD_HEADS
D_FF = 4 * D_MODEL
N_EVEN = (DEPTH + 1) // 2
N_ODD = DEPTH // 2

kernel_name = "hybrid_rglru_s5_ssd_trunk"


def rmsnorm(x, w):
    xf = x.astype(jnp.float32)
    y = xf * lax.rsqrt(jnp.mean(xf * xf, axis=-1, keepdims=True) + NORM_EPS)
    return (y * w.astype(jnp.float32)).astype(x.dtype)


def causal_dwconv(x, w, b):
    t = x.shape[1]
    xp = jnp.pad(x, ((0, 0), (CONV_WIDTH - 1, 0), (0, 0)))
    y = b
    for k in range(CONV_WIDTH):
        y = y + xp[:, k:k + t] * w[k]
    return y


def block_diag(x, w, n_blocks):
    b, t, _ = x.shape
    xb = x.reshape(b, t, n_blocks, -1)
    return jnp.einsum('btgi,gij->btgj', xb, w).reshape(b, t, -1)


def _real_combine(left, right):
    a1, b1 = left
    a2, b2 = right
    return a1 * a2, a2 * b1 + b2


def _complex_combine(left, right):
    ar1, ai1, br1, bi1 = left
    ar2, ai2, br2, bi2 = right
    return (ar2 * ar1 - ai2 * ai1, ar2 * ai1 + ai2 * ar1,
            ar2 * br1 - ai2 * bi1 + br2, ar2 * bi1 + ai2 * br1 + bi2)


def rglru(x, conv_w, conv_b, w_a, b_a, w_x, b_x, lam):
    dtype = x.dtype
    x = causal_dwconv(x, conv_w, conv_b).astype(jnp.float32)
    r = jax.nn.sigmoid(block_diag(x, w_a.astype(jnp.float32), LRU_HEADS) + b_a)
    i = jax.nn.sigmoid(block_diag(x, w_x.astype(jnp.float32), LRU_HEADS) + b_x)
    log_a = -LRU_C * r * jax.nn.softplus(-lam.astype(jnp.float32))
    a = jnp.exp(log_a)
    mult = jnp.sqrt(-jnp.expm1(2.0 * log_a))
    _, h = lax.associative_scan(_real_combine, (a, mult * (i * x)), axis=1)
    return h.astype(dtype)


def s5(u, a_re, a_im, b_re, b_im, c_re, c_im, d, log_dt, w_glu, b_glu):
    dtype = u.dtype
    bsz, t, _ = u.shape
    u = u.astype(jnp.float32)
    a_re, a_im = a_re.astype(jnp.float32), a_im.astype(jnp.float32)
    dt = jnp.exp(log_dt.astype(jnp.float32))[:, None]
    mag = jnp.exp(a_re * dt)
    ar, ai = mag * jnp.cos(a_im * dt), mag * jnp.sin(a_im * dt)
    den = a_re * a_re + a_im * a_im
    fr = ((ar - 1.0) * a_re + ai * a_im) / den
    fi = (ai * a_re - (ar - 1.0) * a_im) / den
    b_re, b_im = b_re.astype(jnp.float32), b_im.astype(jnp.float32)
    bbar_re = fr[..., None] * b_re - fi[..., None] * b_im
    bbar_im = fr[..., None] * b_im + fi[..., None] * b_re
    ug = u.reshape(bsz, t, S5_GROUPS, S5_GROUP)
    bu_re = jnp.einsum('btgi,gpi->btgp', ug, bbar_re)
    bu_im = jnp.einsum('btgi,gpi->btgp', ug, bbar_im)
    ar_t = jnp.broadcast_to(ar[None, None], (1, t, S5_GROUPS, S5_STATE))
    ai_t = jnp.broadcast_to(ai[None, None], (1, t, S5_GROUPS, S5_STATE))
    _, _, s_re, s_im = lax.associative_scan(_complex_combine, (ar_t, ai_t, bu_re, bu_im), axis=1)
    y = (jnp.einsum('btgp,gip->btgi', s_re, c_re.astype(jnp.float32))
         - jnp.einsum('btgp,gip->btgi', s_im, c_im.astype(jnp.float32)))
    y = y.reshape(bsz, t, S5_WIDTH) + d.astype(jnp.float32) * u
    y = jax.nn.gelu(y)
    y = y * jax.nn.sigmoid(block_diag(y, w_glu.astype(jnp.float32), S5_GROUPS) + b_glu)
    return y.astype(dtype)


def even_mixer(h, w_in, lru_conv_w, lru_conv_b, lru_w_a, lru_b_a, lru_w_x, lru_b_x,
               lru_lambda, s5_a_re, s5_a_im, s5_b_re, s5_b_im, s5_c_re, s5_c_im, s5_d,
               s5_log_dt, s5_w_glu, s5_b_glu, w_out):
    proj = h @ w_in
    x_lru = proj[..., :LRU_WIDTH]
    g_lru = proj[..., LRU_WIDTH:2 * LRU_WIDTH]
    u_s5 = proj[..., 2 * LRU_WIDTH:]
    y_lru = rglru(x_lru, lru_conv_w, lru_conv_b, lru_w_a, lru_b_a, lru_w_x, lru_b_x,
                  lru_lambda) * jax.nn.gelu(g_lru)
    y_s5 = s5(u_s5, s5_a_re, s5_a_im, s5_b_re, s5_b_im, s5_c_re, s5_c_im, s5_d,
              s5_log_dt, s5_w_glu, s5_b_glu)
    return jnp.concatenate([y_lru, y_s5], axis=-1) @ w_out


def ssd_chunked(x, dt, a, bm, cm):
    bsz, t = x.shape[:2]
    pad = (-t) % SSD_CHUNK

    def front_pad(v):
        return jnp.pad(v, [(0, 0), (pad, 0)] + [(0, 0)] * (v.ndim - 2))

    x, dt, bm, cm = front_pad(x), front_pad(dt), front_pad(bm), front_pad(cm)
    nc = (t + pad) // SSD_CHUNK
    L = SSD_CHUNK
    x = x.reshape(bsz, nc, L, SSD_GROUPS, SSD_HPG, SSD_HEAD_DIM)
    dt = dt.reshape(bsz, nc, L, SSD_HEADS)
    bm = bm.reshape(bsz, nc, L, SSD_GROUPS, SSD_STATE)
    cm = cm.reshape(bsz, nc, L, SSD_GROUPS, SSD_STATE)
    xdt = x * dt.reshape(bsz, nc, L, SSD_GROUPS, SSD_HPG)[..., None]
    cs = jnp.cumsum(jnp.moveaxis(dt * a, 2, 3), axis=-1)
    seg = cs[..., :, None] - cs[..., None, :]
    causal = jnp.tril(jnp.ones((L, L), dtype=bool))
    decay = jnp.where(causal, jnp.exp(jnp.where(causal, seg, 0.0)), 0.0)
    decay = decay.reshape(bsz, nc, SSD_GROUPS, SSD_HPG, L, L)
    scores = jnp.einsum('bclgn,bcsgn->bcgls', cm, bm)
    y_diag = jnp.einsum('bcgls,bcgrls,bcsgrp->bclgrp', scores, decay, xdt)
    to_end = jnp.exp(cs[..., -1:] - cs).reshape(bsz, nc, SSD_GROUPS, SSD_HPG, L)
    states = jnp.einsum('bcsgn,bcgrs,bcsgrp->bcgrpn', bm, to_end, xdt)
    chunk_decay = jnp.exp(cs[..., -1]).reshape(bsz, nc, SSD_GROUPS, SSD_HPG)

    def step(hc, inp):
        st, dec = inp
        return dec[..., None, None] * hc + st, hc

    h0 = jnp.zeros((bsz, SSD_GROUPS, SSD_HPG, SSD_HEAD_DIM, SSD_STATE), x.dtype)
    _, h_in = lax.scan(step, h0, (jnp.moveaxis(states, 1, 0), jnp.moveaxis(chunk_decay, 1, 0)))
    h_in = jnp.moveaxis(h_in, 0, 1)
    from_start = jnp.exp(cs).reshape(bsz, nc, SSD_GROUPS, SSD_HPG, L)
    y_off = jnp.einsum('bclgn,bcgrpn,bcgrl->bclgrp', cm, h_in, from_start)
    y = (y_diag + y_off).reshape(bsz, nc * L, SSD_HEADS, SSD_HEAD_DIM)
    return y[:, pad:]


def ssd_mixer(h, w_in, conv_w, conv_b, dt_bias, a_log, d, norm_w, w_out):
    dtype = h.dtype
    bsz, t, _ = h.shape
    proj = h @ w_in
    z = proj[..., :SSD_INNER]
    xbc = proj[..., SSD_INNER:SSD_INNER + SSD_CONV_DIM]
    dt_raw = proj[..., SSD_INNER + SSD_CONV_DIM:]
    xbc = jax.nn.silu(causal_dwconv(xbc, conv_w, conv_b)).astype(jnp.float32)
    xs = xbc[..., :SSD_INNER].reshape(bsz, t, SSD_HEADS, SSD_HEAD_DIM)
    bm = xbc[..., SSD_INNER:SSD_INNER + SSD_GROUPS * SSD_STATE].reshape(bsz, t, SSD_GROUPS, SSD_STATE)
    cm = xbc[..., SSD_INNER + SSD_GROUPS * SSD_STATE:].reshape(bsz, t, SSD_GROUPS, SSD_STATE)
    dt = jax.nn.softplus(dt_raw.astype(jnp.float32) + dt_bias.astype(jnp.float32))
    a = -jnp.exp(a_log.astype(jnp.float32))
    y = ssd_chunked(xs, dt, a, bm, cm) + d.astype(jnp.float32)[:, None] * xs
    g = y.reshape(bsz, t, SSD_INNER) * jax.nn.silu(z.astype(jnp.float32))
    g = g.reshape(bsz, t, SSD_GROUPS, -1)
    g = g * lax.rsqrt(jnp.mean(g * g, axis=-1, keepdims=True) + NORM_EPS)
    g = g.reshape(bsz, t, SSD_INNER) * norm_w.astype(jnp.float32)
    return g.astype(dtype) @ w_out


def sq_relu_mlp(h, w_up, w_down):
    return jnp.square(jax.nn.relu(h @ w_up)) @ w_down


def setup_inputs(seed: int = 0) -> dict:
    key = jax.random.key(seed)
    ks = jax.random.split(key, 40)
    nrm = lambda k, shape, s: jax.random.normal(k, shape, jnp.float32) * s
    NE, NO = N_EVEN, N_ODD
    u = jax.random.uniform(ks[12], (NE, LRU_WIDTH), jnp.float32, 0.9, 0.999)
    sg = u ** (1.0 / LRU_C)
    lru_lambda = jnp.log(sg) - jnp.log1p(-sg)
    n_idx = jnp.arange(S5_STATE, dtype=jnp.float32)
    s5_a_re = -0.5 + nrm(ks[13], (NE, S5_GROUPS, S5_STATE), 0.01)
    s5_a_im = math.pi * n_idx + nrm(ks[14], (NE, S5_GROUPS, S5_STATE), 0.01)
    s5_log_dt = jax.random.uniform(ks[20], (NE, S5_GROUPS), jnp.float32,
                                   math.log(0.001), math.log(0.1))
    dt0 = jnp.exp(jax.random.uniform(ks[27], (NO, SSD_HEADS), jnp.float32,
                                     math.log(0.001), math.log(0.1)))
    ssd_dt_bias = dt0 + jnp.log(-jnp.expm1(-dt0))
    ssd_a_log = jnp.log(jax.random.uniform(ks[28], (NO, SSD_HEADS), jnp.float32, 1.0, 16.0))
    return {
        "x": nrm(ks[0], (BATCH, SEQ, D_MODEL), 1.0),
        "meta_tokens": nrm(ks[1], (N_META, D_MODEL), 1.0),
        "norm_mix": 1.0 + nrm(ks[2], (DEPTH, D_MODEL), 0.01),
        "norm_mlp": 1.0 + nrm(ks[3], (DEPTH, D_MODEL), 0.01),
        "norm_final": 1.0 + nrm(ks[4], (D_MODEL,), 0.01),
        "ev_w_in": nrm(ks[5], (NE, D_MODEL, EVEN_IN), D_MODEL ** -0.5),
        "lru_conv_w": nrm(ks[6], (NE, CONV_WIDTH, LRU_WIDTH), CONV_WIDTH ** -0.5),
        "lru_conv_b": nrm(ks[7], (NE, LRU_WIDTH), 0.01),
        "lru_w_a": nrm(ks[8], (NE, LRU_HEADS, LRU_HEAD_DIM, LRU_HEAD_DIM), LRU_HEAD_DIM ** -0.5),
        "lru_b_a": nrm(ks[9], (NE, LRU_WIDTH), 0.01),
        "lru_w_x": nrm(ks[10], (NE, LRU_HEADS, LRU_HEAD_DIM, LRU_HEAD_DIM), LRU_HEAD_DIM ** -0.5),
        "lru_b_x": nrm(ks[11], (NE, LRU_WIDTH), 0.01),
        "lru_lambda": lru_lambda,
        "s5_a_re": s5_a_re,
        "s5_a_im": s5_a_im,
        "s5_b_re": nrm(ks[15], (NE, S5_GROUPS, S5_STATE, S5_GROUP), (2 * S5_GROUP) ** -0.5),
        "s5_b_im": nrm(ks[16], (NE, S5_GROUPS, S5_STATE, S5_GROUP), (2 * S5_GROUP) ** -0.5),
        "s5_c_re": nrm(ks[17], (NE, S5_GROUPS, S5_GROUP, S5_STATE), (2 * S5_STATE) ** -0.5),
        "s5_c_im": nrm(ks[18], (NE, S5_GROUPS, S5_GROUP, S5_STATE), (2 * S5_STATE) ** -0.5),
        "s5_d": nrm(ks[19], (NE, S5_WIDTH), 1.0),
        "s5_log_dt": s5_log_dt,
        "s5_w_glu": nrm(ks[21], (NE, S5_GROUPS, S5_GROUP, S5_GROUP), S5_GROUP ** -0.5),
        "s5_b_glu": nrm(ks[22], (NE, S5_WIDTH), 0.01),
        "ev_w_out": nrm(ks[23], (NE, EVEN_OUT, D_MODEL), EVEN_OUT ** -0.5),
        "ssd_w_in": nrm(ks[24], (NO, D_MODEL, SSD_IN), D_MODEL ** -0.5),
        "ssd_conv_w": nrm(ks[25], (NO, CONV_WIDTH, SSD_CONV_DIM), CONV_WIDTH ** -0.5),
        "ssd_conv_b": nrm(ks[26], (NO, SSD_CONV_DIM), 0.01),
        "ssd_dt_bias": ssd_dt_bias,
        "ssd_a_log": ssd_a_log,
        "ssd_d": 1.0 + nrm(ks[29], (NO, SSD_HEADS), 0.01),
        "ssd_norm": 1.0 + nrm(ks[30], (NO, SSD_INNER), 0.01),
        "ssd_w_out": nrm(ks[31], (NO, SSD_INNER, D_MODEL), SSD_INNER ** -0.5),
        "mlp_w_up": nrm(ks[32], (DEPTH, D_MODEL, D_FF), D_MODEL ** -0.5),
        "mlp_w_down": nrm(ks[33], (DEPTH, D_FF, D_MODEL), D_FF ** -0.5),
    }


def reference(x, meta_tokens, norm_mix, norm_mlp, norm_final, ev_w_in, lru_conv_w,
              lru_conv_b, lru_w_a, lru_b_a, lru_w_x, lru_b_x, lru_lambda, s5_a_re, s5_a_im,
              s5_b_re, s5_b_im, s5_c_re, s5_c_im, s5_d, s5_log_dt, s5_w_glu, s5_b_glu,
              ev_w_out, ssd_w_in, ssd_conv_w, ssd_conv_b, ssd_dt_bias, ssd_a_log, ssd_d,
              ssd_norm, ssd_w_out, mlp_w_up, mlp_w_down):
    bsz = x.shape[0]
    meta = jnp.broadcast_to(meta_tokens[None].astype(x.dtype), (bsz, N_META, x.shape[-1]))
    h = jnp.concatenate([meta, x], axis=1)
    for layer in range(DEPTH):
        i = layer // 2
        hn = rmsnorm(h, norm_mix[layer])
        if layer % 2 == 0:
            mix = even_mixer(hn, ev_w_in[i], lru_conv_w[i], lru_conv_b[i], lru_w_a[i],
                             lru_b_a[i], lru_w_x[i], lru_b_x[i], lru_lambda[i], s5_a_re[i],
                             s5_a_im[i], s5_b_re[i], s5_b_im[i], s5_c_re[i], s5_c_im[i],
                             s5_d[i], s5_log_dt[i], s5_w_glu[i], s5_b_glu[i], ev_w_out[i])
        else:
            mix = ssd_mixer(hn, ssd_w_in[i], ssd_conv_w[i], ssd_conv_b[i], ssd_dt_bias[i],
                            ssd_a_log[i], ssd_d[i], ssd_norm[i], ssd_w_out[i])
        h = h + mix
        h = h + sq_relu_mlp(rmsnorm(h, norm_mlp[layer]), mlp_w_up[layer], mlp_w_down[layer])
    h = rmsnorm(h, norm_final)
    return h[:, N_META:]
```

```python
import functools
import math

import jax
import jax.numpy as jnp
from jax import lax
from jax.experimental import pallas as pl
from jax.experimental.pallas import tpu as pltpu

F32 = jnp.float32
BF16 = jnp.bfloat16

LANES = 128
SUBLANES = 8
NORM_EPS = 1e-5
N_META = 16
CONV_WIDTH = 4
LRU_C = 8.0
S5_GROUP = 16
S5_STATE = 64
SSD_HEAD_DIM = 64
SSD_STATE = 128
SSD_CHUNK = 128
SSD_HPG = 4

LS = 130
TT = SUBLANES * LS
ROW_TILE = 1280
VMEM_LIMIT = 48 * 1024 * 1024


def _cparams(n_axes):
    return pltpu.CompilerParams(dimension_semantics=("arbitrary",) * n_axes,
                                vmem_limit_bytes=VMEM_LIMIT)


def _sigmoid(x):
    return 1.0 / (1.0 + jnp.exp(-x))


def _gelu_tanh(x):
    return 0.5 * x * (1.0 + jnp.tanh(math.sqrt(2.0 / math.pi) * (x + 0.044715 * (x * x * x))))


def _softplus(x):
    return jnp.maximum(x, 0.0) + jnp.log1p(jnp.exp(-jnp.abs(x)))


def _rmsnorm_rows(x, w):
    ms = jnp.mean(x * x, axis=-1, keepdims=True)
    return x * lax.rsqrt(ms + NORM_EPS) * w


def _norm_matmul_kernel(x_ref, nw_ref, w_ref, o_ref, xn_ref, *, slabs):
    @pl.when(pl.program_id(2) == 0)
    def _():
        xn_ref[...] = _rmsnorm_rows(x_ref[0], nw_ref[...]).astype(BF16)

    r = jnp.dot(xn_ref[...], w_ref[...], preferred_element_type=F32)
    if slabs:
        for s in range(slabs):
            o_ref[0, s] = r[:, LANES * s:LANES * (s + 1)]
    else:
        o_ref[0] = r.astype(o_ref.dtype)


def norm_matmul(h, nw, w, *, tm, tn, slab_out):
    b, tp, d = h.shape
    n = w.shape[1]
    grid = (b, tp // tm, n // tn)
    if slab_out:
        out_shape = jax.ShapeDtypeStruct((b, n // LANES, tp, LANES), F32)
        out_spec = pl.BlockSpec((1, tn // LANES, tm, LANES), lambda bi, i, j: (bi, j, i, 0))
    else:
        out_shape = jax.ShapeDtypeStruct((b, tp, n), F32)
        out_spec = pl.BlockSpec((1, tm, tn), lambda bi, i, j: (bi, i, j))
    return pl.pallas_call(
        functools.partial(_norm_matmul_kernel, slabs=(tn // LANES) if slab_out else 0),
        grid=grid,
        in_specs=[pl.BlockSpec((1, tm, d), lambda bi, i, j: (bi, i, 0)),
                  pl.BlockSpec((1, d), lambda bi, i, j: (0, 0)),
                  pl.BlockSpec((d, tn), lambda bi, i, j: (0, j))],
        out_specs=out_spec,
        out_shape=out_shape,
        scratch_shapes=[pltpu.VMEM((tm, d), BF16)],
        compiler_params=_cparams(3),
        name="norm_matmul_slab" if slab_out else "norm_matmul",
    )(h, nw.reshape(1, d), w)


def _mlp_kernel(x_ref, nw_ref, wu_ref, wd_ref, fw_ref, o_ref, xn_ref, *, final_norm):
    j = pl.program_id(1)

    @pl.when(j == 0)
    def _():
        x = x_ref[...]
        xn_ref[...] = _rmsnorm_rows(x, nw_ref[...]).astype(BF16)
        o_ref[...] = x

    u = jnp.dot(xn_ref[...], wu_ref[...], preferred_element_type=F32)
    a = jnp.square(jnp.maximum(u, 0.0)).astype(BF16)
    o_ref[...] += jnp.dot(a, wd_ref[...], preferred_element_type=F32)

    if final_norm:
        @pl.when(j == pl.num_programs(1) - 1)
        def _():
            o_ref[...] = _rmsnorm_rows(o_ref[...], fw_ref[...])


def mlp_residual(h2, nw, w_up, w_down, fw, *, tm, tf, final_norm):
    n, d = h2.shape
    dff = w_up.shape[1]
    return pl.pallas_call(
        functools.partial(_mlp_kernel, final_norm=final_norm),
        grid=(n // tm, dff // tf),
        in_specs=[pl.BlockSpec((tm, d), lambda i, j: (i, 0)),
                  pl.BlockSpec((1, d), lambda i, j: (0, 0)),
                  pl.BlockSpec((d, tf), lambda i, j: (0, j)),
                  pl.BlockSpec((tf, d), lambda i, j: (j, 0)),
                  pl.BlockSpec((1, d), lambda i, j: (0, 0))],
        out_specs=pl.BlockSpec((tm, d), lambda i, j: (i, 0)),
        out_shape=jax.ShapeDtypeStruct((n, d), F32),
        scratch_shapes=[pltpu.VMEM((tm, d), BF16)],
        compiler_params=_cparams(2),
        name="mlp_final" if final_norm else "mlp",
    )(h2, nw.reshape(1, d), w_up, w_down, fw.reshape(1, d))


def _outproj_slab_kernel(h_ref, ya_ref, yb_ref, w_ref, o_ref):
    parts = [ya_ref[0, s] for s in range(ya_ref.shape[1])] + [yb_ref[0, s] for s in range(yb_ref.shape[1])]
    y = jnp.concatenate(parts, axis=-1).astype(BF16)
    o_ref[0] = h_ref[0] + jnp.dot(y, w_ref[...], preferred_element_type=F32)


def outproj_slab(h, ya, yb, w, *, tm):
    b, tp, d = h.shape
    sa, sb = ya.shape[1], yb.shape[1]
    return pl.pallas_call(
        _outproj_slab_kernel,
        grid=(b, tp // tm),
        in_specs=[pl.BlockSpec((1, tm, d), lambda bi, i: (bi, i, 0)),
                  pl.BlockSpec((1, sa, tm, LANES), lambda bi, i: (bi, 0, i, 0)),
                  pl.BlockSpec((1, sb, tm, LANES), lambda bi, i: (bi, 0, i, 0)),
                  pl.BlockSpec(w.shape, lambda bi, i: (0, 0))],
        out_specs=pl.BlockSpec((1, tm, d), lambda bi, i: (bi, i, 0)),
        out_shape=jax.ShapeDtypeStruct(h.shape, F32),
        compiler_params=_cparams(2),
        name="outproj_slab",
    )(h, ya, yb, w)


def _outproj_kernel(h_ref, y_ref, w_ref, o_ref):
    o_ref[0] = h_ref[0] + jnp.dot(y_ref[0], w_ref[...], preferred_element_type=F32)


def outproj(h, y, w, *, tm):
    b, tp, d = h.shape
    k = y.shape[-1]
    return pl.pallas_call(
        _outproj_kernel,
        grid=(b, tp // tm),
        in_specs=[pl.BlockSpec((1, tm, d), lambda bi, i: (bi, i, 0)),
                  pl.BlockSpec((1, tm, k), lambda bi, i: (bi, i, 0)),
                  pl.BlockSpec(w.shape, lambda bi, i: (0, 0))],
        out_specs=pl.BlockSpec((1, tm, d), lambda bi, i: (bi, i, 0)),
        out_shape=jax.ShapeDtypeStruct(h.shape, F32),
        compiler_params=_cparams(2),
        name="outproj",
    )(h, y, w)


def _rglru_kernel(x_ref, g_ref, cw_ref, cb_ref, wa_ref, ba_ref, wx_ref, bx_ref, lam_ref, o_ref,
                  xbuf, xcp, a_s, b_s, carry):
    n_slab = x_ref.shape[1]
    hist = SUBLANES

    @pl.when(pl.program_id(1) == 0)
    def _():
        xbuf[:, 0:hist, :] = jnp.zeros((n_slab, hist, LANES), F32)
        carry[...] = jnp.zeros(carry.shape, F32)

    for s in range(n_slab):
        sl = slice(LANES * s, LANES * (s + 1))
        xbuf[s, hist:hist + TT, :] = x_ref[0, s]
        taps = [jnp.broadcast_to(cw_ref[CONV_WIDTH - 1 - m:CONV_WIDTH - m, sl], (SUBLANES, LANES))
                for m in range(CONV_WIDTH)]
        bias = jnp.broadcast_to(cb_ref[:, sl], (SUBLANES, LANES))

        def conv_body(tau, c, s=s, sl=sl, taps=taps, bias=bias):
            acc = bias
            for m in range(CONV_WIDTH):
                acc = acc + taps[m] * xbuf[s, pl.ds(hist + tau - m, SUBLANES, stride=LS), :]
            xcp[pl.ds(pl.multiple_of(tau * SUBLANES, SUBLANES), SUBLANES), sl] = acc
            return c

        lax.fori_loop(0, LS, conv_body, 0)
        xbuf[s, 0:hist, :] = xbuf[s, TT:TT + hist, :]

    xc = xcp[...]
    xb = xc.astype(BF16)
    r = _sigmoid(jnp.dot(xb, wa_ref[...], preferred_element_type=F32) + ba_ref[...])
    i = _sigmoid(jnp.dot(xb, wx_ref[...], preferred_element_type=F32) + bx_ref[...])
    log_a = (-LRU_C) * r * _softplus(-lam_ref[...])
    a = jnp.exp(log_a)
    a_s[...] = a
    b_s[...] = jnp.sqrt(-jnp.tanh(log_a) * (a * a + 1.0)) * (i * xc)

    for s in range(n_slab):
        sl = slice(LANES * s, LANES * (s + 1))

        def pass1(tau, c, sl=sl):
            p, e = c
            r0 = pl.multiple_of(tau * SUBLANES, SUBLANES)
            av = a_s[pl.ds(r0, SUBLANES), sl]
            return p * av, av * e + b_s[pl.ds(r0, SUBLANES), sl]

        p, e = lax.fori_loop(0, LS, pass1, (jnp.ones((SUBLANES, LANES), F32),
                                            jnp.zeros((SUBLANES, LANES), F32)))
        c = carry[0:1, sl]
        rows = []
        for j in range(SUBLANES):
            rows.append(c)
            c = p[j:j + 1] * c + e[j:j + 1]
        carry[0:1, sl] = c
        h0 = jnp.concatenate(rows, axis=0)

        def pass2(tau, h, s=s, sl=sl):
            r0 = pl.multiple_of(tau * SUBLANES, SUBLANES)
            h = a_s[pl.ds(r0, SUBLANES), sl] * h + b_s[pl.ds(r0, SUBLANES), sl]
            gv = g_ref[0, s, pl.ds(tau, SUBLANES, stride=LS), :]
            o_ref[0, s, pl.ds(tau, SUBLANES, stride=LS), :] = h * _gelu_tanh(gv)
            return h

        lax.fori_loop(0, LS, pass2, h0)


def rglru(proj, cw, cb, wa_bd, ba, wx_bd, bx, lam):
    b, _, tp, _ = proj.shape
    w = cw.shape[1]
    ns = w // LANES
    const = lambda shape: pl.BlockSpec(shape, lambda bi, t: (0,) * len(shape))
    return pl.pallas_call(
        _rglru_kernel,
        grid=(b, tp // TT),
        in_specs=[pl.BlockSpec((1, ns, TT, LANES), lambda bi, t: (bi, 0, t, 0)),
                  pl.BlockSpec((1, ns, TT, LANES), lambda bi, t: (bi, 1, t, 0)),
                  const((CONV_WIDTH, w)), const((1, w)),
                  const((w, w)), const((1, w)), const((w, w)), const((1, w)), const((1, w))],
        out_specs=pl.BlockSpec((1, ns, TT, LANES), lambda bi, t: (bi, 0, t, 0)),
        out_shape=jax.ShapeDtypeStruct((b, ns, tp, LANES), F32),
        scratch_shapes=[pltpu.VMEM((ns, SUBLANES + TT, LANES), F32),
                        pltpu.VMEM((TT, w), F32), pltpu.VMEM((TT, w), F32), pltpu.VMEM((TT, w), F32),
                        pltpu.VMEM((SUBLANES, w), F32)],
        compiler_params=_cparams(2),
        name="rglru",
    )(proj, proj, cw, cb.reshape(1, w), wa_bd, ba.reshape(1, w), wx_bd, bx.reshape(1, w),
      lam.reshape(1, w))


def _s5_kernel(u_ref, bb_ref, wc_ref, lre_ref, lim_ref, pre_ref, pim_ref, d_ref, wg_ref, bg_ref,
               o_ref, up, bu, yp, carry):
    n_cs = lre_ref.shape[1]
    im0 = n_cs * LANES

    @pl.when(pl.program_id(2) == 0)
    def _():
        carry[...] = jnp.zeros(carry.shape, F32)

    def perm_body(tau, c):
        up[pl.ds(pl.multiple_of(tau * SUBLANES, SUBLANES), SUBLANES), :] = \
            u_ref[0, 0, pl.ds(tau, SUBLANES, stride=LS), :]
        return c

    lax.fori_loop(0, LS, perm_body, 0)
    u = up[...]
    bu[...] = jnp.dot(u.astype(BF16), bb_ref[0], preferred_element_type=F32)

    for k in range(n_cs):
        cr = slice(LANES * k, LANES * (k + 1))
        ci = slice(im0 + LANES * k, im0 + LANES * (k + 1))
        lr = jnp.broadcast_to(lre_ref[0, k], (SUBLANES, LANES))
        li = jnp.broadcast_to(lim_ref[0, k], (SUBLANES, LANES))

        def step(tau, sr, si, cr=cr, ci=ci, lr=lr, li=li):
            r0 = pl.multiple_of(tau * SUBLANES, SUBLANES)
            nr = lr * sr - li * si + bu[pl.ds(r0, SUBLANES), cr]
            ni = lr * si + li * sr + bu[pl.ds(r0, SUBLANES), ci]
            return r0, nr, ni

        def pass1(tau, c, step=step):
            _, nr, ni = step(tau, *c)
            return nr, ni

        zero = jnp.zeros((SUBLANES, LANES), F32)
        er, ei = lax.fori_loop(0, LS, pass1, (zero, zero))

        pr, pi = pre_ref[0, k], pim_ref[0, k]
        c_r, c_i = carry[0:1, cr], carry[0:1, ci]
        rows_r, rows_i = [], []
        for j in range(SUBLANES):
            rows_r.append(c_r)
            rows_i.append(c_i)
            c_r, c_i = (pr * c_r - pi * c_i + er[j:j + 1], pr * c_i + pi * c_r + ei[j:j + 1])
        carry[0:1, cr] = c_r
        carry[0:1, ci] = c_i

        def pass2(tau, c, step=step, cr=cr, ci=ci):
            r0, nr, ni = step(tau, *c)
            bu[pl.ds(r0, SUBLANES), cr] = nr
            bu[pl.ds(r0, SUBLANES), ci] = ni
            return nr, ni

        lax.fori_loop(0, LS, pass2, (jnp.concatenate(rows_r, axis=0), jnp.concatenate(rows_i, axis=0)))

    y = jnp.dot(bu[...].astype(BF16), wc_ref[0], preferred_element_type=F32) + d_ref[0] * u
    y = _gelu_tanh(y)
    y = y * _sigmoid(jnp.dot(y.astype(BF16), wg_ref[0], preferred_element_type=F32) + bg_ref[0])
    yp[...] = y

    def unperm_body(tau, c):
        o_ref[0, 0, pl.ds(tau, SUBLANES, stride=LS), :] = \
            yp[pl.ds(pl.multiple_of(tau * SUBLANES, SUBLANES), SUBLANES), :]
        return c

    lax.fori_loop(0, LS, unperm_body, 0)


def s5(proj, first_slab, bb, wc, lre, lim, pre, pim, d, wg, bg):
    b, _, tp, _ = proj.shape
    nblk, _, nst = bb.shape
    n_cs = nst // (2 * LANES)
    per_blk = lambda shape: pl.BlockSpec((1,) + shape, lambda bi, gb, t: (gb,) + (0,) * len(shape))
    return pl.pallas_call(
        _s5_kernel,
        grid=(b, nblk, tp // TT),
        in_specs=[pl.BlockSpec((1, 1, TT, LANES), lambda bi, gb, t: (bi, first_slab + gb, t, 0)),
                  per_blk((LANES, nst)), per_blk((nst, LANES)),
                  per_blk((n_cs, 1, LANES)), per_blk((n_cs, 1, LANES)),
                  per_blk((n_cs, 1, LANES)), per_blk((n_cs, 1, LANES)),
                  per_blk((1, LANES)), per_blk((LANES, LANES)), per_blk((1, LANES))],
        out_specs=pl.BlockSpec((1, 1, TT, LANES), lambda bi, gb, t: (bi, gb, t, 0)),
        out_shape=jax.ShapeDtypeStruct((b, nblk, tp, LANES), F32),
        scratch_shapes=[pltpu.VMEM((TT, LANES), F32), pltpu.VMEM((TT, nst), F32),
                        pltpu.VMEM((TT, LANES), F32), pltpu.VMEM((SUBLANES, nst), F32)],
        compiler_params=_cparams(3),
        name="s5",
    )(proj, bb, wc, lre, lim, pre, pim, d, wg, bg)


def _s5_params(a_re, a_im, b_re, b_im, c_re, c_im, d, log_dt, w_glu, b_glu):
    g, p = a_re.shape
    gpb = LANES // S5_GROUP
    nblk = g // gpb
    dt = jnp.exp(log_dt)[:, None]
    mag = jnp.exp(a_re * dt)
    ar, ai = mag * jnp.cos(a_im * dt), mag * jnp.sin(a_im * dt)
    den = a_re * a_re + a_im * a_im
    fr = ((ar - 1.0) * a_re + ai * a_im) / den
    fi = (ai * a_re - (ar - 1.0) * a_im) / den
    bbar_re = fr[..., None] * b_re - fi[..., None] * b_im
    bbar_im = fr[..., None] * b_im + fi[..., None] * b_re
    mag_p = jnp.exp(a_re * dt * LS)
    pr, pi = mag_p * jnp.cos(a_im * dt * LS), mag_p * jnp.sin(a_im * dt * LS)
    eye = jnp.eye(gpb, dtype=F32)

    def in_bd(m):
        m = m.reshape(nblk, gpb, p, S5_GROUP)
        return jnp.einsum('bqpi,qr->bqirp', m, eye).reshape(nblk, gpb * S5_GROUP, gpb * p)

    def out_bd(m):
        m = m.reshape(nblk, gpb, S5_GROUP, p)
        return jnp.einsum('bqip,qr->bqpri', m, eye).reshape(nblk, gpb * p, gpb * S5_GROUP)

    bb = jnp.concatenate([in_bd(bbar_re), in_bd(bbar_im)], axis=-1).astype(BF16)
    wc = jnp.concatenate([out_bd(c_re), -out_bd(c_im)], axis=1).astype(BF16)
    vec = lambda v: v.reshape(nblk, (gpb * p) // LANES, 1, LANES)
    wg = jnp.einsum('bqij,qr->bqirj', w_glu.reshape(nblk, gpb, S5_GROUP, S5_GROUP), eye)
    wg = wg.reshape(nblk, LANES, LANES).astype(BF16)
    return (bb, wc, vec(ar), vec(ai), vec(pr), vec(pi), d.reshape(nblk, 1, LANES), wg,
            b_glu.reshape(nblk, 1, LANES))


def _ssd_kernel(xbc_ref, z_ref, dt_ref, cw_ref, cb_ref, dtb_ref, alog_ref, dskip_ref, nw_ref, o_ref,
                xbuf, hst):
    L = SSD_CHUNK
    hist = SUBLANES
    n_groups = hst.shape[0]
    gw = SSD_HPG * SSD_HEAD_DIM
    d_inner = n_groups * gw

    @pl.when(pl.program_id(1) == 0)
    def _():
        xbuf[0:hist, :] = jnp.zeros((hist, xbuf.shape[1]), F32)
        hst[...] = jnp.zeros(hst.shape, F32)

    xbuf[hist:hist + L, :] = xbc_ref[0]
    acc = cb_ref[...] + cw_ref[3:4, :] * xbuf[hist:hist + L, :]
    for m in range(1, CONV_WIDTH):
        acc = acc + cw_ref[3 - m:4 - m, :] * xbuf[hist - m:hist - m + L, :]
    xbuf[0:hist, :] = xbuf[L:L + hist, :]
    xa = acc * _sigmoid(acc)

    dt = _softplus(dt_ref[0] + dtb_ref[...])
    da = dt * (-jnp.exp(alog_ref[...]))
    row = lax.broadcasted_iota(jnp.int32, (L, L), 0)
    col = lax.broadcasted_iota(jnp.int32, (L, L), 1)
    causal = row >= col
    tri = causal.astype(F32)
    cs = jnp.dot(tri, da, preferred_element_type=F32, precision=lax.Precision.HIGHEST)
    cst = jnp.dot(da.T, tri.T, preferred_element_type=F32, precision=lax.Precision.HIGHEST)
    cs_last = cs[L - 1:L, :]
    w_end = dt * jnp.exp(cs_last - cs)
    f_start = jnp.exp(cs)
    c_dec = jnp.exp(cs_last)

    def expand(m, g, rows):
        return jnp.concatenate(
            [jnp.broadcast_to(m[:, SSD_HPG * g + r:SSD_HPG * g + r + 1], (rows, SSD_HEAD_DIM))
             for r in range(SSD_HPG)], axis=-1)

    for g in range(n_groups):
        xs = xa[:, gw * g:gw * (g + 1)]
        bm = xa[:, d_inner + SSD_STATE * g:d_inner + SSD_STATE * (g + 1)].astype(BF16)
        cm = xa[:, d_inner + SSD_STATE * (n_groups + g):
                d_inner + SSD_STATE * (n_groups + g + 1)].astype(BF16)
        scores = lax.dot_general(cm, bm, (((1,), (1,)), ((), ())), preferred_element_type=F32)
        xdt = (xs * expand(dt, g, L)).astype(BF16)
        xw = (xs * expand(w_end, g, L)).astype(BF16)
        st_new = jnp.dot(bm.T, xw, preferred_element_type=F32)
        h_prev = hst[g]
        y_off = jnp.dot(cm, h_prev.astype(BF16), preferred_element_type=F32) * expand(f_start, g, L)
        hst[g] = expand(c_dec, g, 1) * h_prev + st_new
        ys = []
        for r in range(SSD_HPG):
            h = SSD_HPG * g + r
            seg = jnp.broadcast_to(cs[:, h:h + 1], (L, L)) - jnp.broadcast_to(cst[h:h + 1, :], (L, L))
            dec = jnp.where(causal, jnp.exp(jnp.where(causal, seg, 0.0)), 0.0)
            m = (scores * dec).astype(BF16)
            ys.append(jnp.dot(m, xdt[:, SSD_HEAD_DIM * r:SSD_HEAD_DIM * (r + 1)],
                              preferred_element_type=F32))
        y = jnp.concatenate(ys, axis=-1) + y_off + dskip_ref[:, gw * g:gw * (g + 1)] * xs
        zg = z_ref[0, :, gw * g:gw * (g + 1)]
        gg = y * (zg * _sigmoid(zg))
        ms = jnp.mean(gg * gg, axis=-1, keepdims=True)
        o_ref[0, :, gw * g:gw * (g + 1)] = (gg * lax.rsqrt(ms + NORM_EPS)
                                            * nw_ref[:, gw * g:gw * (g + 1)]).astype(o_ref.dtype)


def ssd_core(proj, cw, cb, dtb, alog, dskip, nw, *, d_inner, conv_dim):
    b, tp, _ = proj.shape
    n_groups = d_inner // (SSD_HPG * SSD_HEAD_DIM)
    L = SSD_CHUNK
    const = lambda shape: pl.BlockSpec(shape, lambda bi, c: (0,) * len(shape))
    return pl.pallas_call(
        _ssd_kernel,
        grid=(b, tp // L),
        in_specs=[pl.BlockSpec((1, L, conv_dim), lambda bi, c: (bi, c, 0)),
                  pl.BlockSpec((1, L, d_inner), lambda bi, c: (bi, c, conv_dim // d_inner)),
                  pl.BlockSpec((1, L, LANES), lambda bi, c: (bi, c, (conv_dim + d_inner) // LANES)),
                  const((CONV_WIDTH, conv_dim)), const((1, conv_dim)),
                  const((1, LANES)), const((1, LANES)), const((1, d_inner)), const((1, d_inner))],
        out_specs=pl.BlockSpec((1, L, d_inner), lambda bi, c: (bi, c, 0)),
        out_shape=jax.ShapeDtypeStruct((b, tp, d_inner), BF16),
        scratch_shapes=[pltpu.VMEM((SUBLANES + L, conv_dim), F32),
                        pltpu.VMEM((n_groups, SSD_STATE, SSD_HPG * SSD_HEAD_DIM), F32)],
        compiler_params=_cparams(2),
        name="ssd_core",
    )(proj, proj, proj, cw, cb, dtb, alog, dskip, nw)


def _block_diag(w):
    n, di, do = w.shape
    return jnp.einsum('gij,gh->gihj', w, jnp.eye(n, dtype=w.dtype)).reshape(n * di, n * do)


def _pad_lanes(v, fill=0.0):
    return jnp.pad(v, (0, LANES - v.shape[0]), constant_values=fill).reshape(1, LANES)


def kernel(x, meta_tokens, norm_mix, norm_mlp, norm_final, ev_w_in, lru_conv_w, lru_conv_b, lru_w_a, lru_b_a, lru_w_x, lru_b_x, lru_lambda, s5_a_re, s5_a_im, s5_b_re, s5_b_im, s5_c_re, s5_c_im, s5_d, s5_log_dt, s5_w_glu, s5_b_glu, ev_w_out, ssd_w_in, ssd_conv_w, ssd_conv_b, ssd_dt_bias, ssd_a_log, ssd_d, ssd_norm, ssd_w_out, mlp_w_up, mlp_w_down):
    bsz, seq, d = x.shape
    t = seq + N_META
    unit = math.lcm(TT, ROW_TILE, SSD_CHUNK)
    tp = -(-t // unit) * unit
    meta = jnp.broadcast_to(meta_tokens[None].astype(x.dtype), (bsz, N_META, d))
    h = jnp.concatenate([meta, x, jnp.zeros((bsz, tp - t, d), x.dtype)], axis=1)

    lru_w = lru_conv_w.shape[-1]
    proj = norm_matmul(h, norm_mix[0], ev_w_in[0].astype(BF16), tm=ROW_TILE, tn=512, slab_out=True)
    y_lru = rglru(proj, lru_conv_w[0], lru_conv_b[0], _block_diag(lru_w_a[0]).astype(BF16), lru_b_a[0],
                  _block_diag(lru_w_x[0]).astype(BF16), lru_b_x[0], lru_lambda[0])
    y_s5 = s5(proj, 2 * lru_w // LANES,
              *_s5_params(s5_a_re[0], s5_a_im[0], s5_b_re[0], s5_b_im[0], s5_c_re[0], s5_c_im[0],
                          s5_d[0], s5_log_dt[0], s5_w_glu[0], s5_b_glu[0]))
    h = outproj_slab(h, y_lru, y_s5, ev_w_out[0].astype(BF16), tm=ROW_TILE)
    h = mlp_residual(h.reshape(bsz * tp, d), norm_mlp[0], mlp_w_up[0].astype(BF16),
                     mlp_w_down[0].astype(BF16), norm_final, tm=ROW_TILE, tf=512,
                     final_norm=False).reshape(bsz, tp, d)

    d_inner = ssd_w_out.shape[1]
    conv_dim = ssd_conv_w.shape[-1]
    n_heads = ssd_dt_bias.shape[-1]
    w_in = ssd_w_in[0]
    w_cat = jnp.concatenate([w_in[:, d_inner:d_inner + conv_dim], w_in[:, :d_inner],
                             w_in[:, d_inner + conv_dim:],
                             jnp.zeros((d, 2 * LANES - n_heads), F32)], axis=1).astype(BF16)
    proj = norm_matmul(h, norm_mix[1], w_cat, tm=ROW_TILE, tn=w_cat.shape[1] // 5, slab_out=False)
    g = ssd_core(proj, ssd_conv_w[0], ssd_conv_b[0].reshape(1, conv_dim), _pad_lanes(ssd_dt_bias[0]),
                 _pad_lanes(ssd_a_log[0]), jnp.repeat(ssd_d[0], SSD_HEAD_DIM).reshape(1, d_inner),
                 ssd_norm[0].reshape(1, d_inner), d_inner=d_inner, conv_dim=conv_dim)
    h = outproj(h, g, ssd_w_out[0].astype(BF16), tm=ROW_TILE)
    h = mlp_residual(h.reshape(bsz * tp, d), norm_mlp[1], mlp_w_up[1].astype(BF16),
                     mlp_w_down[1].astype(BF16), norm_final, tm=ROW_TILE, tf=512,
                     final_norm=True).reshape(bsz, tp, d)
    return h[:, N_META:t]
```

```python
import functools
import math

import jax
import jax.numpy as jnp
from jax import lax
from jax.experimental import pallas as pl
from jax.experimental.pallas import tpu as pltpu

F32 = jnp.float32
BF16 = jnp.bfloat16

LANES = 128
SUBLANES = 8
NORM_EPS = 1e-5
N_META = 16
CONV_WIDTH = 4
LRU_C = 8.0
S5_GROUP = 16
S5_STATE = 64
SSD_HEAD_DIM = 64
SSD_STATE = 128
SSD_CHUNK = 128
SSD_HPG = 4

LRU_LS = 130
S5_SC = SUBLANES
S5_LS = 65
ROW_TILE = 1280
VMEM_LIMIT = 48 * 1024 * 1024


def _cparams(n_axes):
    return pltpu.CompilerParams(dimension_semantics=("arbitrary",) * n_axes,
                                vmem_limit_bytes=VMEM_LIMIT)


def _sigmoid(x):
    return 1.0 / (1.0 + jnp.exp(-x))


def _silu(x):
    hx = 0.5 * x
    return hx + hx * jnp.tanh(hx)


def _gelu_tanh(x):
    return 0.5 * x * (1.0 + jnp.tanh(math.sqrt(2.0 / math.pi) * (x + 0.044715 * (x * x * x))))


def _log1p(e):
    u = 1.0 + e
    return jnp.where(u == 1.0, e, jnp.log(u) * (e / (u - 1.0)))


def _softplus(x):
    return jnp.maximum(x, 0.0) + _log1p(jnp.exp(-jnp.abs(x)))


def _rmsnorm_rows(x, w):
    ms = jnp.mean(x * x, axis=-1, keepdims=True)
    return x * lax.rsqrt(ms + NORM_EPS) * w


def _rows(tau):
    return pl.ds(pl.multiple_of(tau * SUBLANES, SUBLANES), SUBLANES)


def _norm_matmul_kernel(x_ref, nw_ref, w_ref, o_ref, xn_ref):
    @pl.when(pl.program_id(2) == 0)
    def _():
        xn_ref[...] = _rmsnorm_rows(x_ref[0], nw_ref[...]).astype(BF16)

    r = jnp.dot(xn_ref[...], w_ref[...], preferred_element_type=F32)
    for s in range(o_ref.shape[1]):
        o_ref[0, s] = r[:, LANES * s:LANES * (s + 1)]


def norm_matmul(h, nw, w, *, tm, tn):
    b, tp, d = h.shape
    n = w.shape[1]
    return pl.pallas_call(
        _norm_matmul_kernel,
        grid=(b, tp // tm, n // tn),
        in_specs=[pl.BlockSpec((1, tm, d), lambda bi, i, j: (bi, i, 0)),
                  pl.BlockSpec((1, d), lambda bi, i, j: (0, 0)),
                  pl.BlockSpec((d, tn), lambda bi, i, j: (0, j))],
        out_specs=pl.BlockSpec((1, tn // LANES, tm, LANES), lambda bi, i, j: (bi, j, i, 0)),
        out_shape=jax.ShapeDtypeStruct((b, n // LANES, tp, LANES), F32),
        scratch_shapes=[pltpu.VMEM((tm, d), BF16)],
        compiler_params=_cparams(3),
        name="norm_matmul",
    )(h, nw.reshape(1, d), w)


def _mlp_kernel(x_ref, nw_ref, wu_ref, wd_ref, fw_ref, o_ref, xn_ref, *, final_norm):
    j = pl.program_id(1)

    @pl.when(j == 0)
    def _():
        x = x_ref[...]
        xn_ref[...] = _rmsnorm_rows(x, nw_ref[...]).astype(BF16)
        o_ref[...] = x

    u = jnp.dot(xn_ref[...], wu_ref[...], preferred_element_type=F32)
    a = jnp.square(jnp.maximum(u, 0.0)).astype(BF16)
    o_ref[...] += jnp.dot(a, wd_ref[...], preferred_element_type=F32)

    if final_norm:
        @pl.when(j == pl.num_programs(1) - 1)
        def _():
            o_ref[...] = _rmsnorm_rows(o_ref[...], fw_ref[...])


def mlp_residual(h2, nw, w_up, w_down, fw, *, tm, tf, final_norm):
    n, d = h2.shape
    dff = w_up.shape[1]
    return pl.pallas_call(
        functools.partial(_mlp_kernel, final_norm=final_norm),
        grid=(n // tm, dff // tf),
        in_specs=[pl.BlockSpec((tm, d), lambda i, j: (i, 0)),
                  pl.BlockSpec((1, d), lambda i, j: (0, 0)),
                  pl.BlockSpec((d, tf), lambda i, j: (0, j)),
                  pl.BlockSpec((tf, d), lambda i, j: (j, 0)),
                  pl.BlockSpec((1, d), lambda i, j: (0, 0))],
        out_specs=pl.BlockSpec((tm, d), lambda i, j: (i, 0)),
        out_shape=jax.ShapeDtypeStruct((n, d), F32),
        scratch_shapes=[pltpu.VMEM((tm, d), BF16)],
        compiler_params=_cparams(2),
        name="mlp_final" if final_norm else "mlp",
    )(h2, nw.reshape(1, d), w_up, w_down, fw.reshape(1, d))


def _outproj_slab_kernel(h_ref, ya_ref, yb_ref, w_ref, o_ref):
    parts = [ya_ref[0, s] for s in range(ya_ref.shape[1])] + [yb_ref[0, s] for s in range(yb_ref.shape[1])]
    y = jnp.concatenate(parts, axis=-1).astype(BF16)
    o_ref[0] = h_ref[0] + jnp.dot(y, w_ref[...], preferred_element_type=F32)


def outproj_slab(h, ya, yb, w, *, tm):
    b, tp, d = h.shape
    sa, sb = ya.shape[1], yb.shape[1]
    return pl.pallas_call(
        _outproj_slab_kernel,
        grid=(b, tp // tm),
        in_specs=[pl.BlockSpec((1, tm, d), lambda bi, i: (bi, i, 0)),
                  pl.BlockSpec((1, sa, tm, LANES), lambda bi, i: (bi, 0, i, 0)),
                  pl.BlockSpec((1, sb, tm, LANES), lambda bi, i: (bi, 0, i, 0)),
                  pl.BlockSpec(w.shape, lambda bi, i: (0, 0))],
        out_specs=pl.BlockSpec((1, tm, d), lambda bi, i: (bi, i, 0)),
        out_shape=jax.ShapeDtypeStruct(h.shape, F32),
        compiler_params=_cparams(2),
        name="outproj_slab",
    )(h, ya, yb, w)


def _outproj_kernel(h_ref, y_ref, w_ref, o_ref):
    o_ref[0] = h_ref[0] + jnp.dot(y_ref[0], w_ref[...], preferred_element_type=F32)


def outproj(h, y, w, *, tm):
    b, tp, d = h.shape
    k = y.shape[-1]
    return pl.pallas_call(
        _outproj_kernel,
        grid=(b, tp // tm),
        in_specs=[pl.BlockSpec((1, tm, d), lambda bi, i: (bi, i, 0)),
                  pl.BlockSpec((1, tm, k), lambda bi, i: (bi, i, 0)),
                  pl.BlockSpec(w.shape, lambda bi, i: (0, 0))],
        out_specs=pl.BlockSpec((1, tm, d), lambda bi, i: (bi, i, 0)),
        out_shape=jax.ShapeDtypeStruct(h.shape, F32),
        compiler_params=_cparams(2),
        name="outproj",
    )(h, y, w)


def _rglru_kernel(x_ref, g_ref, cw_ref, cb_ref, wa_ref, ba_ref, wx_ref, bx_ref, lam_ref, o_ref,
                  xbuf, xcp, a_s, b_s, hn, carry, *, ls):
    n_slab = x_ref.shape[1]
    tt = SUBLANES * ls
    hist = SUBLANES
    slabs = [slice(LANES * s, LANES * (s + 1)) for s in range(n_slab)]

    @pl.when(pl.program_id(1) == 0)
    def _():
        xbuf[:, 0:hist, :] = jnp.zeros((n_slab, hist, LANES), F32)
        carry[...] = jnp.zeros(carry.shape, F32)

    for s in range(n_slab):
        xbuf[s, hist:hist + tt, :] = x_ref[0, s]

    def conv_body(tau, c):
        for s in range(n_slab):
            acc = cb_ref[:, slabs[s]]
            for m in range(CONV_WIDTH):
                acc = acc + cw_ref[m, :, slabs[s]] * xbuf[s, pl.ds(hist + tau - m, SUBLANES, stride=ls), :]
            xcp[_rows(tau), slabs[s]] = acc
        return c

    lax.fori_loop(0, ls, conv_body, 0, unroll=2)
    for s in range(n_slab):
        xbuf[s, 0:hist, :] = xbuf[s, tt:tt + hist, :]

    xc = xcp[...]
    xb = xc.astype(BF16)
    r = _sigmoid(jnp.dot(xb, wa_ref[...], preferred_element_type=F32) + ba_ref[...])
    i = _sigmoid(jnp.dot(xb, wx_ref[...], preferred_element_type=F32) + bx_ref[...])
    log_a = (-LRU_C) * r * _softplus(-lam_ref[...])
    a = jnp.exp(log_a)
    a_s[...] = a
    b_s[...] = jnp.sqrt(-jnp.tanh(log_a) * (a * a + 1.0)) * (i * xc)

    def pass1(tau, c):
        out = []
        for s in range(n_slab):
            p, e = c[2 * s], c[2 * s + 1]
            av = a_s[_rows(tau), slabs[s]]
            out += [p * av, av * e + b_s[_rows(tau), slabs[s]]]
        return tuple(out)

    one = jnp.ones((SUBLANES, LANES), F32)
    zero = jnp.zeros((SUBLANES, LANES), F32)
    pe = lax.fori_loop(0, ls, pass1, (one, zero) * n_slab, unroll=2)

    h0 = []
    for s in range(n_slab):
        p, e = pe[2 * s], pe[2 * s + 1]
        c = carry[0:1, slabs[s]]
        rows = []
        for j in range(SUBLANES):
            rows.append(c)
            c = p[j:j + 1] * c + e[j:j + 1]
        carry[0:1, slabs[s]] = c
        h0.append(jnp.concatenate(rows, axis=0))

    def pass2(tau, hs):
        out = []
        for s in range(n_slab):
            h = a_s[_rows(tau), slabs[s]] * hs[s] + b_s[_rows(tau), slabs[s]]
            hn[s, pl.ds(tau, SUBLANES, stride=ls), :] = h
            out.append(h)
        return tuple(out)

    lax.fori_loop(0, ls, pass2, tuple(h0), unroll=2)
    for s in range(n_slab):
        o_ref[0, s] = hn[s] * _gelu_tanh(g_ref[0, s])


def rglru(proj, cw, cb, wa_bd, ba, wx_bd, bx, lam, *, ls):
    b, _, tp, _ = proj.shape
    w = cw.shape[1]
    ns = w // LANES
    tt = SUBLANES * ls
    cw8 = jnp.broadcast_to(jnp.stack([cw[CONV_WIDTH - 1 - m] for m in range(CONV_WIDTH)])[:, None, :],
                           (CONV_WIDTH, SUBLANES, w))
    cb8 = jnp.broadcast_to(cb[None, :], (SUBLANES, w))
    const = lambda shape: pl.BlockSpec(shape, lambda bi, t: (0,) * len(shape))
    return pl.pallas_call(
        functools.partial(_rglru_kernel, ls=ls),
        grid=(b, tp // tt),
        in_specs=[pl.BlockSpec((1, ns, tt, LANES), lambda bi, t: (bi, 0, t, 0)),
                  pl.BlockSpec((1, ns, tt, LANES), lambda bi, t: (bi, 1, t, 0)),
                  const((CONV_WIDTH, SUBLANES, w)), const((SUBLANES, w)),
                  const((w, w)), const((1, w)), const((w, w)), const((1, w)), const((1, w))],
        out_specs=pl.BlockSpec((1, ns, tt, LANES), lambda bi, t: (bi, 0, t, 0)),
        out_shape=jax.ShapeDtypeStruct((b, ns, tp, LANES), F32),
        scratch_shapes=[pltpu.VMEM((ns, SUBLANES + tt, LANES), F32),
                        pltpu.VMEM((tt, w), F32), pltpu.VMEM((tt, w), F32), pltpu.VMEM((tt, w), F32),
                        pltpu.VMEM((ns, tt, LANES), F32), pltpu.VMEM((SUBLANES, w), F32)],
        compiler_params=_cparams(2),
        name="rglru",
    )(proj, proj, cw8, cb8, wa_bd, ba.reshape(1, w), wx_bd, bx.reshape(1, w), lam.reshape(1, w))


def _s5_kernel(u_ref, bend_ref, kc_ref, l8r_ref, l8i_ref, plr_ref, pli_ref, d_ref, wg_ref, bg_ref,
               o_ref, lp, st, yv, carry, *, ls):
    sc = S5_SC
    n_cs = l8r_ref.shape[1]
    stride = sc * ls
    cre = [slice(LANES * k, LANES * (k + 1)) for k in range(n_cs)]
    cim = [slice(LANES * (n_cs + k), LANES * (n_cs + k + 1)) for k in range(n_cs)]
    lanes = [slice(LANES * q, LANES * (q + 1)) for q in range(sc)]

    @pl.when(pl.program_id(2) == 0)
    def _():
        carry[...] = jnp.zeros(carry.shape, F32)

    def gather_body(tau, c):
        for sg in range(sc):
            lp[_rows(tau), lanes[sg]] = u_ref[0, 0, pl.ds(tau * sc + sg, SUBLANES, stride=stride), :]
        return c

    lax.fori_loop(0, ls, gather_body, 0)
    st[...] = jnp.dot(lp[...].astype(BF16), bend_ref[0], preferred_element_type=F32)

    lam = [(jnp.broadcast_to(l8r_ref[0, k], (SUBLANES, LANES)),
            jnp.broadcast_to(l8i_ref[0, k], (SUBLANES, LANES))) for k in range(n_cs)]

    def step(tau, k, sr, si):
        lr, li = lam[k]
        return (lr * sr - li * si + st[_rows(tau), cre[k]], lr * si + li * sr + st[_rows(tau), cim[k]])

    def pass1(tau, c):
        out = []
        for k in range(n_cs):
            out += list(step(tau, k, c[2 * k], c[2 * k + 1]))
        return tuple(out)

    zero = jnp.zeros((SUBLANES, LANES), F32)
    ends = lax.fori_loop(0, ls, pass1, (zero,) * (2 * n_cs))

    starts = []
    for k in range(n_cs):
        er, ei = ends[2 * k], ends[2 * k + 1]
        pr, pi = plr_ref[0, k], pli_ref[0, k]
        c_r, c_i = carry[0:1, cre[k]], carry[0:1, cim[k]]
        rows_r, rows_i = [], []
        for j in range(SUBLANES):
            rows_r.append(c_r)
            rows_i.append(c_i)
            c_r, c_i = (pr * c_r - pi * c_i + er[j:j + 1], pr * c_i + pi * c_r + ei[j:j + 1])
        carry[0:1, cre[k]] = c_r
        carry[0:1, cim[k]] = c_i
        starts += [jnp.concatenate(rows_r, axis=0), jnp.concatenate(rows_i, axis=0)]

    def pass2(tau, c):
        out = []
        for k in range(n_cs):
            nr, ni = step(tau, k, c[2 * k], c[2 * k + 1])
            st[_rows(tau), cre[k]] = c[2 * k]
            st[_rows(tau), cim[k]] = c[2 * k + 1]
            out += [nr, ni]
        return tuple(out)

    lax.fori_loop(0, ls, pass2, tuple(starts))

    lhs = jnp.concatenate([lp[...], st[...]], axis=1).astype(BF16)
    yv[...] = jnp.dot(lhs, kc_ref[0], preferred_element_type=F32)
    for q in range(sc):
        y = yv[:, lanes[q]] + d_ref[0] * lp[:, lanes[q]]
        y = _gelu_tanh(y)
        yv[:, lanes[q]] = y * _sigmoid(jnp.dot(y.astype(BF16), wg_ref[0], preferred_element_type=F32)
                                       + bg_ref[0])

    def scatter_body(tau, c):
        for sg in range(sc):
            o_ref[0, 0, pl.ds(tau * sc + sg, SUBLANES, stride=stride), :] = yv[_rows(tau), lanes[sg]]
        return c

    lax.fori_loop(0, ls, scatter_body, 0)


def s5(proj, first_slab, bend, kc, l8r, l8i, plr, pli, d, wg, bg, *, ls):
    b, _, tp, _ = proj.shape
    nblk, kin, nst = bend.shape
    n_cs = nst // (2 * LANES)
    tt = SUBLANES * ls * S5_SC
    nc = SUBLANES * ls
    per_blk = lambda shape: pl.BlockSpec((1,) + shape, lambda bi, gb, t: (gb,) + (0,) * len(shape))
    return pl.pallas_call(
        functools.partial(_s5_kernel, ls=ls),
        grid=(b, nblk, tp // tt),
        in_specs=[pl.BlockSpec((1, 1, tt, LANES), lambda bi, gb, t: (bi, first_slab + gb, t, 0)),
                  per_blk((kin, nst)), per_blk((kin + nst, kin)),
                  per_blk((n_cs, 1, LANES)), per_blk((n_cs, 1, LANES)),
                  per_blk((n_cs, 1, LANES)), per_blk((n_cs, 1, LANES)),
                  per_blk((1, LANES)), per_blk((LANES, LANES)), per_blk((1, LANES))],
        out_specs=pl.BlockSpec((1, 1, tt, LANES), lambda bi, gb, t: (bi, gb, t, 0)),
        out_shape=jax.ShapeDtypeStruct((b, nblk, tp, LANES), F32),
        scratch_shapes=[pltpu.VMEM((nc, kin), F32), pltpu.VMEM((nc, nst), F32),
                        pltpu.VMEM((nc, kin), F32), pltpu.VMEM((SUBLANES, nst), F32)],
        compiler_params=_cparams(3),
        name="s5",
    )(proj, bend, kc, l8r, l8i, plr, pli, d, wg, bg)


def _s5_params(a_re, a_im, b_re, b_im, c_re, c_im, d, log_dt, w_glu, b_glu, *, ls):
    g, p = a_re.shape
    sc = S5_SC
    gpb = LANES // S5_GROUP
    nblk = g // gpb
    dt = jnp.exp(log_dt)[:, None]
    mag = jnp.exp(a_re * dt)
    ar, ai = mag * jnp.cos(a_im * dt), mag * jnp.sin(a_im * dt)
    den = a_re * a_re + a_im * a_im
    fr = ((ar - 1.0) * a_re + ai * a_im) / den
    fi = (ai * a_re - (ar - 1.0) * a_im) / den
    bbar_re = fr[..., None] * b_re - fi[..., None] * b_im
    bbar_im = fr[..., None] * b_im + fi[..., None] * b_re

    def lam_pow(k):
        k = jnp.asarray(k, F32).reshape(-1, 1, 1)
        m = jnp.exp(a_re * dt * k)
        return m * jnp.cos(a_im * dt * k), m * jnp.sin(a_im * dt * k)

    pw_r, pw_i = lam_pow(jnp.arange(sc + 1))
    eye = jnp.eye(gpb, dtype=F32)
    blk = lambda m: m.reshape(m.shape[:-3] + (nblk, gpb) + m.shape[-2:])

    wr = jnp.stack([pw_r[sc - 1 - s] for s in range(sc)])[..., None]
    wi = jnp.stack([pw_i[sc - 1 - s] for s in range(sc)])[..., None]
    e_re = wr * bbar_re - wi * bbar_im
    e_im = wr * bbar_im + wi * bbar_re
    end_bd = lambda m: jnp.einsum('sbqpj,qr->bsqjrp', blk(m), eye).reshape(
        nblk, sc * gpb * S5_GROUP, gpb * p)
    bend = jnp.concatenate([end_bd(e_re), end_bd(e_im)], axis=-1).astype(BF16)

    cl_re = c_re[None] * pw_r[:, :, None, :] - c_im[None] * pw_i[:, :, None, :]
    cl_im = c_re[None] * pw_i[:, :, None, :] + c_im[None] * pw_r[:, :, None, :]
    kl = (jnp.einsum('kgip,gpj->kgij', cl_re[:sc], bbar_re)
          - jnp.einsum('kgip,gpj->kgij', cl_im[:sc], bbar_im))
    kst = jnp.stack([jnp.stack([kl[t - s] if t >= s else jnp.zeros_like(kl[0]) for t in range(sc)])
                     for s in range(sc)])
    kintra = jnp.einsum('stbqij,qr->bsqjtri', blk(kst), eye).reshape(
        nblk, sc * gpb * S5_GROUP, sc * gpb * S5_GROUP)
    out_bd = lambda m: jnp.einsum('tbqip,qr->bqptri', blk(m), eye).reshape(
        nblk, gpb * p, sc * gpb * S5_GROUP)
    kc = jnp.concatenate([kintra, out_bd(cl_re[1:]), -out_bd(cl_im[1:])], axis=1).astype(BF16)

    vec = lambda v: v.reshape(nblk, (gpb * p) // LANES, 1, LANES)
    pl_r, pl_i = lam_pow(jnp.asarray([sc * ls]))
    wg = jnp.einsum('bqij,qr->bqirj', w_glu.reshape(nblk, gpb, S5_GROUP, S5_GROUP), eye)
    wg = wg.reshape(nblk, LANES, LANES).astype(BF16)
    return (bend, kc, vec(pw_r[sc]), vec(pw_i[sc]), vec(pl_r[0]), vec(pl_i[0]),
            d.reshape(nblk, 1, LANES), wg, b_glu.reshape(nblk, 1, LANES))


def _ssd_kernel(xbc_ref, z_ref, dt_ref, cw_ref, cb_ref, dtb_ref, alog_ref, dskip_ref, nw_ref, o_ref,
                hist, hst):
    L = SSD_CHUNK
    n_groups = hst.shape[0]
    gw = SSD_HPG * SSD_HEAD_DIM
    spg = gw // LANES
    n_xs = n_groups * spg

    @pl.when(pl.program_id(1) == 0)
    def _():
        hist[:, 0:SUBLANES, :] = jnp.zeros((hist.shape[0], SUBLANES, LANES), F32)
        hst[...] = jnp.zeros(hst.shape, F32)

    def conv_silu(s):
        hist[s, SUBLANES:2 * SUBLANES, :] = xbc_ref[0, s, 0:SUBLANES, :]
        sl = slice(LANES * s, LANES * (s + 1))
        taps = [cw_ref[m, :, sl] for m in range(CONV_WIDTH)]
        bias = cb_ref[:, sl]
        outs = []
        for i in range(L // SUBLANES):
            acc = bias
            for m in range(CONV_WIDTH):
                if i == 0:
                    xv = hist[s, SUBLANES - m:2 * SUBLANES - m, :]
                else:
                    xv = xbc_ref[0, s, SUBLANES * i - m:SUBLANES * (i + 1) - m, :]
                acc = acc + taps[m] * xv
            outs.append(_silu(acc))
        hist[s, 0:SUBLANES, :] = xbc_ref[0, s, L - SUBLANES:L, :]
        return jnp.concatenate(outs, axis=0)

    dt = _softplus(dt_ref[0, 0] + dtb_ref[...])
    da = dt * (-jnp.exp(alog_ref[...]))
    row = lax.broadcasted_iota(jnp.int32, (L, L), 0)
    col = lax.broadcasted_iota(jnp.int32, (L, L), 1)
    causal = row >= col
    da_hi = da.astype(BF16)
    r1 = da - da_hi.astype(F32)
    da_mid = r1.astype(BF16)
    da_lo = (r1 - da_mid.astype(F32)).astype(BF16)
    tri = causal.astype(BF16)
    cs = (jnp.dot(tri, da_hi, preferred_element_type=F32) + jnp.dot(tri, da_mid, preferred_element_type=F32)
          + jnp.dot(tri, da_lo, preferred_element_type=F32))
    dt_t = dt.T
    cs_t = cs.T
    last_t = jnp.broadcast_to(cs_t[:, L - 1:L], (L, L))
    w_end_t = dt_t * jnp.exp(last_t - cs_t)
    c_dec_t = jnp.exp(last_t)
    rowp = cs_t - jnp.log(dt_t)
    left_half = col < SSD_HEAD_DIM

    def head_rows(m, g):
        return jnp.concatenate(
            [jnp.broadcast_to(m[SSD_HPG * g + r:SSD_HPG * g + r + 1, :], (SSD_HEAD_DIM, L))
             for r in range(SSD_HPG)], axis=0)

    for g in range(n_groups):
        xs = jnp.concatenate([conv_silu(spg * g + k) for k in range(spg)], axis=-1)
        bm = conv_silu(n_xs + g).astype(BF16)
        cm = conv_silu(n_xs + n_groups + g).astype(BF16)
        scores = lax.dot_general(cm, bm, (((1,), (1,)), ((), ())), preferred_element_type=F32)
        xw_t = (xs.T * head_rows(w_end_t, g)).astype(BF16)
        st_new = jnp.dot(xw_t, bm, preferred_element_type=F32)
        h_prev = hst[g]
        y_off = lax.dot_general(cm, h_prev.astype(BF16), (((1,), (1,)), ((), ())),
                                preferred_element_type=F32)
        hst[g] = head_rows(c_dec_t, g) * h_prev + st_new
        xsb = xs.astype(BF16)
        ys, bcs = [], []
        for r in range(SSD_HPG):
            h = SSD_HPG * g + r
            bcs.append(jnp.broadcast_to(cs[:, h:h + 1], (L, L)))
            m = (scores * jnp.exp(jnp.where(causal, bcs[r] - rowp[h:h + 1, :], -jnp.inf))).astype(BF16)
            ys.append(jnp.dot(m, xsb[:, SSD_HEAD_DIM * r:SSD_HEAD_DIM * (r + 1)],
                              preferred_element_type=F32))
        f_start = jnp.concatenate([jnp.exp(jnp.where(left_half, bcs[2 * k], bcs[2 * k + 1]))
                                   for k in range(spg)], axis=-1)
        y = jnp.concatenate(ys, axis=-1) + y_off * f_start + dskip_ref[:, gw * g:gw * (g + 1)] * xs
        zg = jnp.concatenate([z_ref[0, spg * g + k] for k in range(spg)], axis=-1)
        gg = y * _silu(zg)
        ms = jnp.mean(gg * gg, axis=-1, keepdims=True)
        o_ref[0, :, gw * g:gw * (g + 1)] = (gg * lax.rsqrt(ms + NORM_EPS)
                                            * nw_ref[:, gw * g:gw * (g + 1)]).astype(o_ref.dtype)


def ssd_core(proj, cw, cb, dtb, alog, dskip, nw, *, d_inner, conv_dim):
    b, _, tp, _ = proj.shape
    n_groups = d_inner // (SSD_HPG * SSD_HEAD_DIM)
    L = SSD_CHUNK
    n_conv = conv_dim // LANES
    n_z = d_inner // LANES
    cw8 = jnp.broadcast_to(jnp.stack([cw[CONV_WIDTH - 1 - m] for m in range(CONV_WIDTH)])[:, None, :],
                           (CONV_WIDTH, SUBLANES, conv_dim))
    cb8 = jnp.broadcast_to(cb.reshape(1, conv_dim), (SUBLANES, conv_dim))
    const = lambda shape: pl.BlockSpec(shape, lambda bi, c: (0,) * len(shape))
    return pl.pallas_call(
        _ssd_kernel,
        grid=(b, tp // L),
        in_specs=[pl.BlockSpec((1, n_conv, L, LANES), lambda bi, c: (bi, 0, c, 0)),
                  pl.BlockSpec((1, n_z, L, LANES), lambda bi, c: (bi, n_conv // n_z, c, 0)),
                  pl.BlockSpec((1, 1, L, LANES), lambda bi, c: (bi, n_conv + n_z, c, 0)),
                  const((CONV_WIDTH, SUBLANES, conv_dim)), const((SUBLANES, conv_dim)),
                  const((1, LANES)), const((1, LANES)), const((1, d_inner)), const((1, d_inner))],
        out_specs=pl.BlockSpec((1, L, d_inner), lambda bi, c: (bi, c, 0)),
        out_shape=jax.ShapeDtypeStruct((b, tp, d_inner), BF16),
        scratch_shapes=[pltpu.VMEM((n_conv, 2 * SUBLANES, LANES), F32),
                        pltpu.VMEM((n_groups, SSD_HPG * SSD_HEAD_DIM, SSD_STATE), F32)],
        compiler_params=_cparams(2),
        name="ssd_core",
    )(proj, proj, proj, cw8, cb8, dtb, alog, dskip, nw)


def _block_diag(w):
    n, di, do = w.shape
    return jnp.einsum('gij,gh->gihj', w, jnp.eye(n, dtype=w.dtype)).reshape(n * di, n * do)


def _pad_lanes(v, fill=0.0):
    return jnp.pad(v, (0, LANES - v.shape[0]), constant_values=fill).reshape(1, LANES)


def kernel(x, meta_tokens, norm_mix, norm_mlp, norm_final, ev_w_in, lru_conv_w, lru_conv_b, lru_w_a, lru_b_a, lru_w_x, lru_b_x, lru_lambda, s5_a_re, s5_a_im, s5_b_re, s5_b_im, s5_c_re, s5_c_im, s5_d, s5_log_dt, s5_w_glu, s5_b_glu, ev_w_out, ssd_w_in, ssd_conv_w, ssd_conv_b, ssd_dt_bias, ssd_a_log, ssd_d, ssd_norm, ssd_w_out, mlp_w_up, mlp_w_down):
    bsz, seq, d = x.shape
    t = seq + N_META
    unit = math.lcm(SUBLANES * LRU_LS, SUBLANES * S5_LS * S5_SC, ROW_TILE, SSD_CHUNK)
    tp = -(-t // unit) * unit
    meta = jnp.broadcast_to(meta_tokens[None].astype(x.dtype), (bsz, N_META, d))
    h = jnp.concatenate([meta, x, jnp.zeros((bsz, tp - t, d), x.dtype)], axis=1)

    lru_w = lru_conv_w.shape[-1]
    proj = norm_matmul(h, norm_mix[0], ev_w_in[0].astype(BF16), tm=ROW_TILE, tn=512)
    y_lru = rglru(proj, lru_conv_w[0], lru_conv_b[0], _block_diag(lru_w_a[0]).astype(BF16), lru_b_a[0],
                  _block_diag(lru_w_x[0]).astype(BF16), lru_b_x[0], lru_lambda[0], ls=LRU_LS)
    y_s5 = s5(proj, 2 * lru_w // LANES,
              *_s5_params(s5_a_re[0], s5_a_im[0], s5_b_re[0], s5_b_im[0], s5_c_re[0], s5_c_im[0],
                          s5_d[0], s5_log_dt[0], s5_w_glu[0], s5_b_glu[0], ls=S5_LS), ls=S5_LS)
    h = outproj_slab(h, y_lru, y_s5, ev_w_out[0].astype(BF16), tm=ROW_TILE)
    h = mlp_residual(h.reshape(bsz * tp, d), norm_mlp[0], mlp_w_up[0].astype(BF16),
                     mlp_w_down[0].astype(BF16), norm_final, tm=ROW_TILE, tf=512,
                     final_norm=False).reshape(bsz, tp, d)

    d_inner = ssd_w_out.shape[1]
    conv_dim = ssd_conv_w.shape[-1]
    n_heads = ssd_dt_bias.shape[-1]
    w_in = ssd_w_in[0]
    w_cat = jnp.concatenate([w_in[:, d_inner:d_inner + conv_dim], w_in[:, :d_inner],
                             w_in[:, d_inner + conv_dim:],
                             jnp.zeros((d, 2 * LANES - n_heads), F32)], axis=1).astype(BF16)
    proj = norm_matmul(h, norm_mix[1], w_cat, tm=ROW_TILE, tn=w_cat.shape[1] // 5)
    g = ssd_core(proj, ssd_conv_w[0], ssd_conv_b[0], _pad_lanes(ssd_dt_bias[0]),
                 _pad_lanes(ssd_a_log[0]), jnp.repeat(ssd_d[0], SSD_HEAD_DIM).reshape(1, d_inner),
                 ssd_norm[0].reshape(1, d_inner), d_inner=d_inner, conv_dim=conv_dim)
    h = outproj(h, g, ssd_w_out[0].astype(BF16), tm=ROW_TILE)
    h = mlp_residual(h.reshape(bsz * tp, d), norm_mlp[1], mlp_w_up[1].astype(BF16),
                     mlp_w_down[1].astype(BF16), norm_final, tm=ROW_TILE, tf=512,
                     final_norm=True).reshape(bsz, tp, d)
    return h[:, N_META:t]
```

```python
import functools
import math

import jax
import jax.numpy as jnp
from jax import lax
from jax.experimental import pallas as pl
from jax.experimental.pallas import tpu as pltpu

F32 = jnp.float32
BF16 = jnp.bfloat16

LANES = 128
SUBLANES = 8
NORM_EPS = 1e-5
LOG2_E = 1.4426950408889634
N_META = 16
CONV_WIDTH = 4
LRU_C = 8.0
S5_GROUP = 16
S5_STATE = 64
SSD_HEAD_DIM = 64
SSD_STATE = 128
SSD_CHUNK = 128
SSD_HPG = 4
SSD_CHUNKS_PER_STEP = 1
SSD_PREP_CHUNKS = 5

LRU_LS = 130
S5_SC = SUBLANES
S5_LS = 65
ROW_TILE = 1280
FINAL_ROW_TILE = 1024
VMEM_LIMIT = 48 * 1024 * 1024


def _cparams(n_axes, flags=None):
    return pltpu.CompilerParams(dimension_semantics=("arbitrary",) * n_axes,
                                vmem_limit_bytes=VMEM_LIMIT, flags=flags)


def _sigmoid(x):
    return 1.0 / (1.0 + jnp.exp(-x))


def _silu(x):
    hx = 0.5 * x
    return hx + hx * jnp.tanh(hx)


def _gelu_tanh(x):
    return 0.5 * x * (1.0 + jnp.tanh(math.sqrt(2.0 / math.pi) * (x + 0.044715 * (x * x * x))))


def _log1p(e):
    u = 1.0 + e
    return jnp.where(u == 1.0, e, jnp.log(u) * (e / (u - 1.0)))


def _softplus(x):
    return jnp.maximum(x, 0.0) + _log1p(jnp.exp(-jnp.abs(x)))


def _rmsnorm_rows(x, w):
    ms = jnp.mean(x * x, axis=-1, keepdims=True)
    return x * lax.rsqrt(ms + NORM_EPS) * w


def _rows(tau):
    return pl.ds(pl.multiple_of(tau * SUBLANES, SUBLANES), SUBLANES)


def _norm_matmul_kernel(x_ref, nw_ref, w_ref, o_ref, xn_ref):
    @pl.when(pl.program_id(2) == 0)
    def _():
        xn_ref[...] = _rmsnorm_rows(x_ref[0], nw_ref[...]).astype(BF16)

    r = jnp.dot(xn_ref[...], w_ref[...], preferred_element_type=F32)
    for s in range(o_ref.shape[1]):
        o_ref[0, s] = r[:, LANES * s:LANES * (s + 1)]


def norm_matmul(h, nw, w, *, tm, tn):
    b, tp, d = h.shape
    n = w.shape[1]
    return pl.pallas_call(
        _norm_matmul_kernel,
        grid=(b, tp // tm, n // tn),
        in_specs=[pl.BlockSpec((1, tm, d), lambda bi, i, j: (bi, i, 0)),
                  pl.BlockSpec((1, d), lambda bi, i, j: (0, 0)),
                  pl.BlockSpec((d, tn), lambda bi, i, j: (0, j))],
        out_specs=pl.BlockSpec((1, tn // LANES, tm, LANES), lambda bi, i, j: (bi, j, i, 0)),
        out_shape=jax.ShapeDtypeStruct((b, n // LANES, tp, LANES), F32),
        scratch_shapes=[pltpu.VMEM((tm, d), BF16)],
        compiler_params=_cparams(3),
        name="norm_matmul",
    )(h, nw.reshape(1, d), w)


def _inproj_conv_kernel(x_ref, nw_ref, w_ref, cw_ref, cb_ref, o_ref, xn_ref, rs, hist):
    i, j = pl.program_id(1), pl.program_id(2)
    tm = x_ref.shape[1]
    n_slab = o_ref.shape[1]
    hrows = SUBLANES
    pack = 2 * SUBLANES

    @pl.when(j == 0)
    def _():
        xn_ref[...] = _rmsnorm_rows(x_ref[0], nw_ref[...]).astype(BF16)

    @pl.when(i == 0)
    def _():
        hist[j] = jnp.zeros(hist.shape[1:], F32)

    xn = xn_ref[...]
    proj = lambda c: jnp.dot(xn, w_ref[:, 2 * LANES * c:2 * LANES * (c + 1)], preferred_element_type=F32)
    r_next = proj(0)
    for c in range(n_slab // 2):
        r = r_next
        if c + 1 < n_slab // 2:
            r_next = proj(c + 1)
        for k in range(2):
            s = 2 * c + k
            sl = slice(LANES * s, LANES * (s + 1))
            rs[s, 0:hrows, :] = hist[j, s]
            rs[s, hrows:hrows + tm, :] = r[:, LANES * k:LANES * (k + 1)]
            hist[j, s] = rs[s, tm:tm + hrows, :]
            taps = [cw_ref[m, :, sl] for m in range(CONV_WIDTH)]
            bias = cb_ref[:, sl]
            for g in range(tm // pack):
                halves = []
                for hh in range(pack // SUBLANES):
                    r0 = hrows + pack * g + SUBLANES * hh
                    acc = bias
                    for m in range(CONV_WIDTH):
                        acc = acc + taps[m] * rs[s, r0 - m:r0 - m + SUBLANES, :]
                    halves.append(_silu(acc))
                o_ref[0, s, pack * g:pack * (g + 1), :] = jnp.concatenate(halves, axis=0).astype(BF16)


def inproj_conv(h, nw, w, cw, cb, *, tm, tn):
    b, tp, d = h.shape
    n = w.shape[1]
    ns = tn // LANES
    cw8 = jnp.broadcast_to(jnp.stack([cw[CONV_WIDTH - 1 - m] for m in range(CONV_WIDTH)])[:, None, :],
                           (CONV_WIDTH, SUBLANES, n))
    cb8 = jnp.broadcast_to(cb.reshape(1, n), (SUBLANES, n))
    return pl.pallas_call(
        _inproj_conv_kernel,
        grid=(b, tp // tm, n // tn),
        in_specs=[pl.BlockSpec((1, tm, d), lambda bi, i, j: (bi, i, 0)),
                  pl.BlockSpec((1, d), lambda bi, i, j: (0, 0)),
                  pl.BlockSpec((d, tn), lambda bi, i, j: (0, j)),
                  pl.BlockSpec((CONV_WIDTH, SUBLANES, tn), lambda bi, i, j: (0, 0, j)),
                  pl.BlockSpec((SUBLANES, tn), lambda bi, i, j: (0, j))],
        out_specs=pl.BlockSpec((1, ns, tm, LANES), lambda bi, i, j: (bi, j, i, 0)),
        out_shape=jax.ShapeDtypeStruct((b, n // LANES, tp, LANES), BF16),
        scratch_shapes=[pltpu.VMEM((tm, d), BF16),
                        pltpu.VMEM((ns, SUBLANES + tm, LANES), F32),
                        pltpu.VMEM((n // tn, ns, SUBLANES, LANES), F32)],
        compiler_params=_cparams(3),
        name="inproj_conv",
    )(h, nw.reshape(1, d), w, cw8, cb8)


def _inproj_zdt_kernel(x_ref, nw_ref, w_ref, z_ref, dt_ref):
    xn = _rmsnorm_rows(x_ref[0], nw_ref[...]).astype(BF16)
    n_z = z_ref.shape[1]
    step = 4
    for c in range(n_z // step):
        r = jnp.dot(xn, w_ref[:, LANES * step * c:LANES * step * (c + 1)], preferred_element_type=F32)
        for k in range(step):
            z_ref[0, step * c + k] = r[:, LANES * k:LANES * (k + 1)].astype(BF16)
    dt_ref[0, 0] = jnp.dot(xn, w_ref[:, LANES * n_z:LANES * (n_z + 1)], preferred_element_type=F32)


def inproj_zdt(h, nw, w, *, tm):
    b, tp, d = h.shape
    n_z = w.shape[1] // LANES - 1
    return pl.pallas_call(
        _inproj_zdt_kernel,
        grid=(b, tp // tm),
        in_specs=[pl.BlockSpec((1, tm, d), lambda bi, i: (bi, i, 0)),
                  pl.BlockSpec((1, d), lambda bi, i: (0, 0)),
                  pl.BlockSpec(w.shape, lambda bi, i: (0, 0))],
        out_specs=[pl.BlockSpec((1, n_z, tm, LANES), lambda bi, i: (bi, 0, i, 0)),
                   pl.BlockSpec((1, 1, tm, LANES), lambda bi, i: (bi, 0, i, 0))],
        out_shape=[jax.ShapeDtypeStruct((b, n_z, tp, LANES), BF16),
                   jax.ShapeDtypeStruct((b, 1, tp, LANES), F32)],
        compiler_params=_cparams(2),
        name="inproj_zdt",
    )(h, nw.reshape(1, d), w)


def _mlp_kernel(x_ref, nw_ref, wu_ref, wd_ref, fw_ref, o_ref, xn_ref, *, final_norm):
    j = pl.program_id(1)

    @pl.when(j == 0)
    def _():
        x = x_ref[...]
        xn_ref[...] = _rmsnorm_rows(x, nw_ref[...]).astype(BF16)
        o_ref[...] = x

    u = jnp.dot(xn_ref[...], wu_ref[...], preferred_element_type=F32)
    a = jnp.square(jnp.maximum(u, 0.0)).astype(BF16)
    o_ref[...] += jnp.dot(a, wd_ref[...], preferred_element_type=F32)

    if final_norm:
        @pl.when(j == pl.num_programs(1) - 1)
        def _():
            o_ref[...] = _rmsnorm_rows(o_ref[...], fw_ref[...])


def mlp_residual(h2, nw, w_up, w_down, fw, *, tm, tf, final_norm):
    n, d = h2.shape
    dff = w_up.shape[1]
    return pl.pallas_call(
        functools.partial(_mlp_kernel, final_norm=final_norm),
        grid=(n // tm, dff // tf),
        in_specs=[pl.BlockSpec((tm, d), lambda i, j: (i, 0)),
                  pl.BlockSpec((1, d), lambda i, j: (0, 0)),
                  pl.BlockSpec((d, tf), lambda i, j: (0, j)),
                  pl.BlockSpec((tf, d), lambda i, j: (j, 0)),
                  pl.BlockSpec((1, d), lambda i, j: (0, 0))],
        out_specs=pl.BlockSpec((tm, d), lambda i, j: (i, 0)),
        out_shape=jax.ShapeDtypeStruct((n, d), F32),
        scratch_shapes=[pltpu.VMEM((tm, d), BF16)],
        compiler_params=_cparams(2),
        name="mlp_final" if final_norm else "mlp",
    )(h2, nw.reshape(1, d), w_up, w_down, fw.reshape(1, d))


def _mlp_final_kernel(x_ref, nw_ref, wu_ref, wd_ref, fw_ref, o_ref, xn_ref):
    j = pl.program_id(2)

    @pl.when(j == 0)
    def _():
        x = x_ref[0]
        xn_ref[...] = _rmsnorm_rows(x, nw_ref[...]).astype(BF16)
        o_ref[0] = x

    u = jnp.dot(xn_ref[...], wu_ref[...], preferred_element_type=F32)
    a = jnp.square(jnp.maximum(u, 0.0)).astype(BF16)
    o_ref[0] += jnp.dot(a, wd_ref[...], preferred_element_type=F32)

    @pl.when(j == pl.num_programs(2) - 1)
    def _():
        o_ref[0] = _rmsnorm_rows(o_ref[0], fw_ref[...])


def mlp_residual_final(h, nw, w_up, w_down, fw, *, seq, tm, tf):
    b, _, d = h.shape
    dff = w_up.shape[1]
    return pl.pallas_call(
        _mlp_final_kernel,
        grid=(b, seq // tm, dff // tf),
        in_specs=[pl.BlockSpec((pl.Element(1), pl.Element(tm), pl.Element(d)),
                               lambda bi, i, j: (bi, pl.multiple_of(N_META + i * tm, SUBLANES), 0)),
                  pl.BlockSpec((1, d), lambda bi, i, j: (0, 0)),
                  pl.BlockSpec((d, tf), lambda bi, i, j: (0, j)),
                  pl.BlockSpec((tf, d), lambda bi, i, j: (j, 0)),
                  pl.BlockSpec((1, d), lambda bi, i, j: (0, 0))],
        out_specs=pl.BlockSpec((1, tm, d), lambda bi, i, j: (bi, i, 0)),
        out_shape=jax.ShapeDtypeStruct((b, seq, d), F32),
        scratch_shapes=[pltpu.VMEM((tm, d), BF16)],
        compiler_params=_cparams(3),
        name="mlp_final",
    )(h, nw.reshape(1, d), w_up, w_down, fw.reshape(1, d))


def _outproj_slab_kernel(h_ref, ya_ref, yb_ref, w_ref, o_ref):
    parts = [ya_ref[0, s] for s in range(ya_ref.shape[1])] + [yb_ref[0, s] for s in range(yb_ref.shape[1])]
    y = jnp.concatenate(parts, axis=-1).astype(BF16)
    o_ref[0] = h_ref[0] + jnp.dot(y, w_ref[...], preferred_element_type=F32)


def outproj_slab(h, ya, yb, w, *, tm):
    b, tp, d = h.shape
    sa, sb = ya.shape[1], yb.shape[1]
    return pl.pallas_call(
        _outproj_slab_kernel,
        grid=(b, tp // tm),
        in_specs=[pl.BlockSpec((1, tm, d), lambda bi, i: (bi, i, 0)),
                  pl.BlockSpec((1, sa, tm, LANES), lambda bi, i: (bi, 0, i, 0)),
                  pl.BlockSpec((1, sb, tm, LANES), lambda bi, i: (bi, 0, i, 0)),
                  pl.BlockSpec(w.shape, lambda bi, i: (0, 0))],
        out_specs=pl.BlockSpec((1, tm, d), lambda bi, i: (bi, i, 0)),
        out_shape=jax.ShapeDtypeStruct(h.shape, F32),
        compiler_params=_cparams(2),
        name="outproj_slab",
    )(h, ya, yb, w)


def _outproj_kernel(h_ref, y_ref, w_ref, o_ref):
    o_ref[0] = h_ref[0] + jnp.dot(y_ref[0], w_ref[...], preferred_element_type=F32)


def outproj(h, y, w, *, tm):
    b, tp, d = h.shape
    k = y.shape[-1]
    return pl.pallas_call(
        _outproj_kernel,
        grid=(b, tp // tm),
        in_specs=[pl.BlockSpec((1, tm, d), lambda bi, i: (bi, i, 0)),
                  pl.BlockSpec((1, tm, k), lambda bi, i: (bi, i, 0)),
                  pl.BlockSpec(w.shape, lambda bi, i: (0, 0))],
        out_specs=pl.BlockSpec((1, tm, d), lambda bi, i: (bi, i, 0)),
        out_shape=jax.ShapeDtypeStruct(h.shape, F32),
        compiler_params=_cparams(2),
        name="outproj",
    )(h, y, w)


def _rglru_kernel(x_ref, g_ref, cw_ref, cb_ref, wa_ref, ba_ref, wx_ref, bx_ref, lam_ref, o_ref,
                  xbuf, xcp, a_s, b_s, hn, carry, *, ls):
    n_slab = x_ref.shape[1]
    tt = SUBLANES * ls
    hist = SUBLANES
    slabs = [slice(LANES * s, LANES * (s + 1)) for s in range(n_slab)]

    @pl.when(pl.program_id(1) == 0)
    def _():
        xbuf[:, 0:hist, :] = jnp.zeros((n_slab, hist, LANES), F32)
        carry[...] = jnp.zeros(carry.shape, F32)

    for s in range(n_slab):
        xbuf[s, hist:hist + tt, :] = x_ref[0, s]

    def conv_body(tau, c):
        for s in range(n_slab):
            acc = cb_ref[:, slabs[s]]
            for m in range(CONV_WIDTH):
                acc = acc + cw_ref[m, :, slabs[s]] * xbuf[s, pl.ds(hist + tau - m, SUBLANES, stride=ls), :]
            xcp[_rows(tau), slabs[s]] = acc
        return c

    lax.fori_loop(0, ls, conv_body, 0, unroll=2)
    for s in range(n_slab):
        xbuf[s, 0:hist, :] = xbuf[s, tt:tt + hist, :]

    xc = xcp[...]
    xb = xc.astype(BF16)
    r = _sigmoid(jnp.dot(xb, wa_ref[...], preferred_element_type=F32) + ba_ref[...])
    i = _sigmoid(jnp.dot(xb, wx_ref[...], preferred_element_type=F32) + bx_ref[...])
    log_a = (-LRU_C) * r * _softplus(-lam_ref[...])
    a = jnp.exp(log_a)
    a_s[...] = a
    b_s[...] = jnp.sqrt(-jnp.tanh(log_a) * (a * a + 1.0)) * (i * xc)

    def pass1(tau, c):
        out = []
        for s in range(n_slab):
            p, e = c[2 * s], c[2 * s + 1]
            av = a_s[_rows(tau), slabs[s]]
            out += [p * av, av * e + b_s[_rows(tau), slabs[s]]]
        return tuple(out)

    one = jnp.ones((SUBLANES, LANES), F32)
    zero = jnp.zeros((SUBLANES, LANES), F32)
    pe = lax.fori_loop(0, ls, pass1, (one, zero) * n_slab, unroll=2)

    h0 = []
    for s in range(n_slab):
        p, e = pe[2 * s], pe[2 * s + 1]
        c = carry[0:1, slabs[s]]
        rows = []
        for j in range(SUBLANES):
            rows.append(c)
            c = p[j:j + 1] * c + e[j:j + 1]
        carry[0:1, slabs[s]] = c
        h0.append(jnp.concatenate(rows, axis=0))

    def pass2(tau, hs):
        out = []
        for s in range(n_slab):
            h = a_s[_rows(tau), slabs[s]] * hs[s] + b_s[_rows(tau), slabs[s]]
            hn[s, pl.ds(tau, SUBLANES, stride=ls), :] = h
            out.append(h)
        return tuple(out)

    lax.fori_loop(0, ls, pass2, tuple(h0), unroll=2)
    for s in range(n_slab):
        o_ref[0, s] = hn[s] * _gelu_tanh(g_ref[0, s])


def rglru(proj, cw, cb, wa_bd, ba, wx_bd, bx, lam, *, ls):
    b, _, tp, _ = proj.shape
    w = cw.shape[1]
    ns = w // LANES
    tt = SUBLANES * ls
    cw8 = jnp.broadcast_to(jnp.stack([cw[CONV_WIDTH - 1 - m] for m in range(CONV_WIDTH)])[:, None, :],
                           (CONV_WIDTH, SUBLANES, w))
    cb8 = jnp.broadcast_to(cb[None, :], (SUBLANES, w))
    const = lambda shape: pl.BlockSpec(shape, lambda bi, t: (0,) * len(shape))
    return pl.pallas_call(
        functools.partial(_rglru_kernel, ls=ls),
        grid=(b, tp // tt),
        in_specs=[pl.BlockSpec((1, ns, tt, LANES), lambda bi, t: (bi, 0, t, 0)),
                  pl.BlockSpec((1, ns, tt, LANES), lambda bi, t: (bi, 1, t, 0)),
                  const((CONV_WIDTH, SUBLANES, w)), const((SUBLANES, w)),
                  const((w, w)), const((1, w)), const((w, w)), const((1, w)), const((1, w))],
        out_specs=pl.BlockSpec((1, ns, tt, LANES), lambda bi, t: (bi, 0, t, 0)),
        out_shape=jax.ShapeDtypeStruct((b, ns, tp, LANES), F32),
        scratch_shapes=[pltpu.VMEM((ns, SUBLANES + tt, LANES), F32),
                        pltpu.VMEM((tt, w), F32), pltpu.VMEM((tt, w), F32), pltpu.VMEM((tt, w), F32),
                        pltpu.VMEM((ns, tt, LANES), F32), pltpu.VMEM((SUBLANES, w), F32)],
        compiler_params=_cparams(2),
        name="rglru",
    )(proj, proj, cw8, cb8, wa_bd, ba.reshape(1, w), wx_bd, bx.reshape(1, w), lam.reshape(1, w))


def _s5_kernel(u_ref, bend_ref, kc_ref, l8r_ref, l8i_ref, plr_ref, pli_ref, d_ref, wg_ref, bg_ref,
               o_ref, lp, st, yv, carry, *, ls):
    sc = S5_SC
    n_cs = l8r_ref.shape[1]
    stride = sc * ls
    cre = [slice(LANES * k, LANES * (k + 1)) for k in range(n_cs)]
    cim = [slice(LANES * (n_cs + k), LANES * (n_cs + k + 1)) for k in range(n_cs)]
    lanes = [slice(LANES * q, LANES * (q + 1)) for q in range(sc)]

    @pl.when(pl.program_id(2) == 0)
    def _():
        carry[...] = jnp.zeros(carry.shape, F32)

    def gather_body(tau, c):
        for sg in range(sc):
            lp[_rows(tau), lanes[sg]] = u_ref[0, 0, pl.ds(tau * sc + sg, SUBLANES, stride=stride), :]
        return c

    lax.fori_loop(0, ls, gather_body, 0)
    st[...] = jnp.dot(lp[...].astype(BF16), bend_ref[0], preferred_element_type=F32)

    lam = [(jnp.broadcast_to(l8r_ref[0, k], (SUBLANES, LANES)),
            jnp.broadcast_to(l8i_ref[0, k], (SUBLANES, LANES))) for k in range(n_cs)]

    def step(tau, k, sr, si):
        lr, li = lam[k]
        return (lr * sr - li * si + st[_rows(tau), cre[k]], lr * si + li * sr + st[_rows(tau), cim[k]])

    def pass1(tau, c):
        out = []
        for k in range(n_cs):
            out += list(step(tau, k, c[2 * k], c[2 * k + 1]))
        return tuple(out)

    zero = jnp.zeros((SUBLANES, LANES), F32)
    ends = lax.fori_loop(0, ls, pass1, (zero,) * (2 * n_cs))

    starts = []
    for k in range(n_cs):
        er, ei = ends[2 * k], ends[2 * k + 1]
        pr, pi = plr_ref[0, k], pli_ref[0, k]
        c_r, c_i = carry[0:1, cre[k]], carry[0:1, cim[k]]
        rows_r, rows_i = [], []
        for j in range(SUBLANES):
            rows_r.append(c_r)
            rows_i.append(c_i)
            c_r, c_i = (pr * c_r - pi * c_i + er[j:j + 1], pr * c_i + pi * c_r + ei[j:j + 1])
        carry[0:1, cre[k]] = c_r
        carry[0:1, cim[k]] = c_i
        starts += [jnp.concatenate(rows_r, axis=0), jnp.concatenate(rows_i, axis=0)]

    def pass2(tau, c):
        out = []
        for k in range(n_cs):
            nr, ni = step(tau, k, c[2 * k], c[2 * k + 1])
            st[_rows(tau), cre[k]] = c[2 * k]
            st[_rows(tau), cim[k]] = c[2 * k + 1]
            out += [nr, ni]
        return tuple(out)

    lax.fori_loop(0, ls, pass2, tuple(starts))

    lhs = jnp.concatenate([lp[...], st[...]], axis=1).astype(BF16)
    yv[...] = jnp.dot(lhs, kc_ref[0], preferred_element_type=F32)
    for q in range(sc):
        y = yv[:, lanes[q]] + d_ref[0] * lp[:, lanes[q]]
        y = _gelu_tanh(y)
        yv[:, lanes[q]] = y * _sigmoid(jnp.dot(y.astype(BF16), wg_ref[0], preferred_element_type=F32)
                                       + bg_ref[0])

    def scatter_body(tau, c):
        for sg in range(sc):
            o_ref[0, 0, pl.ds(tau * sc + sg, SUBLANES, stride=stride), :] = yv[_rows(tau), lanes[sg]]
        return c

    lax.fori_loop(0, ls, scatter_body, 0)


def s5(proj, first_slab, bend, kc, l8r, l8i, plr, pli, d, wg, bg, *, ls):
    b, _, tp, _ = proj.shape
    nblk, kin, nst = bend.shape
    n_cs = nst // (2 * LANES)
    tt = SUBLANES * ls * S5_SC
    nc = SUBLANES * ls
    per_blk = lambda shape: pl.BlockSpec((1,) + shape, lambda bi, gb, t: (gb,) + (0,) * len(shape))
    return pl.pallas_call(
        functools.partial(_s5_kernel, ls=ls),
        grid=(b, nblk, tp // tt),
        in_specs=[pl.BlockSpec((1, 1, tt, LANES), lambda bi, gb, t: (bi, first_slab + gb, t, 0)),
                  per_blk((kin, nst)), per_blk((kin + nst, kin)),
                  per_blk((n_cs, 1, LANES)), per_blk((n_cs, 1, LANES)),
                  per_blk((n_cs, 1, LANES)), per_blk((n_cs, 1, LANES)),
                  per_blk((1, LANES)), per_blk((LANES, LANES)), per_blk((1, LANES))],
        out_specs=pl.BlockSpec((1, 1, tt, LANES), lambda bi, gb, t: (bi, gb, t, 0)),
        out_shape=jax.ShapeDtypeStruct((b, nblk, tp, LANES), F32),
        scratch_shapes=[pltpu.VMEM((nc, kin), F32), pltpu.VMEM((nc, nst), F32),
                        pltpu.VMEM((nc, kin), F32), pltpu.VMEM((SUBLANES, nst), F32)],
        compiler_params=_cparams(3),
        name="s5",
    )(proj, bend, kc, l8r, l8i, plr, pli, d, wg, bg)


def _s5_params(a_re, a_im, b_re, b_im, c_re, c_im, d, log_dt, w_glu, b_glu, *, ls):
    g, p = a_re.shape
    sc = S5_SC
    gpb = LANES // S5_GROUP
    nblk = g // gpb
    dt = jnp.exp(log_dt)[:, None]
    mag = jnp.exp(a_re * dt)
    ar, ai = mag * jnp.cos(a_im * dt), mag * jnp.sin(a_im * dt)
    den = a_re * a_re + a_im * a_im
    fr = ((ar - 1.0) * a_re + ai * a_im) / den
    fi = (ai * a_re - (ar - 1.0) * a_im) / den
    bbar_re = fr[..., None] * b_re - fi[..., None] * b_im
    bbar_im = fr[..., None] * b_im + fi[..., None] * b_re

    def lam_pow(k):
        k = jnp.asarray(k, F32).reshape(-1, 1, 1)
        m = jnp.exp(a_re * dt * k)
        return m * jnp.cos(a_im * dt * k), m * jnp.sin(a_im * dt * k)

    pw_r, pw_i = lam_pow(jnp.arange(sc + 1))
    eye = jnp.eye(gpb, dtype=F32)
    blk = lambda m: m.reshape(m.shape[:-3] + (nblk, gpb) + m.shape[-2:])

    wr = jnp.stack([pw_r[sc - 1 - s] for s in range(sc)])[..., None]
    wi = jnp.stack([pw_i[sc - 1 - s] for s in range(sc)])[..., None]
    e_re = wr * bbar_re - wi * bbar_im
    e_im = wr * bbar_im + wi * bbar_re
    def end_bd(m):
        t = jnp.transpose(blk(m), (1, 0, 2, 4, 3))
        return (t[:, :, :, :, None, :] * eye.reshape(1, 1, gpb, 1, gpb, 1)).reshape(
            nblk, sc * gpb * S5_GROUP, gpb * p)

    bend = jnp.concatenate([end_bd(e_re), end_bd(e_im)], axis=-1).astype(BF16)

    cl_re = c_re[None] * pw_r[:, :, None, :] - c_im[None] * pw_i[:, :, None, :]
    cl_im = c_re[None] * pw_i[:, :, None, :] + c_im[None] * pw_r[:, :, None, :]
    kl = (jnp.einsum('kgip,gpj->kgij', cl_re[:sc], bbar_re)
          - jnp.einsum('kgip,gpj->kgij', cl_im[:sc], bbar_im))
    kst = jnp.stack([jnp.stack([kl[t - s] if t >= s else jnp.zeros_like(kl[0]) for t in range(sc)])
                     for s in range(sc)])
    kt = jnp.transpose(blk(kst), (2, 0, 3, 5, 1, 4))
    kintra = (kt[:, :, :, :, :, None, :] * eye.reshape(1, 1, gpb, 1, 1, gpb, 1)).reshape(
        nblk, sc * gpb * S5_GROUP, sc * gpb * S5_GROUP)

    def out_bd(m):
        t = jnp.transpose(blk(m), (1, 2, 4, 0, 3))
        return (t[:, :, :, :, None, :] * eye.reshape(1, gpb, 1, 1, gpb, 1)).reshape(
            nblk, gpb * p, sc * gpb * S5_GROUP)

    kc = jnp.concatenate([kintra, out_bd(cl_re[1:]), -out_bd(cl_im[1:])], axis=1).astype(BF16)

    vec = lambda v: v.reshape(nblk, (gpb * p) // LANES, 1, LANES)
    pl_r, pl_i = lam_pow(jnp.asarray([sc * ls]))
    wg = jnp.einsum('bqij,qr->bqirj', w_glu.reshape(nblk, gpb, S5_GROUP, S5_GROUP), eye)
    wg = wg.reshape(nblk, LANES, LANES).astype(BF16)
    return (bend, kc, vec(pw_r[sc]), vec(pw_i[sc]), vec(pl_r[0]), vec(pl_i[0]),
            d.reshape(nblk, 1, LANES), wg, b_glu.reshape(nblk, 1, LANES))


def _ssd_prep_kernel(dt_ref, dtb_ref, alog_ref, cs_ref, rowp_ref, wend_ref, cdec_ref):
    L = SSD_CHUNK
    row = lax.broadcasted_iota(jnp.int32, (L, L), 0)
    col = lax.broadcasted_iota(jnp.int32, (L, L), 1)
    tri = (row >= col).astype(BF16)
    a = -jnp.exp(alog_ref[...])
    for c in range(cs_ref.shape[1]):
        dt = _softplus(dt_ref[0, 0, L * c:L * (c + 1), :] + dtb_ref[...])
        da = dt * a
        da_hi = da.astype(BF16)
        r1 = da - da_hi.astype(F32)
        da_mid = r1.astype(BF16)
        da_lo = (r1 - da_mid.astype(F32)).astype(BF16)
        cs = (jnp.dot(tri, da_hi, preferred_element_type=F32)
              + jnp.dot(tri, da_mid, preferred_element_type=F32)
              + jnp.dot(tri, da_lo, preferred_element_type=F32))
        dt_t = dt.T
        cs_t = cs.T
        last_t = jnp.broadcast_to(cs_t[:, L - 1:L], (L, L))
        cs_ref[0, c] = cs * LOG2_E
        rowp_ref[0, c] = (cs_t - jnp.log(dt_t)) * LOG2_E
        wend_ref[0, c] = dt_t * jnp.exp(last_t - cs_t)
        cdec_ref[0, c] = jnp.exp(last_t)


def ssd_prep(dt, dtb, alog, *, chunks_per_step):
    b, _, tp, _ = dt.shape
    L = SSD_CHUNK
    cp = chunks_per_step
    nc = tp // L
    const = lambda shape: pl.BlockSpec(shape, lambda bi, c: (0,) * len(shape))
    tab = pl.BlockSpec((1, cp, L, LANES), lambda bi, c: (bi, c, 0, 0))
    return pl.pallas_call(
        _ssd_prep_kernel,
        grid=(b, nc // cp),
        in_specs=[pl.BlockSpec((1, 1, cp * L, LANES), lambda bi, c: (bi, 0, c, 0)),
                  const((1, LANES)), const((1, LANES))],
        out_specs=[tab] * 4,
        out_shape=[jax.ShapeDtypeStruct((b, nc, L, LANES), F32)] * 4,
        compiler_params=_cparams(2),
        name="ssd_prep",
    )(dt, dtb, alog)


def _ssd_kernel(xbc_ref, z_ref, cs_ref, rowp_ref, wend_ref, cdec_ref, dskip_ref, nw_ref, o_ref, hst):
    @pl.when(pl.program_id(1) == 0)
    def _():
        hst[...] = jnp.zeros(hst.shape, F32)

    for c in range(xbc_ref.shape[2] // SSD_CHUNK):
        _ssd_chunk(xbc_ref, z_ref, cs_ref, rowp_ref, wend_ref, cdec_ref, dskip_ref, nw_ref, o_ref, hst, c)


def _ssd_chunk(xbc_ref, z_ref, cs_ref, rowp_ref, wend_ref, cdec_ref, dskip_ref, nw_ref, o_ref, hst, c):
    L = SSD_CHUNK
    rs = slice(L * c, L * (c + 1))
    n_groups = hst.shape[0]
    gw = SSD_HPG * SSD_HEAD_DIM
    spg = gw // LANES
    n_xs = n_groups * spg
    row = lax.broadcasted_iota(jnp.int32, (L, L), 0)
    col = lax.broadcasted_iota(jnp.int32, (L, L), 1)
    causal = row >= col
    left_half = col < SSD_HEAD_DIM
    head_of_lane = lax.broadcasted_iota(jnp.int32, (L, gw), 1) // SSD_HEAD_DIM

    def head_rows(ref, g):
        return jnp.concatenate(
            [jnp.broadcast_to(ref[0, c, SSD_HPG * g + r:SSD_HPG * g + r + 1, :], (SSD_HEAD_DIM, L))
             for r in range(SSD_HPG)], axis=0)

    def x_slabs(g):
        return jnp.concatenate([xbc_ref[0, spg * g + k, rs, :] for k in range(spg)], axis=-1)

    scores, y_offs = [], []
    for g in range(n_groups):
        bm = xbc_ref[0, n_xs + g, rs, :]
        cm = xbc_ref[0, n_xs + n_groups + g, rs, :]
        scores.append(lax.dot_general(cm, bm, (((1,), (1,)), ((), ())), preferred_element_type=F32))
        h_prev = hst[g]
        y_offs.append(lax.dot_general(cm, h_prev.astype(BF16), (((1,), (1,)), ((), ())),
                                      preferred_element_type=F32))
        xw_t = (x_slabs(g).astype(F32).T * head_rows(wend_ref, g)).astype(BF16)
        hst[g] = head_rows(cdec_ref, g) * h_prev + jnp.dot(xw_t, bm, preferred_element_type=F32)

    y_mix = []
    for g in range(n_groups):
        xsb = x_slabs(g)
        ms, bcs = [], []
        for r in range(SSD_HPG):
            h = SSD_HPG * g + r
            bcs.append(jnp.broadcast_to(cs_ref[0, c, :, h:h + 1], (L, L)))
            ms.append((scores[g] * jnp.exp2(jnp.where(causal, bcs[r] - rowp_ref[0, c, h:h + 1, :], -jnp.inf))
                       ).astype(BF16))
        x_bd = jnp.concatenate([jnp.where(head_of_lane == r, xsb, jnp.zeros_like(xsb))
                                for r in range(SSD_HPG)], axis=0)
        y_diag = jnp.dot(jnp.concatenate(ms, axis=1), x_bd, preferred_element_type=F32)
        f_start = jnp.concatenate([jnp.exp2(jnp.where(left_half, bcs[2 * k], bcs[2 * k + 1]))
                                   for k in range(spg)], axis=-1)
        y_mix.append(y_diag + y_offs[g] * f_start + dskip_ref[:, gw * g:gw * (g + 1)] * xsb.astype(F32))

    for g in range(n_groups):
        y = y_mix[g]
        zg =jnp.concatenate([z_ref[0, spg * g + k, rs, :] for k in range(spg)], axis=-1).astype(F32)
        gg = y * _silu(zg)
        ms = jnp.mean(gg * gg, axis=-1, keepdims=True)
        o_ref[0, rs, gw * g:gw * (g + 1)] = (gg * lax.rsqrt(ms + NORM_EPS)
                                            * nw_ref[:, gw * g:gw * (g + 1)]).astype(o_ref.dtype)


def ssd_core(xbc, z, tables, dskip, nw):
    b, n_conv, tp, _ = xbc.shape
    n_z = z.shape[1]
    d_inner = n_z * LANES
    n_groups = d_inner // (SSD_HPG * SSD_HEAD_DIM)
    cps = SSD_CHUNKS_PER_STEP
    L = SSD_CHUNK * cps
    const = lambda shape: pl.BlockSpec(shape, lambda bi, c: (0,) * len(shape))
    tab = pl.BlockSpec((1, cps, SSD_CHUNK, LANES), lambda bi, c: (bi, c, 0, 0))
    return pl.pallas_call(
        _ssd_kernel,
        grid=(b, tp // L),
        in_specs=[pl.BlockSpec((1, n_conv, L, LANES), lambda bi, c: (bi, 0, c, 0)),
                  pl.BlockSpec((1, n_z, L, LANES), lambda bi, c: (bi, 0, c, 0)),
                  tab, tab, tab, tab, const((1, d_inner)), const((1, d_inner))],
        out_specs=pl.BlockSpec((1, L, d_inner), lambda bi, c: (bi, c, 0)),
        out_shape=jax.ShapeDtypeStruct((b, tp, d_inner), BF16),
        scratch_shapes=[pltpu.VMEM((n_groups, SSD_HPG * SSD_HEAD_DIM, SSD_STATE), F32)],
        compiler_params=_cparams(2),
        name="ssd_core",
    )(xbc, z, *tables, dskip, nw)


def _block_diag(w):
    n, di, do = w.shape
    return jnp.einsum('gij,gh->gihj', w, jnp.eye(n, dtype=w.dtype)).reshape(n * di, n * do)


def _pad_lanes(v, fill=0.0):
    return jnp.pad(v, (0, LANES - v.shape[0]), constant_values=fill).reshape(1, LANES)


def kernel(x, meta_tokens, norm_mix, norm_mlp, norm_final, ev_w_in, lru_conv_w, lru_conv_b, lru_w_a, lru_b_a, lru_w_x, lru_b_x, lru_lambda, s5_a_re, s5_a_im, s5_b_re, s5_b_im, s5_c_re, s5_c_im, s5_d, s5_log_dt, s5_w_glu, s5_b_glu, ev_w_out, ssd_w_in, ssd_conv_w, ssd_conv_b, ssd_dt_bias, ssd_a_log, ssd_d, ssd_norm, ssd_w_out, mlp_w_up, mlp_w_down):
    bsz, seq, d = x.shape
    t = seq + N_META
    unit = math.lcm(SUBLANES * LRU_LS, SUBLANES * S5_LS * S5_SC, ROW_TILE, SSD_CHUNK)
    tp = -(-t // unit) * unit
    meta = jnp.broadcast_to(meta_tokens[None].astype(x.dtype), (bsz, N_META, d))
    h = jnp.concatenate([meta, x, jnp.zeros((bsz, tp - t, d), x.dtype)], axis=1)

    lru_w = lru_conv_w.shape[-1]
    proj = norm_matmul(h, norm_mix[0], ev_w_in[0].astype(BF16), tm=ROW_TILE, tn=512)
    y_lru = rglru(proj, lru_conv_w[0], lru_conv_b[0], _block_diag(lru_w_a[0]).astype(BF16), lru_b_a[0],
                  _block_diag(lru_w_x[0]).astype(BF16), lru_b_x[0], lru_lambda[0], ls=LRU_LS)
    y_s5 = s5(proj, 2 * lru_w // LANES,
              *_s5_params(s5_a_re[0], s5_a_im[0], s5_b_re[0], s5_b_im[0], s5_c_re[0], s5_c_im[0],
                          s5_d[0], s5_log_dt[0], s5_w_glu[0], s5_b_glu[0], ls=S5_LS), ls=S5_LS)
    h = outproj_slab(h, y_lru, y_s5, ev_w_out[0].astype(BF16), tm=ROW_TILE)
    h = mlp_residual(h.reshape(bsz * tp, d), norm_mlp[0], mlp_w_up[0].astype(BF16),
                     mlp_w_down[0].astype(BF16), norm_final, tm=ROW_TILE, tf=512,
                     final_norm=False).reshape(bsz, tp, d)

    d_inner = ssd_w_out.shape[1]
    conv_dim = ssd_conv_w.shape[-1]
    n_heads = ssd_dt_bias.shape[-1]
    w_in = ssd_w_in[0]
    w_zdt = jnp.concatenate([w_in[:, :d_inner], w_in[:, d_inner + conv_dim:],
                             jnp.zeros((d, LANES - n_heads), F32)], axis=1).astype(BF16)
    xbc = inproj_conv(h, norm_mix[1], w_in[:, d_inner:d_inner + conv_dim].astype(BF16),
                      ssd_conv_w[0], ssd_conv_b[0], tm=ROW_TILE, tn=1024)
    z, dt = inproj_zdt(h, norm_mix[1], w_zdt, tm=ROW_TILE)
    tables = ssd_prep(dt, _pad_lanes(ssd_dt_bias[0]), _pad_lanes(ssd_a_log[0]), chunks_per_step=SSD_PREP_CHUNKS)
    g = ssd_core(xbc, z, tables, jnp.repeat(ssd_d[0], SSD_HEAD_DIM).reshape(1, d_inner),
                 ssd_norm[0].reshape(1, d_inner))
    h = outproj(h, g, ssd_w_out[0].astype(BF16), tm=ROW_TILE)
    return mlp_residual_final(h, norm_mlp[1], mlp_w_up[1].astype(BF16), mlp_w_down[1].astype(BF16),
                              norm_final, seq=seq, tm=FINAL_ROW_TILE, tf=512)
```

```python
import functools
import math

import jax
import jax.numpy as jnp
from jax import lax
from jax.experimental import pallas as pl
from jax.experimental.pallas import tpu as pltpu

F32 = jnp.float32
BF16 = jnp.bfloat16

LANES = 128
SUBLANES = 8
NORM_EPS = 1e-5
LOG2_E = 1.4426950408889634
N_META = 16
CONV_WIDTH = 4
LRU_C = 8.0
S5_GROUP = 16
S5_STATE = 64
SSD_HEAD_DIM = 64
SSD_STATE = 128
SSD_CHUNK = 128
SSD_HPG = 4
SSD_CHUNKS_PER_STEP = 1
SSD_PREP_CHUNKS = 5

LRU_LS = 130
S5_SC = SUBLANES
S5_LS = 65
ROW_TILE = 1280
FINAL_ROW_TILE = 1024
VMEM_LIMIT = 48 * 1024 * 1024


def _cparams(n_axes, flags=None):
    return pltpu.CompilerParams(dimension_semantics=("arbitrary",) * n_axes,
                                vmem_limit_bytes=VMEM_LIMIT, flags=flags)


def _sigmoid(x):
    return 1.0 / (1.0 + jnp.exp(-x))


def _silu(x):
    hx = 0.5 * x
    return hx + hx * jnp.tanh(hx)


def _gelu_tanh(x):
    return 0.5 * x * (1.0 + jnp.tanh(math.sqrt(2.0 / math.pi) * (x + 0.044715 * (x * x * x))))


def _log1p(e):
    u = 1.0 + e
    return jnp.where(u == 1.0, e, jnp.log(u) * (e / (u - 1.0)))


def _softplus(x):
    return jnp.maximum(x, 0.0) + _log1p(jnp.exp(-jnp.abs(x)))


def _rmsnorm_rows(x, w):
    ms = jnp.mean(x * x, axis=-1, keepdims=True)
    return x * lax.rsqrt(ms + NORM_EPS) * w


def _rows(tau):
    return pl.ds(pl.multiple_of(tau * SUBLANES, SUBLANES), SUBLANES)


def _norm_matmul_kernel(x_ref, nw_ref, w_ref, o_ref):
    x = x_ref[0]
    rs = lax.rsqrt(jnp.mean(x * x, axis=-1, keepdims=True) + NORM_EPS)
    xb = (x * nw_ref[...]).astype(BF16)
    step = 4
    for c in range(o_ref.shape[1] // step):
        r = jnp.dot(xb, w_ref[:, LANES * step * c:LANES * step * (c + 1)], preferred_element_type=F32) * rs
        for k in range(step):
            o_ref[0, step * c + k] = r[:, LANES * k:LANES * (k + 1)]


def norm_matmul(h, nw, w, *, tm):
    b, tp, d = h.shape
    n = w.shape[1]
    return pl.pallas_call(
        _norm_matmul_kernel,
        grid=(b, tp // tm),
        in_specs=[pl.BlockSpec((1, tm, d), lambda bi, i: (bi, i, 0)),
                  pl.BlockSpec((1, d), lambda bi, i: (0, 0)),
                  pl.BlockSpec((d, n), lambda bi, i: (0, 0))],
        out_specs=pl.BlockSpec((1, n // LANES, tm, LANES), lambda bi, i: (bi, 0, i, 0)),
        out_shape=jax.ShapeDtypeStruct((b, n // LANES, tp, LANES), F32),
        compiler_params=_cparams(2),
        name="norm_matmul",
    )(h, nw.reshape(1, d), w)


def _inproj_conv_kernel(x_ref, nw_ref, w_ref, cw_ref, cb_ref, o_ref, xn_ref, rs, hist):
    i, j = pl.program_id(1), pl.program_id(2)
    tm = x_ref.shape[1]
    n_slab = o_ref.shape[1]
    hrows = SUBLANES
    pack = 2 * SUBLANES

    @pl.when(j == 0)
    def _():
        xn_ref[...] = _rmsnorm_rows(x_ref[0], nw_ref[...]).astype(BF16)

    @pl.when(i == 0)
    def _():
        hist[j] = jnp.zeros(hist.shape[1:], F32)

    xn = xn_ref[...]
    proj = lambda c: jnp.dot(xn, w_ref[:, 2 * LANES * c:2 * LANES * (c + 1)], preferred_element_type=F32)
    r_next = proj(0)
    for c in range(n_slab // 2):
        r = r_next
        if c + 1 < n_slab // 2:
            r_next = proj(c + 1)
        for k in range(2):
            s = 2 * c + k
            sl = slice(LANES * s, LANES * (s + 1))
            rs[s, 0:hrows, :] = hist[j, s]
            rs[s, hrows:hrows + tm, :] = r[:, LANES * k:LANES * (k + 1)]
            hist[j, s] = rs[s, tm:tm + hrows, :]
            taps = [cw_ref[m, :, sl] for m in range(CONV_WIDTH)]
            bias = cb_ref[:, sl]
            for g in range(tm // pack):
                halves = []
                for hh in range(pack // SUBLANES):
                    r0 = hrows + pack * g + SUBLANES * hh
                    acc = bias
                    for m in range(CONV_WIDTH):
                        acc = acc + taps[m] * rs[s, r0 - m:r0 - m + SUBLANES, :]
                    halves.append(_silu(acc))
                o_ref[0, s, pack * g:pack * (g + 1), :] = jnp.concatenate(halves, axis=0).astype(BF16)


def inproj_conv(h, nw, w, cw, cb, *, tm, tn):
    b, tp, d = h.shape
    n = w.shape[1]
    ns = tn // LANES
    cw8 = jnp.broadcast_to(jnp.stack([cw[CONV_WIDTH - 1 - m] for m in range(CONV_WIDTH)])[:, None, :],
                           (CONV_WIDTH, SUBLANES, n))
    cb8 = jnp.broadcast_to(cb.reshape(1, n), (SUBLANES, n))
    return pl.pallas_call(
        _inproj_conv_kernel,
        grid=(b, tp // tm, n // tn),
        in_specs=[pl.BlockSpec((1, tm, d), lambda bi, i, j: (bi, i, 0)),
                  pl.BlockSpec((1, d), lambda bi, i, j: (0, 0)),
                  pl.BlockSpec((d, tn), lambda bi, i, j: (0, j)),
                  pl.BlockSpec((CONV_WIDTH, SUBLANES, tn), lambda bi, i, j: (0, 0, j)),
                  pl.BlockSpec((SUBLANES, tn), lambda bi, i, j: (0, j))],
        out_specs=pl.BlockSpec((1, ns, tm, LANES), lambda bi, i, j: (bi, j, i, 0)),
        out_shape=jax.ShapeDtypeStruct((b, n // LANES, tp, LANES), BF16),
        scratch_shapes=[pltpu.VMEM((tm, d), BF16),
                        pltpu.VMEM((ns, SUBLANES + tm, LANES), F32),
                        pltpu.VMEM((n // tn, ns, SUBLANES, LANES), F32)],
        compiler_params=_cparams(3),
        name="inproj_conv",
    )(h, nw.reshape(1, d), w, cw8, cb8)


def _inproj_zdt_kernel(x_ref, nw_ref, w_ref, z_ref, dt_ref):
    x = x_ref[0]
    rs = lax.rsqrt(jnp.mean(x * x, axis=-1, keepdims=True) + NORM_EPS)
    xb = (x * nw_ref[...]).astype(BF16)
    n_z = z_ref.shape[1]
    step = 4
    for c in range(n_z // step):
        r = jnp.dot(xb, w_ref[:, LANES * step * c:LANES * step * (c + 1)], preferred_element_type=F32) * rs
        for k in range(step):
            z_ref[0, step * c + k] = r[:, LANES * k:LANES * (k + 1)].astype(BF16)
    dt_ref[0, 0] = jnp.dot(xb, w_ref[:, LANES * n_z:LANES * (n_z + 1)], preferred_element_type=F32) * rs


def inproj_zdt(h, nw, w, *, tm):
    b, tp, d = h.shape
    n_z = w.shape[1] // LANES - 1
    return pl.pallas_call(
        _inproj_zdt_kernel,
        grid=(b, tp // tm),
        in_specs=[pl.BlockSpec((1, tm, d), lambda bi, i: (bi, i, 0)),
                  pl.BlockSpec((1, d), lambda bi, i: (0, 0)),
                  pl.BlockSpec(w.shape, lambda bi, i: (0, 0))],
        out_specs=[pl.BlockSpec((1, n_z, tm, LANES), lambda bi, i: (bi, 0, i, 0)),
                   pl.BlockSpec((1, 1, tm, LANES), lambda bi, i: (bi, 0, i, 0))],
        out_shape=[jax.ShapeDtypeStruct((b, n_z, tp, LANES), BF16),
                   jax.ShapeDtypeStruct((b, 1, tp, LANES), F32)],
        compiler_params=_cparams(2),
        name="inproj_zdt",
    )(h, nw.reshape(1, d), w)


def _mlp_kernel(x_ref, nw_ref, wu_ref, wd_ref, fw_ref, o_ref, xn_ref, *, final_norm):
    j = pl.program_id(1)

    @pl.when(j == 0)
    def _():
        x = x_ref[...]
        xn_ref[...] = _rmsnorm_rows(x, nw_ref[...]).astype(BF16)
        o_ref[...] = x

    u = jnp.dot(xn_ref[...], wu_ref[...].astype(BF16), preferred_element_type=F32)
    a = jnp.square(jnp.maximum(u, 0.0)).astype(BF16)
    o_ref[...] += jnp.dot(a, wd_ref[...].astype(BF16), preferred_element_type=F32)

    if final_norm:
        @pl.when(j == pl.num_programs(1) - 1)
        def _():
            o_ref[...] = _rmsnorm_rows(o_ref[...], fw_ref[...])


def mlp_residual(h2, nw, w_up, w_down, fw, *, tm, tf, final_norm):
    n, d = h2.shape
    dff = w_up.shape[1]
    return pl.pallas_call(
        functools.partial(_mlp_kernel, final_norm=final_norm),
        grid=(n // tm, dff // tf),
        in_specs=[pl.BlockSpec((tm, d), lambda i, j: (i, 0)),
                  pl.BlockSpec((1, d), lambda i, j: (0, 0)),
                  pl.BlockSpec((d, tf), lambda i, j: (0, j)),
                  pl.BlockSpec((tf, d), lambda i, j: (j, 0)),
                  pl.BlockSpec((1, d), lambda i, j: (0, 0))],
        out_specs=pl.BlockSpec((tm, d), lambda i, j: (i, 0)),
        out_shape=jax.ShapeDtypeStruct((n, d), F32),
        scratch_shapes=[pltpu.VMEM((tm, d), BF16)],
        compiler_params=_cparams(2),
        name="mlp_final" if final_norm else "mlp",
    )(h2, nw.reshape(1, d), w_up, w_down, fw.reshape(1, d))


def _mlp_final_kernel(x_ref, nw_ref, wu_ref, wd_ref, fw_ref, o_ref, xn_ref):
    j = pl.program_id(2)

    @pl.when(j == 0)
    def _():
        x = x_ref[0]
        xn_ref[...] = _rmsnorm_rows(x, nw_ref[...]).astype(BF16)
        o_ref[0] = x

    u = jnp.dot(xn_ref[...], wu_ref[...].astype(BF16), preferred_element_type=F32)
    a = jnp.square(jnp.maximum(u, 0.0)).astype(BF16)
    o_ref[0] += jnp.dot(a, wd_ref[...].astype(BF16), preferred_element_type=F32)

    @pl.when(j == pl.num_programs(2) - 1)
    def _():
        o_ref[0] = _rmsnorm_rows(o_ref[0], fw_ref[...])


def mlp_residual_final(h, nw, w_up, w_down, fw, *, seq, tm, tf):
    b, _, d = h.shape
    dff = w_up.shape[1]
    return pl.pallas_call(
        _mlp_final_kernel,
        grid=(b, seq // tm, dff // tf),
        in_specs=[pl.BlockSpec((pl.Element(1), pl.Element(tm), pl.Element(d)),
                               lambda bi, i, j: (bi, pl.multiple_of(N_META + i * tm, SUBLANES), 0)),
                  pl.BlockSpec((1, d), lambda bi, i, j: (0, 0)),
                  pl.BlockSpec((d, tf), lambda bi, i, j: (0, j)),
                  pl.BlockSpec((tf, d), lambda bi, i, j: (j, 0)),
                  pl.BlockSpec((1, d), lambda bi, i, j: (0, 0))],
        out_specs=pl.BlockSpec((1, tm, d), lambda bi, i, j: (bi, i, 0)),
        out_shape=jax.ShapeDtypeStruct((b, seq, d), F32),
        scratch_shapes=[pltpu.VMEM((tm, d), BF16)],
        compiler_params=_cparams(3),
        name="mlp_final",
    )(h, nw.reshape(1, d), w_up, w_down, fw.reshape(1, d))


def _outproj_slab_kernel(h_ref, ya_ref, yb_ref, w_ref, o_ref):
    parts = [ya_ref[0, s] for s in range(ya_ref.shape[1])] + [yb_ref[0, s] for s in range(yb_ref.shape[1])]
    y = jnp.concatenate(parts, axis=-1).astype(BF16)
    o_ref[0] = h_ref[0] + jnp.dot(y, w_ref[...], preferred_element_type=F32)


def outproj_slab(h, ya, yb, w, *, tm):
    b, tp, d = h.shape
    sa, sb = ya.shape[1], yb.shape[1]
    return pl.pallas_call(
        _outproj_slab_kernel,
        grid=(b, tp // tm),
        in_specs=[pl.BlockSpec((1, tm, d), lambda bi, i: (bi, i, 0)),
                  pl.BlockSpec((1, sa, tm, LANES), lambda bi, i: (bi, 0, i, 0)),
                  pl.BlockSpec((1, sb, tm, LANES), lambda bi, i: (bi, 0, i, 0)),
                  pl.BlockSpec(w.shape, lambda bi, i: (0, 0))],
        out_specs=pl.BlockSpec((1, tm, d), lambda bi, i: (bi, i, 0)),
        out_shape=jax.ShapeDtypeStruct(h.shape, F32),
        compiler_params=_cparams(2),
        name="outproj_slab",
    )(h, ya, yb, w)


def _outproj_kernel(h_ref, y_ref, w_ref, o_ref):
    o_ref[0] = h_ref[0] + jnp.dot(y_ref[0], w_ref[...], preferred_element_type=F32)


def outproj(h, y, w, *, tm):
    b, tp, d = h.shape
    k = y.shape[-1]
    return pl.pallas_call(
        _outproj_kernel,
        grid=(b, tp // tm),
        in_specs=[pl.BlockSpec((1, tm, d), lambda bi, i: (bi, i, 0)),
                  pl.BlockSpec((1, tm, k), lambda bi, i: (bi, i, 0)),
                  pl.BlockSpec(w.shape, lambda bi, i: (0, 0))],
        out_specs=pl.BlockSpec((1, tm, d), lambda bi, i: (bi, i, 0)),
        out_shape=jax.ShapeDtypeStruct(h.shape, F32),
        compiler_params=_cparams(2),
        name="outproj",
    )(h, y, w)


def _rglru_kernel(x_ref, g_ref, cw_ref, cb_ref, wa_ref, ba_ref, wx_ref, bx_ref, lam_ref, o_ref,
                  xbuf, xcp, a_s, b_s, hn, carry, *, ls):
    n_slab = x_ref.shape[1]
    tt = SUBLANES * ls
    hist = SUBLANES
    slabs = [slice(LANES * s, LANES * (s + 1)) for s in range(n_slab)]

    @pl.when(pl.program_id(1) == 0)
    def _():
        xbuf[:, 0:hist, :] = jnp.zeros((n_slab, hist, LANES), F32)
        carry[...] = jnp.zeros(carry.shape, F32)

    for s in range(n_slab):
        xbuf[s, hist:hist + tt, :] = x_ref[0, s]

    def conv_body(tau, c):
        for s in range(n_slab):
            acc = cb_ref[:, slabs[s]]
            for m in range(CONV_WIDTH):
                acc = acc + cw_ref[m, :, slabs[s]] * xbuf[s, pl.ds(hist + tau - m, SUBLANES, stride=ls), :]
            xcp[_rows(tau), slabs[s]] = acc
        return c

    lax.fori_loop(0, ls, conv_body, 0, unroll=2)
    for s in range(n_slab):
        xbuf[s, 0:hist, :] = xbuf[s, tt:tt + hist, :]

    xc = xcp[...]
    xb = xc.astype(BF16)
    r = _sigmoid(jnp.dot(xb, wa_ref[...], preferred_element_type=F32) + ba_ref[...])
    i = _sigmoid(jnp.dot(xb, wx_ref[...], preferred_element_type=F32) + bx_ref[...])
    log_a = (-LRU_C) * r * _softplus(-lam_ref[...])
    a = jnp.exp(log_a)
    a_s[...] = a
    b_s[...] = jnp.sqrt(-jnp.tanh(log_a) * (a * a + 1.0)) * (i * xc)

    def pass1(tau, c):
        out = []
        for s in range(n_slab):
            p, e = c[2 * s], c[2 * s + 1]
            av = a_s[_rows(tau), slabs[s]]
            out += [p * av, av * e + b_s[_rows(tau), slabs[s]]]
        return tuple(out)

    one = jnp.ones((SUBLANES, LANES), F32)
    zero = jnp.zeros((SUBLANES, LANES), F32)
    pe = lax.fori_loop(0, ls, pass1, (one, zero) * n_slab, unroll=2)

    h0 = []
    for s in range(n_slab):
        p, e = pe[2 * s], pe[2 * s + 1]
        c = carry[0:1, slabs[s]]
        rows = []
        for j in range(SUBLANES):
            rows.append(c)
            c = p[j:j + 1] * c + e[j:j + 1]
        carry[0:1, slabs[s]] = c
        h0.append(jnp.concatenate(rows, axis=0))

    def pass2(tau, hs):
        out = []
        for s in range(n_slab):
            h = a_s[_rows(tau), slabs[s]] * hs[s] + b_s[_rows(tau), slabs[s]]
            hn[s, pl.ds(tau, SUBLANES, stride=ls), :] = h
            out.append(h)
        return tuple(out)

    lax.fori_loop(0, ls, pass2, tuple(h0), unroll=2)
    for s in range(n_slab):
        o_ref[0, s] = hn[s] * _gelu_tanh(g_ref[0, s])


def rglru(proj, cw, cb, wa_bd, ba, wx_bd, bx, lam, *, ls):
    b, _, tp, _ = proj.shape
    w = cw.shape[1]
    ns = w // LANES
    tt = SUBLANES * ls
    cw8 = jnp.broadcast_to(jnp.stack([cw[CONV_WIDTH - 1 - m] for m in range(CONV_WIDTH)])[:, None, :],
                           (CONV_WIDTH, SUBLANES, w))
    cb8 = jnp.broadcast_to(cb[None, :], (SUBLANES, w))
    const = lambda shape: pl.BlockSpec(shape, lambda bi, t: (0,) * len(shape))
    return pl.pallas_call(
        functools.partial(_rglru_kernel, ls=ls),
        grid=(b, tp // tt),
        in_specs=[pl.BlockSpec((1, ns, tt, LANES), lambda bi, t: (bi, 0, t, 0)),
                  pl.BlockSpec((1, ns, tt, LANES), lambda bi, t: (bi, 1, t, 0)),
                  const((CONV_WIDTH, SUBLANES, w)), const((SUBLANES, w)),
                  const((w, w)), const((1, w)), const((w, w)), const((1, w)), const((1, w))],
        out_specs=pl.BlockSpec((1, ns, tt, LANES), lambda bi, t: (bi, 0, t, 0)),
        out_shape=jax.ShapeDtypeStruct((b, ns, tp, LANES), F32),
        scratch_shapes=[pltpu.VMEM((ns, SUBLANES + tt, LANES), F32),
                        pltpu.VMEM((tt, w), F32), pltpu.VMEM((tt, w), F32), pltpu.VMEM((tt, w), F32),
                        pltpu.VMEM((ns, tt, LANES), F32), pltpu.VMEM((SUBLANES, w), F32)],
        compiler_params=_cparams(2),
        name="rglru",
    )(proj, proj, cw8, cb8, wa_bd, ba.reshape(1, w), wx_bd, bx.reshape(1, w), lam.reshape(1, w))


def _s5_kernel(u_ref, bend_ref, kc_ref, l8r_ref, l8i_ref, plr_ref, pli_ref, d_ref, wg_ref, bg_ref,
               o_ref, lp, st, yv, carry, *, ls):
    sc = S5_SC
    n_cs = l8r_ref.shape[1]
    stride = sc * ls
    cre = [slice(LANES * k, LANES * (k + 1)) for k in range(n_cs)]
    cim = [slice(LANES * (n_cs + k), LANES * (n_cs + k + 1)) for k in range(n_cs)]
    lanes = [slice(LANES * q, LANES * (q + 1)) for q in range(sc)]

    @pl.when(pl.program_id(2) == 0)
    def _():
        carry[...] = jnp.zeros(carry.shape, F32)

    def gather_body(tau, c):
        for sg in range(sc):
            lp[_rows(tau), lanes[sg]] = u_ref[0, 0, pl.ds(tau * sc + sg, SUBLANES, stride=stride), :]
        return c

    lax.fori_loop(0, ls, gather_body, 0)
    st[...] = jnp.dot(lp[...].astype(BF16), bend_ref[0], preferred_element_type=F32)

    lam = [(jnp.broadcast_to(l8r_ref[0, k], (SUBLANES, LANES)),
            jnp.broadcast_to(l8i_ref[0, k], (SUBLANES, LANES))) for k in range(n_cs)]

    def step(tau, k, sr, si):
        lr, li = lam[k]
        return (lr * sr - li * si + st[_rows(tau), cre[k]], lr * si + li * sr + st[_rows(tau), cim[k]])

    def pass1(tau, c):
        out = []
        for k in range(n_cs):
            out += list(step(tau, k, c[2 * k], c[2 * k + 1]))
        return tuple(out)

    zero = jnp.zeros((SUBLANES, LANES), F32)
    ends = lax.fori_loop(0, ls, pass1, (zero,) * (2 * n_cs))

    starts = []
    for k in range(n_cs):
        er, ei = ends[2 * k], ends[2 * k + 1]
        pr, pi = plr_ref[0, k], pli_ref[0, k]
        c_r, c_i = carry[0:1, cre[k]], carry[0:1, cim[k]]
        rows_r, rows_i = [], []
        for j in range(SUBLANES):
            rows_r.append(c_r)
            rows_i.append(c_i)
            c_r, c_i = (pr * c_r - pi * c_i + er[j:j + 1], pr * c_i + pi * c_r + ei[j:j + 1])
        carry[0:1, cre[k]] = c_r
        carry[0:1, cim[k]] = c_i
        starts += [jnp.concatenate(rows_r, axis=0), jnp.concatenate(rows_i, axis=0)]

    def pass2(tau, c):
        out = []
        for k in range(n_cs):
            nr, ni = step(tau, k, c[2 * k], c[2 * k + 1])
            st[_rows(tau), cre[k]] = c[2 * k]
            st[_rows(tau), cim[k]] = c[2 * k + 1]
            out += [nr, ni]
        return tuple(out)

    lax.fori_loop(0, ls, pass2, tuple(starts))

    lhs = jnp.concatenate([lp[...], st[...]], axis=1).astype(BF16)
    yv[...] = jnp.dot(lhs, kc_ref[0], preferred_element_type=F32)
    for q in range(sc):
        y = yv[:, lanes[q]] + d_ref[0] * lp[:, lanes[q]]
        y = _gelu_tanh(y)
        yv[:, lanes[q]] = y * _sigmoid(jnp.dot(y.astype(BF16), wg_ref[0], preferred_element_type=F32)
                                       + bg_ref[0])

    def scatter_body(tau, c):
        for sg in range(sc):
            o_ref[0, 0, pl.ds(tau * sc + sg, SUBLANES, stride=stride), :] = yv[_rows(tau), lanes[sg]]
        return c

    lax.fori_loop(0, ls, scatter_body, 0)


def s5(proj, first_slab, bend, kc, l8r, l8i, plr, pli, d, wg, bg, *, ls):
    b, _, tp, _ = proj.shape
    nblk, kin, nst = bend.shape
    n_cs = nst // (2 * LANES)
    tt = SUBLANES * ls * S5_SC
    nc = SUBLANES * ls
    per_blk = lambda shape: pl.BlockSpec((1,) + shape, lambda bi, gb, t: (gb,) + (0,) * len(shape))
    return pl.pallas_call(
        functools.partial(_s5_kernel, ls=ls),
        grid=(b, nblk, tp // tt),
        in_specs=[pl.BlockSpec((1, 1, tt, LANES), lambda bi, gb, t: (bi, first_slab + gb, t, 0)),
                  per_blk((kin, nst)), per_blk((kin + nst, kin)),
                  per_blk((n_cs, 1, LANES)), per_blk((n_cs, 1, LANES)),
                  per_blk((n_cs, 1, LANES)), per_blk((n_cs, 1, LANES)),
                  per_blk((1, LANES)), per_blk((LANES, LANES)), per_blk((1, LANES))],
        out_specs=pl.BlockSpec((1, 1, tt, LANES), lambda bi, gb, t: (bi, gb, t, 0)),
        out_shape=jax.ShapeDtypeStruct((b, nblk, tp, LANES), F32),
        scratch_shapes=[pltpu.VMEM((nc, kin), F32), pltpu.VMEM((nc, nst), F32),
                        pltpu.VMEM((nc, kin), F32), pltpu.VMEM((SUBLANES, nst), F32)],
        compiler_params=_cparams(3),
        name="s5",
    )(proj, bend, kc, l8r, l8i, plr, pli, d, wg, bg)


def _s5_params(a_re, a_im, b_re, b_im, c_re, c_im, d, log_dt, w_glu, b_glu, *, ls):
    g, p = a_re.shape
    sc = S5_SC
    gpb = LANES // S5_GROUP
    nblk = g // gpb
    dt = jnp.exp(log_dt)[:, None]
    mag = jnp.exp(a_re * dt)
    ar, ai = mag * jnp.cos(a_im * dt), mag * jnp.sin(a_im * dt)
    den = a_re * a_re + a_im * a_im
    fr = ((ar - 1.0) * a_re + ai * a_im) / den
    fi = (ai * a_re - (ar - 1.0) * a_im) / den
    bbar_re = fr[..., None] * b_re - fi[..., None] * b_im
    bbar_im = fr[..., None] * b_im + fi[..., None] * b_re

    def lam_pow(k):
        k = jnp.asarray(k, F32).reshape(-1, 1, 1)
        m = jnp.exp(a_re * dt * k)
        return m * jnp.cos(a_im * dt * k), m * jnp.sin(a_im * dt * k)

    pw_r, pw_i = lam_pow(jnp.arange(sc + 1))
    def group_diag(dense, rows_per_group):
        n = dense.shape[-1]
        tiled = jnp.tile(dense, (1,) * (dense.ndim - 1) + (gpb,))
        rg = lax.broadcasted_iota(jnp.int32, tiled.shape[-2:], 0) // rows_per_group
        cg = lax.broadcasted_iota(jnp.int32, tiled.shape[-2:], 1) // n
        return jnp.where(rg == cg, tiled, 0.0)

    def by_block(m):
        k, _, a, b2 = m.shape
        return jnp.transpose(m.reshape(k, nblk, gpb, a, b2), (0, 1, 2, 4, 3)).reshape(k, nblk, gpb * b2, a)

    wr = jnp.stack([pw_r[sc - 1 - s] for s in range(sc)])[..., None]
    wi = jnp.stack([pw_i[sc - 1 - s] for s in range(sc)])[..., None]
    e_re = wr * bbar_re - wi * bbar_im
    e_im = wr * bbar_im + wi * bbar_re
    rows_cat = lambda m: jnp.concatenate([m[k] for k in range(m.shape[0])], axis=-2)
    bend = jnp.concatenate([rows_cat(group_diag(by_block(e_re), S5_GROUP)),
                            rows_cat(group_diag(by_block(e_im), S5_GROUP))], axis=-1).astype(BF16)

    cl_re = c_re[None] * pw_r[:, :, None, :] - c_im[None] * pw_i[:, :, None, :]
    cl_im = c_re[None] * pw_i[:, :, None, :] + c_im[None] * pw_r[:, :, None, :]
    kl = (jnp.einsum('kgip,gpj->kgij', cl_re[:sc], bbar_re)
          - jnp.einsum('kgip,gpj->kgij', cl_im[:sc], bbar_im))
    kd = group_diag(by_block(kl), S5_GROUP)
    zero_blk = jnp.zeros_like(kd[0])
    kintra = jnp.concatenate(
        [jnp.concatenate([kd[t - s] if t >= s else zero_blk for t in range(sc)], axis=-1)
         for s in range(sc)], axis=-2)

    def out_bd(m):
        dense = jnp.transpose(m, (0, 1, 3, 2))
        d2 = group_diag(dense.reshape(sc, nblk, gpb * p, S5_GROUP), p)
        return jnp.concatenate([d2[t] for t in range(sc)], axis=-1)

    kc = jnp.concatenate([kintra, out_bd(cl_re[1:]), -out_bd(cl_im[1:])], axis=-2).astype(BF16)

    vec = lambda v: v.reshape(nblk, (gpb * p) // LANES, 1, LANES)
    pl_r, pl_i = lam_pow(jnp.asarray([sc * ls]))
    wg = group_diag(w_glu.reshape(nblk, LANES, S5_GROUP), S5_GROUP).astype(BF16)
    return (bend, kc, vec(pw_r[sc]), vec(pw_i[sc]), vec(pl_r[0]), vec(pl_i[0]),
            d.reshape(nblk, 1, LANES), wg, b_glu.reshape(nblk, 1, LANES))


def _ssd_prep_kernel(dt_ref, dtb_ref, alog_ref, cs_ref, rowp_ref, wend_ref, cdec_ref):
    L = SSD_CHUNK
    row = lax.broadcasted_iota(jnp.int32, (L, L), 0)
    col = lax.broadcasted_iota(jnp.int32, (L, L), 1)
    tri = (row >= col).astype(BF16)
    a = -jnp.exp(alog_ref[...])
    for c in range(cs_ref.shape[1]):
        dt = _softplus(dt_ref[0, 0, L * c:L * (c + 1), :] + dtb_ref[...])
        da = dt * a
        da_hi = da.astype(BF16)
        r1 = da - da_hi.astype(F32)
        da_mid = r1.astype(BF16)
        da_lo = (r1 - da_mid.astype(F32)).astype(BF16)
        cs = (jnp.dot(tri, da_hi, preferred_element_type=F32)
              + jnp.dot(tri, da_mid, preferred_element_type=F32)
              + jnp.dot(tri, da_lo, preferred_element_type=F32))
        dt_t = dt.T
        cs_t = cs.T
        last_t = jnp.broadcast_to(cs_t[:, L - 1:L], (L, L))
        cs_ref[0, c] = cs * LOG2_E
        rowp_ref[0, c] = (cs_t - jnp.log(dt_t)) * LOG2_E
        wend_ref[0, c] = dt_t * jnp.exp(last_t - cs_t)
        cdec_ref[0, c] = jnp.exp(last_t)


def ssd_prep(dt, dtb, alog, *, chunks_per_step):
    b, _, tp, _ = dt.shape
    L = SSD_CHUNK
    cp = chunks_per_step
    nc = tp // L
    const = lambda shape: pl.BlockSpec(shape, lambda bi, c: (0,) * len(shape))
    tab = pl.BlockSpec((1, cp, L, LANES), lambda bi, c: (bi, c, 0, 0))
    return pl.pallas_call(
        _ssd_prep_kernel,
        grid=(b, nc // cp),
        in_specs=[pl.BlockSpec((1, 1, cp * L, LANES), lambda bi, c: (bi, 0, c, 0)),
                  const((1, LANES)), const((1, LANES))],
        out_specs=[tab] * 4,
        out_shape=[jax.ShapeDtypeStruct((b, nc, L, LANES), F32)] * 4,
        compiler_params=_cparams(2),
        name="ssd_prep",
    )(dt, dtb, alog)


def _ssd_kernel(xbc_ref, z_ref, cs_ref, rowp_ref, wend_ref, cdec_ref, dskip_ref, nw_ref, o_ref, hst):
    @pl.when(pl.program_id(1) == 0)
    def _():
        hst[...] = jnp.zeros(hst.shape, F32)

    for c in range(xbc_ref.shape[2] // SSD_CHUNK):
        _ssd_chunk(xbc_ref, z_ref, cs_ref, rowp_ref, wend_ref, cdec_ref, dskip_ref, nw_ref, o_ref, hst, c)


def _ssd_chunk(xbc_ref, z_ref, cs_ref, rowp_ref, wend_ref, cdec_ref, dskip_ref, nw_ref, o_ref, hst, c):
    L = SSD_CHUNK
    rs = slice(L * c, L * (c + 1))
    n_groups = hst.shape[0]
    gw = SSD_HPG * SSD_HEAD_DIM
    spg = gw // LANES
    n_xs = n_groups * spg
    row = lax.broadcasted_iota(jnp.int32, (L, L), 0)
    col = lax.broadcasted_iota(jnp.int32, (L, L), 1)
    causal = row >= col
    left_half = col < SSD_HEAD_DIM
    head_of_lane = lax.broadcasted_iota(jnp.int32, (L, gw), 1) // SSD_HEAD_DIM

    def head_rows(ref, g):
        return jnp.concatenate(
            [jnp.broadcast_to(ref[0, c, SSD_HPG * g + r:SSD_HPG * g + r + 1, :], (SSD_HEAD_DIM, L))
             for r in range(SSD_HPG)], axis=0)

    def x_slabs(g):
        return jnp.concatenate([xbc_ref[0, spg * g + k, rs, :] for k in range(spg)], axis=-1)

    scores, y_offs = [], []
    for g in range(n_groups):
        bm = xbc_ref[0, n_xs + g, rs, :]
        cm = xbc_ref[0, n_xs + n_groups + g, rs, :]
        scores.append(lax.dot_general(cm, bm, (((1,), (1,)), ((), ())), preferred_element_type=F32))
        h_prev = hst[g]
        y_offs.append(lax.dot_general(cm, h_prev.astype(BF16), (((1,), (1,)), ((), ())),
                                      preferred_element_type=F32))
        xw_t = (x_slabs(g).astype(F32).T * head_rows(wend_ref, g)).astype(BF16)
        hst[g] = head_rows(cdec_ref, g) * h_prev + jnp.dot(xw_t, bm, preferred_element_type=F32)

    y_mix = []
    for g in range(n_groups):
        xsb = x_slabs(g)
        ms, bcs = [], []
        for r in range(SSD_HPG):
            h = SSD_HPG * g + r
            bcs.append(jnp.broadcast_to(cs_ref[0, c, :, h:h + 1], (L, L)))
            ms.append((scores[g] * jnp.exp2(jnp.where(causal, bcs[r] - rowp_ref[0, c, h:h + 1, :], -jnp.inf))
                       ).astype(BF16))
        x_bd = jnp.concatenate([jnp.where(head_of_lane == r, xsb, jnp.zeros_like(xsb))
                                for r in range(SSD_HPG)], axis=0)
        y_diag = jnp.dot(jnp.concatenate(ms, axis=1), x_bd, preferred_element_type=F32)
        f_start = jnp.concatenate([jnp.exp2(jnp.where(left_half, bcs[2 * k], bcs[2 * k + 1]))
                                   for k in range(spg)], axis=-1)
        y_mix.append(y_diag + y_offs[g] * f_start + dskip_ref[:, gw * g:gw * (g + 1)] * xsb.astype(F32))

    for g in range(n_groups):
        y = y_mix[g]
        zg =jnp.concatenate([z_ref[0, spg * g + k, rs, :] for k in range(spg)], axis=-1).astype(F32)
        gg = y * _silu(zg)
        ms = jnp.mean(gg * gg, axis=-1, keepdims=True)
        o_ref[0, rs, gw * g:gw * (g + 1)] = (gg * lax.rsqrt(ms + NORM_EPS)
                                            * nw_ref[:, gw * g:gw * (g + 1)]).astype(o_ref.dtype)


def ssd_core(xbc, z, tables, dskip, nw):
    b, n_conv, tp, _ = xbc.shape
    n_z = z.shape[1]
    d_inner = n_z * LANES
    n_groups = d_inner // (SSD_HPG * SSD_HEAD_DIM)
    cps = SSD_CHUNKS_PER_STEP
    L = SSD_CHUNK * cps
    const = lambda shape: pl.BlockSpec(shape, lambda bi, c: (0,) * len(shape))
    tab = pl.BlockSpec((1, cps, SSD_CHUNK, LANES), lambda bi, c: (bi, c, 0, 0))
    return pl.pallas_call(
        _ssd_kernel,
        grid=(b, tp // L),
        in_specs=[pl.BlockSpec((1, n_conv, L, LANES), lambda bi, c: (bi, 0, c, 0)),
                  pl.BlockSpec((1, n_z, L, LANES), lambda bi, c: (bi, 0, c, 0)),
                  tab, tab, tab, tab, const((1, d_inner)), const((1, d_inner))],
        out_specs=pl.BlockSpec((1, L, d_inner), lambda bi, c: (bi, c, 0)),
        out_shape=jax.ShapeDtypeStruct((b, tp, d_inner), BF16),
        scratch_shapes=[pltpu.VMEM((n_groups, SSD_HPG * SSD_HEAD_DIM, SSD_STATE), F32)],
        compiler_params=_cparams(2),
        name="ssd_core",
    )(xbc, z, *tables, dskip, nw)


def _block_diag(w):
    n, di, do = w.shape
    return jnp.einsum('gij,gh->gihj', w, jnp.eye(n, dtype=w.dtype)).reshape(n * di, n * do)


def _pad_lanes(v, fill=0.0):
    return jnp.pad(v, (0, LANES - v.shape[0]), constant_values=fill).reshape(1, LANES)


def kernel(x, meta_tokens, norm_mix, norm_mlp, norm_final, ev_w_in, lru_conv_w, lru_conv_b, lru_w_a, lru_b_a, lru_w_x, lru_b_x, lru_lambda, s5_a_re, s5_a_im, s5_b_re, s5_b_im, s5_c_re, s5_c_im, s5_d, s5_log_dt, s5_w_glu, s5_b_glu, ev_w_out, ssd_w_in, ssd_conv_w, ssd_conv_b, ssd_dt_bias, ssd_a_log, ssd_d, ssd_norm, ssd_w_out, mlp_w_up, mlp_w_down):
    bsz, seq, d = x.shape
    t = seq + N_META
    unit = math.lcm(SUBLANES * LRU_LS, SUBLANES * S5_LS * S5_SC, ROW_TILE, SSD_CHUNK)
    tp = -(-t // unit) * unit
    meta = jnp.broadcast_to(meta_tokens[None].astype(x.dtype), (bsz, N_META, d))
    h = jnp.concatenate([meta, x, jnp.zeros((bsz, tp - t, d), x.dtype)], axis=1)

    lru_w = lru_conv_w.shape[-1]
    proj = norm_matmul(h, norm_mix[0], ev_w_in[0].astype(BF16), tm=ROW_TILE)
    y_lru = rglru(proj, lru_conv_w[0], lru_conv_b[0], _block_diag(lru_w_a[0]).astype(BF16), lru_b_a[0],
                  _block_diag(lru_w_x[0]).astype(BF16), lru_b_x[0], lru_lambda[0], ls=LRU_LS)
    y_s5 = s5(proj, 2 * lru_w // LANES,
              *_s5_params(s5_a_re[0], s5_a_im[0], s5_b_re[0], s5_b_im[0], s5_c_re[0], s5_c_im[0],
                          s5_d[0], s5_log_dt[0], s5_w_glu[0], s5_b_glu[0], ls=S5_LS), ls=S5_LS)
    h = outproj_slab(h, y_lru, y_s5, ev_w_out[0].astype(BF16), tm=ROW_TILE)
    h = mlp_residual(h.reshape(bsz * tp, d), norm_mlp[0], mlp_w_up[0], mlp_w_down[0],
                     norm_final, tm=ROW_TILE, tf=512,
                     final_norm=False).reshape(bsz, tp, d)

    d_inner = ssd_w_out.shape[1]
    conv_dim = ssd_conv_w.shape[-1]
    n_heads = ssd_dt_bias.shape[-1]
    w_in = ssd_w_in[0]
    w_zdt = jnp.concatenate([w_in[:, :d_inner], w_in[:, d_inner + conv_dim:],
                             jnp.zeros((d, LANES - n_heads), F32)], axis=1).astype(BF16)
    xbc = inproj_conv(h, norm_mix[1], w_in[:, d_inner:d_inner + conv_dim].astype(BF16),
                      ssd_conv_w[0], ssd_conv_b[0], tm=ROW_TILE, tn=1024)
    z, dt = inproj_zdt(h, norm_mix[1], w_zdt, tm=ROW_TILE)
    tables = ssd_prep(dt, _pad_lanes(ssd_dt_bias[0]), _pad_lanes(ssd_a_log[0]), chunks_per_step=SSD_PREP_CHUNKS)
    g = ssd_core(xbc, z, tables, jnp.repeat(ssd_d[0], SSD_HEAD_DIM).reshape(1, d_inner),
                 ssd_norm[0].reshape(1, d_inner))
    h = outproj(h, g, ssd_w_out[0].astype(BF16), tm=ROW_TILE)
    return mlp_residual_final(h, norm_mlp[1], mlp_w_up[1], mlp_w_down[1],
                              norm_final, seq=seq, tm=FINAL_ROW_TILE, tf=512)
```

```python
import functools
import math

import jax
import jax.numpy as jnp
from jax import lax
from jax.experimental import pallas as pl
from jax.experimental.pallas import tpu as pltpu

F32 = jnp.float32
BF16 = jnp.bfloat16

LANES = 128
SUBLANES = 8
NORM_EPS = 1e-5
LOG2_E = 1.4426950408889634
N_META = 16
CONV_WIDTH = 4
LRU_C = 8.0
S5_GROUP = 16
S5_STATE = 64
SSD_HEAD_DIM = 64
SSD_STATE = 128
SSD_CHUNK = 128
SSD_HPG = 4
SSD_CHUNKS_PER_STEP = 1
SSD_PREP_CHUNKS = 5

LRU_LS = 130
S5_SC = SUBLANES
S5_LS = 65
ROW_TILE = 1280
FINAL_ROW_TILE = 1024
VMEM_LIMIT = 48 * 1024 * 1024


def _cparams(n_axes, flags=None):
    return pltpu.CompilerParams(dimension_semantics=("arbitrary",) * n_axes,
                                vmem_limit_bytes=VMEM_LIMIT, flags=flags)


def _sigmoid(x):
    return 0.5 + 0.5 * jnp.tanh(0.5 * x)


def _silu(x):
    hx = 0.5 * x
    return hx + hx * jnp.tanh(hx)


def _gelu_tanh(x):
    return 0.5 * x * (1.0 + jnp.tanh(math.sqrt(2.0 / math.pi) * (x + 0.044715 * (x * x * x))))


def _log1p(e):
    u = 1.0 + e
    return jnp.where(u == 1.0, e, jnp.log(u) * (e / (u - 1.0)))


def _softplus(x):
    return jnp.maximum(x, 0.0) + _log1p(jnp.exp(-jnp.abs(x)))


def _rmsnorm_rows(x, w):
    ms = jnp.mean(x * x, axis=-1, keepdims=True)
    return x * lax.rsqrt(ms + NORM_EPS) * w


def _rows(tau):
    return pl.ds(pl.multiple_of(tau * SUBLANES, SUBLANES), SUBLANES)


def _norm_matmul_kernel(x_ref, nw_ref, w_ref, cw_ref, cb_ref, o_ref, rbuf, hist):
    tm = x_ref.shape[1]
    n_conv = rbuf.shape[0]
    hrows = SUBLANES

    @pl.when(pl.program_id(1) == 0)
    def _():
        hist[...] = jnp.zeros(hist.shape, F32)

    x = x_ref[0]
    rs = lax.rsqrt(jnp.mean(x * x, axis=-1, keepdims=True) + NORM_EPS)
    xb = (x * nw_ref[...]).astype(BF16)
    step = 4
    for c in range(o_ref.shape[1] // step):
        r = jnp.dot(xb, w_ref[:, LANES * step * c:LANES * step * (c + 1)], preferred_element_type=F32) * rs
        for k in range(step):
            s = step * c + k
            if s >= n_conv:
                o_ref[0, s] = r[:, LANES * k:LANES * (k + 1)]
                continue
            sl = slice(LANES * s, LANES * (s + 1))
            rbuf[s, 0:hrows, :] = hist[s]
            rbuf[s, hrows:hrows + tm, :] = r[:, LANES * k:LANES * (k + 1)]
            hist[s] = rbuf[s, tm:tm + hrows, :]
            taps = [cw_ref[m, :, sl] for m in range(CONV_WIDTH)]
            bias = cb_ref[:, sl]
            for g in range(tm // SUBLANES):
                r0 = hrows + SUBLANES * g
                acc = bias
                for m in range(CONV_WIDTH):
                    acc = acc + taps[m] * rbuf[s, r0 - m:r0 - m + SUBLANES, :]
                o_ref[0, s, SUBLANES * g:SUBLANES * (g + 1), :] = acc


def norm_matmul(h, nw, w, cw, cb, *, tm):
    b, tp, d = h.shape
    n = w.shape[1]
    wc = cw.shape[1]
    cw8 = jnp.broadcast_to(jnp.stack([cw[CONV_WIDTH - 1 - m] for m in range(CONV_WIDTH)])[:, None, :],
                           (CONV_WIDTH, SUBLANES, wc))
    cb8 = jnp.broadcast_to(cb[None, :], (SUBLANES, wc))
    const = lambda shape: pl.BlockSpec(shape, lambda bi, i: (0,) * len(shape))
    return pl.pallas_call(
        _norm_matmul_kernel,
        grid=(b, tp // tm),
        in_specs=[pl.BlockSpec((1, tm, d), lambda bi, i: (bi, i, 0)),
                  const((1, d)), const((d, n)), const((CONV_WIDTH, SUBLANES, wc)), const((SUBLANES, wc))],
        out_specs=pl.BlockSpec((1, n // LANES, tm, LANES), lambda bi, i: (bi, 0, i, 0)),
        out_shape=jax.ShapeDtypeStruct((b, n // LANES, tp, LANES), F32),
        scratch_shapes=[pltpu.VMEM((wc // LANES, SUBLANES + tm, LANES), F32),
                        pltpu.VMEM((wc // LANES, SUBLANES, LANES), F32)],
        compiler_params=_cparams(2),
        name="norm_matmul",
    )(h, nw.reshape(1, d), w, cw8, cb8)


def _inproj_conv_kernel(x_ref, nw_ref, w_ref, cw_ref, cb_ref, o_ref, xn_ref, rs, hist):
    i, j = pl.program_id(1), pl.program_id(2)
    tm = x_ref.shape[1]
    n_slab = o_ref.shape[1]
    hrows = SUBLANES
    pack = 2 * SUBLANES

    @pl.when(j == 0)
    def _():
        xn_ref[...] = _rmsnorm_rows(x_ref[0], nw_ref[...]).astype(BF16)

    @pl.when(i == 0)
    def _():
        hist[j] = jnp.zeros(hist.shape[1:], F32)

    xn = xn_ref[...]
    proj = lambda c: jnp.dot(xn, w_ref[:, 2 * LANES * c:2 * LANES * (c + 1)], preferred_element_type=F32)
    r_next = proj(0)
    for c in range(n_slab // 2):
        r = r_next
        if c + 1 < n_slab // 2:
            r_next = proj(c + 1)
        for k in range(2):
            s = 2 * c + k
            sl = slice(LANES * s, LANES * (s + 1))
            rs[s, 0:hrows, :] = hist[j, s]
            rs[s, hrows:hrows + tm, :] = r[:, LANES * k:LANES * (k + 1)]
            hist[j, s] = rs[s, tm:tm + hrows, :]
            taps = [cw_ref[m, :, sl] for m in range(CONV_WIDTH)]
            bias = cb_ref[:, sl]
            for g in range(tm // pack):
                halves = []
                for hh in range(pack // SUBLANES):
                    r0 = hrows + pack * g + SUBLANES * hh
                    acc = bias
                    for m in range(CONV_WIDTH):
                        acc = acc + taps[m] * rs[s, r0 - m:r0 - m + SUBLANES, :]
                    halves.append(_silu(acc))
                o_ref[0, s, pack * g:pack * (g + 1), :] = jnp.concatenate(halves, axis=0).astype(BF16)


def inproj_conv(h, nw, w, cw, cb, *, tm, tn):
    b, tp, d = h.shape
    n = w.shape[1]
    ns = tn // LANES
    cw8 = jnp.broadcast_to(jnp.stack([cw[CONV_WIDTH - 1 - m] for m in range(CONV_WIDTH)])[:, None, :],
                           (CONV_WIDTH, SUBLANES, n))
    cb8 = jnp.broadcast_to(cb.reshape(1, n), (SUBLANES, n))
    return pl.pallas_call(
        _inproj_conv_kernel,
        grid=(b, tp // tm, n // tn),
        in_specs=[pl.BlockSpec((1, tm, d), lambda bi, i, j: (bi, i, 0)),
                  pl.BlockSpec((1, d), lambda bi, i, j: (0, 0)),
                  pl.BlockSpec((d, tn), lambda bi, i, j: (0, j)),
                  pl.BlockSpec((CONV_WIDTH, SUBLANES, tn), lambda bi, i, j: (0, 0, j)),
                  pl.BlockSpec((SUBLANES, tn), lambda bi, i, j: (0, j))],
        out_specs=pl.BlockSpec((1, ns, tm, LANES), lambda bi, i, j: (bi, j, i, 0)),
        out_shape=jax.ShapeDtypeStruct((b, n // LANES, tp, LANES), BF16),
        scratch_shapes=[pltpu.VMEM((tm, d), BF16),
                        pltpu.VMEM((ns, SUBLANES + tm, LANES), F32),
                        pltpu.VMEM((n // tn, ns, SUBLANES, LANES), F32)],
        compiler_params=_cparams(3),
        name="inproj_conv",
    )(h, nw.reshape(1, d), w, cw8, cb8)


def _inproj_zdt_kernel(x_ref, nw_ref, w_ref, z_ref, dt_ref):
    x = x_ref[0]
    rs = lax.rsqrt(jnp.mean(x * x, axis=-1, keepdims=True) + NORM_EPS)
    xb = (x * nw_ref[...]).astype(BF16)
    n_z = z_ref.shape[1]
    step = 4
    for c in range(n_z // step):
        r = jnp.dot(xb, w_ref[:, LANES * step * c:LANES * step * (c + 1)], preferred_element_type=F32) * rs
        for k in range(step):
            z_ref[0, step * c + k] = r[:, LANES * k:LANES * (k + 1)].astype(BF16)
    dt_ref[0, 0] = jnp.dot(xb, w_ref[:, LANES * n_z:LANES * (n_z + 1)], preferred_element_type=F32) * rs


def inproj_zdt(h, nw, w, *, tm):
    b, tp, d = h.shape
    n_z = w.shape[1] // LANES - 1
    return pl.pallas_call(
        _inproj_zdt_kernel,
        grid=(b, tp // tm),
        in_specs=[pl.BlockSpec((1, tm, d), lambda bi, i: (bi, i, 0)),
                  pl.BlockSpec((1, d), lambda bi, i: (0, 0)),
                  pl.BlockSpec(w.shape, lambda bi, i: (0, 0))],
        out_specs=[pl.BlockSpec((1, n_z, tm, LANES), lambda bi, i: (bi, 0, i, 0)),
                   pl.BlockSpec((1, 1, tm, LANES), lambda bi, i: (bi, 0, i, 0))],
        out_shape=[jax.ShapeDtypeStruct((b, n_z, tp, LANES), BF16),
                   jax.ShapeDtypeStruct((b, 1, tp, LANES), F32)],
        compiler_params=_cparams(2),
        name="inproj_zdt",
    )(h, nw.reshape(1, d), w)


def _mlp_kernel(x_ref, nw_ref, wu_ref, wd_ref, fw_ref, o_ref, xn_ref, *, final_norm):
    j = pl.program_id(1)

    @pl.when(j == 0)
    def _():
        x = x_ref[...]
        xn_ref[...] = _rmsnorm_rows(x, nw_ref[...]).astype(BF16)
        o_ref[...] = x

    u = jnp.dot(xn_ref[...], wu_ref[...].astype(BF16), preferred_element_type=F32)
    a = jnp.square(jnp.maximum(u, 0.0)).astype(BF16)
    o_ref[...] += jnp.dot(a, wd_ref[...].astype(BF16), preferred_element_type=F32)

    if final_norm:
        @pl.when(j == pl.num_programs(1) - 1)
        def _():
            o_ref[...] = _rmsnorm_rows(o_ref[...], fw_ref[...])


def mlp_residual(h2, nw, w_up, w_down, fw, *, tm, tf, final_norm):
    n, d = h2.shape
    dff = w_up.shape[1]
    return pl.pallas_call(
        functools.partial(_mlp_kernel, final_norm=final_norm),
        grid=(n // tm, dff // tf),
        in_specs=[pl.BlockSpec((tm, d), lambda i, j: (i, 0)),
                  pl.BlockSpec((1, d), lambda i, j: (0, 0)),
                  pl.BlockSpec((d, tf), lambda i, j: (0, j)),
                  pl.BlockSpec((tf, d), lambda i, j: (j, 0)),
                  pl.BlockSpec((1, d), lambda i, j: (0, 0))],
        out_specs=pl.BlockSpec((tm, d), lambda i, j: (i, 0)),
        out_shape=jax.ShapeDtypeStruct((n, d), F32),
        scratch_shapes=[pltpu.VMEM((tm, d), BF16)],
        compiler_params=_cparams(2),
        name="mlp_final" if final_norm else "mlp",
    )(h2, nw.reshape(1, d), w_up, w_down, fw.reshape(1, d))


def _mlp_final_kernel(x_ref, nw_ref, wu_ref, wd_ref, fw_ref, o_ref, xn_ref):
    j = pl.program_id(2)

    @pl.when(j == 0)
    def _():
        x = x_ref[0]
        xn_ref[...] = _rmsnorm_rows(x, nw_ref[...]).astype(BF16)
        o_ref[0] = x

    u = jnp.dot(xn_ref[...], wu_ref[...].astype(BF16), preferred_element_type=F32)
    a = jnp.square(jnp.maximum(u, 0.0)).astype(BF16)
    o_ref[0] += jnp.dot(a, wd_ref[...].astype(BF16), preferred_element_type=F32)

    @pl.when(j == pl.num_programs(2) - 1)
    def _():
        o_ref[0] = _rmsnorm_rows(o_ref[0], fw_ref[...])


def mlp_residual_final(h, nw, w_up, w_down, fw, *, seq, tm, tf):
    b, _, d = h.shape
    dff = w_up.shape[1]
    return pl.pallas_call(
        _mlp_final_kernel,
        grid=(b, seq // tm, dff // tf),
        in_specs=[pl.BlockSpec((pl.Element(1), pl.Element(tm), pl.Element(d)),
                               lambda bi, i, j: (bi, pl.multiple_of(N_META + i * tm, SUBLANES), 0)),
                  pl.BlockSpec((1, d), lambda bi, i, j: (0, 0)),
                  pl.BlockSpec((d, tf), lambda bi, i, j: (0, j)),
                  pl.BlockSpec((tf, d), lambda bi, i, j: (j, 0)),
                  pl.BlockSpec((1, d), lambda bi, i, j: (0, 0))],
        out_specs=pl.BlockSpec((1, tm, d), lambda bi, i, j: (bi, i, 0)),
        out_shape=jax.ShapeDtypeStruct((b, seq, d), F32),
        scratch_shapes=[pltpu.VMEM((tm, d), BF16)],
        compiler_params=_cparams(3),
        name="mlp_final",
    )(h, nw.reshape(1, d), w_up, w_down, fw.reshape(1, d))


def _outproj_slab_kernel(h_ref, ya_ref, yb_ref, w_ref, o_ref):
    parts = [ya_ref[0, s] for s in range(ya_ref.shape[1])] + [yb_ref[0, s] for s in range(yb_ref.shape[1])]
    y = jnp.concatenate(parts, axis=-1).astype(BF16)
    o_ref[0] = h_ref[0] + jnp.dot(y, w_ref[...], preferred_element_type=F32)


def outproj_slab(h, ya, yb, w, *, tm):
    b, tp, d = h.shape
    sa, sb = ya.shape[1], yb.shape[1]
    return pl.pallas_call(
        _outproj_slab_kernel,
        grid=(b, tp // tm),
        in_specs=[pl.BlockSpec((1, tm, d), lambda bi, i: (bi, i, 0)),
                  pl.BlockSpec((1, sa, tm, LANES), lambda bi, i: (bi, 0, i, 0)),
                  pl.BlockSpec((1, sb, tm, LANES), lambda bi, i: (bi, 0, i, 0)),
                  pl.BlockSpec(w.shape, lambda bi, i: (0, 0))],
        out_specs=pl.BlockSpec((1, tm, d), lambda bi, i: (bi, i, 0)),
        out_shape=jax.ShapeDtypeStruct(h.shape, F32),
        compiler_params=_cparams(2),
        name="outproj_slab",
    )(h, ya, yb, w)


def _gated_outproj_kernel(h_ref, y_ref, z_ref, nw_ref, w_ref, o_ref, *, group):
    acc = h_ref[0]
    spg = group // LANES
    for g in range(y_ref.shape[2] // group):
        cols = slice(group * g, group * (g + 1))
        zg = jnp.concatenate([z_ref[0, spg * g + k] for k in range(spg)], axis=-1).astype(F32)
        gg = y_ref[0, :, cols].astype(F32) * _silu(zg)
        ms = jnp.mean(gg * gg, axis=-1, keepdims=True)
        part = (gg * lax.rsqrt(ms + NORM_EPS) * nw_ref[:, cols]).astype(BF16)
        acc = acc + jnp.dot(part, w_ref[cols, :], preferred_element_type=F32)
    o_ref[0] = acc


def gated_outproj(h, y, z, nw, w, *, tm, group):
    b, tp, d = h.shape
    k = y.shape[-1]
    return pl.pallas_call(
        functools.partial(_gated_outproj_kernel, group=group),
        grid=(b, tp // tm),
        in_specs=[pl.BlockSpec((1, tm, d), lambda bi, i: (bi, i, 0)),
                  pl.BlockSpec((1, tm, k), lambda bi, i: (bi, i, 0)),
                  pl.BlockSpec((1, k // LANES, tm, LANES), lambda bi, i: (bi, 0, i, 0)),
                  pl.BlockSpec((1, k), lambda bi, i: (0, 0)),
                  pl.BlockSpec(w.shape, lambda bi, i: (0, 0))],
        out_specs=pl.BlockSpec((1, tm, d), lambda bi, i: (bi, i, 0)),
        out_shape=jax.ShapeDtypeStruct(h.shape, F32),
        compiler_params=_cparams(2),
        name="gated_outproj",
    )(h, y, z, nw, w)


def _rglru_kernel(x_ref, g_ref, wa_ref, ba_ref, wx_ref, bx_ref, lam_ref, o_ref,
                  xcp, a_s, b_s, hn, carry, *, ls):
    n_slab = x_ref.shape[1]
    slabs = [slice(LANES * s, LANES * (s + 1)) for s in range(n_slab)]

    @pl.when(pl.program_id(1) == 0)
    def _():
        carry[...] = jnp.zeros(carry.shape, F32)

    def gather_body(tau, c):
        for s in range(n_slab):
            xcp[_rows(tau), slabs[s]] = x_ref[0, s, pl.ds(tau, SUBLANES, stride=ls), :]
        return c

    lax.fori_loop(0, ls, gather_body, 0, unroll=2)

    xc = xcp[...]
    xb = xc.astype(BF16)
    r = _sigmoid(jnp.dot(xb, wa_ref[...], preferred_element_type=F32) + ba_ref[...])
    i = _sigmoid(jnp.dot(xb, wx_ref[...], preferred_element_type=F32) + bx_ref[...])
    log_a = (-LRU_C) * r * _softplus(-lam_ref[...])
    a = jnp.exp(log_a)
    a_s[...] = a
    v = -jnp.tanh(log_a) * (a * a + 1.0)
    b_s[...] = jnp.where(v > 0.0, v * lax.rsqrt(v), 0.0) * (i * xc)

    def pass1(tau, c):
        out = []
        for s in range(n_slab):
            p, e = c[2 * s], c[2 * s + 1]
            av = a_s[_rows(tau), slabs[s]]
            out += [p * av, av * e + b_s[_rows(tau), slabs[s]]]
        return tuple(out)

    one = jnp.ones((SUBLANES, LANES), F32)
    zero = jnp.zeros((SUBLANES, LANES), F32)
    pe = lax.fori_loop(0, ls, pass1, (one, zero) * n_slab, unroll=2)

    h0 = []
    for s in range(n_slab):
        p, e = pe[2 * s], pe[2 * s + 1]
        c = carry[0:1, slabs[s]]
        rows = []
        for j in range(SUBLANES):
            rows.append(c)
            c = p[j:j + 1] * c + e[j:j + 1]
        carry[0:1, slabs[s]] = c
        h0.append(jnp.concatenate(rows, axis=0))

    def pass2(tau, hs):
        out = []
        for s in range(n_slab):
            h = a_s[_rows(tau), slabs[s]] * hs[s] + b_s[_rows(tau), slabs[s]]
            hn[s, pl.ds(tau, SUBLANES, stride=ls), :] = h
            out.append(h)
        return tuple(out)

    lax.fori_loop(0, ls, pass2, tuple(h0), unroll=2)
    for s in range(n_slab):
        o_ref[0, s] = hn[s] * _gelu_tanh(g_ref[0, s])


def rglru(proj, wa_bd, ba, wx_bd, bx, lam, *, ls):
    b, _, tp, _ = proj.shape
    w = lam.shape[0]
    ns = w // LANES
    tt = SUBLANES * ls
    const = lambda shape: pl.BlockSpec(shape, lambda bi, t: (0,) * len(shape))
    return pl.pallas_call(
        functools.partial(_rglru_kernel, ls=ls),
        grid=(b, tp // tt),
        in_specs=[pl.BlockSpec((1, ns, tt, LANES), lambda bi, t: (bi, 0, t, 0)),
                  pl.BlockSpec((1, ns, tt, LANES), lambda bi, t: (bi, 1, t, 0)),
                  const((w, w)), const((1, w)), const((w, w)), const((1, w)), const((1, w))],
        out_specs=pl.BlockSpec((1, ns, tt, LANES), lambda bi, t: (bi, 0, t, 0)),
        out_shape=jax.ShapeDtypeStruct((b, ns, tp, LANES), F32),
        scratch_shapes=[pltpu.VMEM((tt, w), F32), pltpu.VMEM((tt, w), F32), pltpu.VMEM((tt, w), F32),
                        pltpu.VMEM((ns, tt, LANES), F32), pltpu.VMEM((SUBLANES, w), F32)],
        compiler_params=_cparams(2),
        name="rglru",
    )(proj, proj, wa_bd, ba.reshape(1, w), wx_bd, bx.reshape(1, w), lam.reshape(1, w))


def _s5_kernel(u_ref, bend_ref, kc_ref, l8r_ref, l8i_ref, plr_ref, pli_ref, d_ref, wg_ref, bg_ref,
               o_ref, lp, st, yv, carry, *, ls):
    sc = S5_SC
    n_cs = l8r_ref.shape[1]
    stride = sc * ls
    cre = [slice(LANES * k, LANES * (k + 1)) for k in range(n_cs)]
    cim = [slice(LANES * (n_cs + k), LANES * (n_cs + k + 1)) for k in range(n_cs)]
    lanes = [slice(LANES * q, LANES * (q + 1)) for q in range(sc)]

    @pl.when(pl.program_id(2) == 0)
    def _():
        carry[...] = jnp.zeros(carry.shape, F32)

    def gather_body(tau, c):
        for sg in range(sc):
            lp[_rows(tau), lanes[sg]] = u_ref[0, 0, pl.ds(tau * sc + sg, SUBLANES, stride=stride), :]
        return c

    lax.fori_loop(0, ls, gather_body, 0)
    st[...] = jnp.dot(lp[...].astype(BF16), bend_ref[0], preferred_element_type=F32)

    lam = [(jnp.broadcast_to(l8r_ref[0, k], (SUBLANES, LANES)),
            jnp.broadcast_to(l8i_ref[0, k], (SUBLANES, LANES))) for k in range(n_cs)]

    def step(tau, k, sr, si):
        lr, li = lam[k]
        return (lr * sr - li * si + st[_rows(tau), cre[k]], lr * si + li * sr + st[_rows(tau), cim[k]])

    def pass1(tau, c):
        out = []
        for k in range(n_cs):
            out += list(step(tau, k, c[2 * k], c[2 * k + 1]))
        return tuple(out)

    zero = jnp.zeros((SUBLANES, LANES), F32)
    ends = lax.fori_loop(0, ls, pass1, (zero,) * (2 * n_cs))

    starts = []
    for k in range(n_cs):
        er, ei = ends[2 * k], ends[2 * k + 1]
        pr, pi = plr_ref[0, k], pli_ref[0, k]
        c_r, c_i = carry[0:1, cre[k]], carry[0:1, cim[k]]
        rows_r, rows_i = [], []
        for j in range(SUBLANES):
            rows_r.append(c_r)
            rows_i.append(c_i)
            c_r, c_i = (pr * c_r - pi * c_i + er[j:j + 1], pr * c_i + pi * c_r + ei[j:j + 1])
        carry[0:1, cre[k]] = c_r
        carry[0:1, cim[k]] = c_i
        starts += [jnp.concatenate(rows_r, axis=0), jnp.concatenate(rows_i, axis=0)]

    def pass2(tau, c):
        out = []
        for k in range(n_cs):
            nr, ni = step(tau, k, c[2 * k], c[2 * k + 1])
            st[_rows(tau), cre[k]] = c[2 * k]
            st[_rows(tau), cim[k]] = c[2 * k + 1]
            out += [nr, ni]
        return tuple(out)

    lax.fori_loop(0, ls, pass2, tuple(starts))

    lhs = jnp.concatenate([lp[...], st[...]], axis=1).astype(BF16)
    yv[...] = jnp.dot(lhs, kc_ref[0], preferred_element_type=F32)
    for q in range(sc):
        y = yv[:, lanes[q]] + d_ref[0] * lp[:, lanes[q]]
        y = _gelu_tanh(y)
        yv[:, lanes[q]] = y * _sigmoid(jnp.dot(y.astype(BF16), wg_ref[0], preferred_element_type=F32)
                                       + bg_ref[0])

    def scatter_body(tau, c):
        for sg in range(sc):
            o_ref[0, 0, pl.ds(tau * sc + sg, SUBLANES, stride=stride), :] = yv[_rows(tau), lanes[sg]]
        return c

    lax.fori_loop(0, ls, scatter_body, 0)


def s5(proj, first_slab, bend, kc, l8r, l8i, plr, pli, d, wg, bg, *, ls):
    b, _, tp, _ = proj.shape
    nblk, kin, nst = bend.shape
    n_cs = nst // (2 * LANES)
    tt = SUBLANES * ls * S5_SC
    nc = SUBLANES * ls
    per_blk = lambda shape: pl.BlockSpec((1,) + shape, lambda bi, gb, t: (gb,) + (0,) * len(shape))
    return pl.pallas_call(
        functools.partial(_s5_kernel, ls=ls),
        grid=(b, nblk, tp // tt),
        in_specs=[pl.BlockSpec((1, 1, tt, LANES), lambda bi, gb, t: (bi, first_slab + gb, t, 0)),
                  per_blk((kin, nst)), per_blk((kin + nst, kin)),
                  per_blk((n_cs, 1, LANES)), per_blk((n_cs, 1, LANES)),
                  per_blk((n_cs, 1, LANES)), per_blk((n_cs, 1, LANES)),
                  per_blk((1, LANES)), per_blk((LANES, LANES)), per_blk((1, LANES))],
        out_specs=pl.BlockSpec((1, 1, tt, LANES), lambda bi, gb, t: (bi, gb, t, 0)),
        out_shape=jax.ShapeDtypeStruct((b, nblk, tp, LANES), F32),
        scratch_shapes=[pltpu.VMEM((nc, kin), F32), pltpu.VMEM((nc, nst), F32),
                        pltpu.VMEM((nc, kin), F32), pltpu.VMEM((SUBLANES, nst), F32)],
        compiler_params=_cparams(3),
        name="s5",
    )(proj, bend, kc, l8r, l8i, plr, pli, d, wg, bg)


def _s5_params(a_re, a_im, b_re, b_im, c_re, c_im, d, log_dt, w_glu, b_glu, *, ls):
    g, p = a_re.shape
    sc = S5_SC
    gpb = LANES // S5_GROUP
    nblk = g // gpb
    dt = jnp.exp(log_dt)[:, None]
    mag = jnp.exp(a_re * dt)
    ar, ai = mag * jnp.cos(a_im * dt), mag * jnp.sin(a_im * dt)
    den = a_re * a_re + a_im * a_im
    fr = ((ar - 1.0) * a_re + ai * a_im) / den
    fi = (ai * a_re - (ar - 1.0) * a_im) / den
    bbar_re = fr[..., None] * b_re - fi[..., None] * b_im
    bbar_im = fr[..., None] * b_im + fi[..., None] * b_re

    def lam_pow(k):
        k = jnp.asarray(k, F32).reshape(-1, 1, 1)
        m = jnp.exp(a_re * dt * k)
        return m * jnp.cos(a_im * dt * k), m * jnp.sin(a_im * dt * k)

    pw_r, pw_i = lam_pow(jnp.arange(sc + 1))
    def group_diag(dense, rows_per_group):
        n = dense.shape[-1]
        tiled = jnp.tile(dense, (1,) * (dense.ndim - 1) + (gpb,))
        rg = lax.broadcasted_iota(jnp.int32, tiled.shape[-2:], 0) // rows_per_group
        cg = lax.broadcasted_iota(jnp.int32, tiled.shape[-2:], 1) // n
        return jnp.where(rg == cg, tiled, 0.0)

    def by_block(m):
        k, _, a, b2 = m.shape
        return jnp.transpose(m.reshape(k, nblk, gpb, a, b2), (0, 1, 2, 4, 3)).reshape(k, nblk, gpb * b2, a)

    wr = jnp.stack([pw_r[sc - 1 - s] for s in range(sc)])[..., None]
    wi = jnp.stack([pw_i[sc - 1 - s] for s in range(sc)])[..., None]
    e_re = wr * bbar_re - wi * bbar_im
    e_im = wr * bbar_im + wi * bbar_re
    rows_cat = lambda m: jnp.concatenate([m[k] for k in range(m.shape[0])], axis=-2)
    bend = jnp.concatenate([rows_cat(group_diag(by_block(e_re), S5_GROUP)),
                            rows_cat(group_diag(by_block(e_im), S5_GROUP))], axis=-1).astype(BF16)

    cl_re = c_re[None] * pw_r[:, :, None, :] - c_im[None] * pw_i[:, :, None, :]
    cl_im = c_re[None] * pw_i[:, :, None, :] + c_im[None] * pw_r[:, :, None, :]
    kl = (jnp.einsum('kgip,gpj->kgij', cl_re[:sc], bbar_re)
          - jnp.einsum('kgip,gpj->kgij', cl_im[:sc], bbar_im))
    kd = group_diag(by_block(kl), S5_GROUP)
    zero_blk = jnp.zeros_like(kd[0])
    kintra = jnp.concatenate(
        [jnp.concatenate([kd[t - s] if t >= s else zero_blk for t in range(sc)], axis=-1)
         for s in range(sc)], axis=-2)

    def out_bd(m):
        dense = jnp.transpose(m, (0, 1, 3, 2))
        d2 = group_diag(dense.reshape(sc, nblk, gpb * p, S5_GROUP), p)
        return jnp.concatenate([d2[t] for t in range(sc)], axis=-1)

    kc = jnp.concatenate([kintra, out_bd(cl_re[1:]), -out_bd(cl_im[1:])], axis=-2).astype(BF16)

    vec = lambda v: v.reshape(nblk, (gpb * p) // LANES, 1, LANES)
    pl_r, pl_i = lam_pow(jnp.asarray([sc * ls]))
    wg = group_diag(w_glu.reshape(nblk, LANES, S5_GROUP), S5_GROUP).astype(BF16)
    return (bend, kc, vec(pw_r[sc]), vec(pw_i[sc]), vec(pl_r[0]), vec(pl_i[0]),
            d.reshape(nblk, 1, LANES), wg, b_glu.reshape(nblk, 1, LANES))


def _ssd_prep_kernel(dt_ref, dtb_ref, alog_ref, cs_ref, rowp_ref, wend_ref, cdec_ref):
    L = SSD_CHUNK
    row = lax.broadcasted_iota(jnp.int32, (L, L), 0)
    col = lax.broadcasted_iota(jnp.int32, (L, L), 1)
    tri = (row >= col).astype(BF16)
    a = -jnp.exp(alog_ref[...])
    for c in range(cs_ref.shape[1]):
        dt = _softplus(dt_ref[0, 0, L * c:L * (c + 1), :] + dtb_ref[...])
        da = dt * a
        da_hi = da.astype(BF16)
        r1 = da - da_hi.astype(F32)
        da_mid = r1.astype(BF16)
        da_lo = (r1 - da_mid.astype(F32)).astype(BF16)
        cs = (jnp.dot(tri, da_hi, preferred_element_type=F32)
              + jnp.dot(tri, da_mid, preferred_element_type=F32)
              + jnp.dot(tri, da_lo, preferred_element_type=F32))
        dt_t = dt.T
        cs_t = cs.T
        last_t = jnp.broadcast_to(cs_t[:, L - 1:L], (L, L))
        cs_ref[0, c] = cs * LOG2_E
        rowp_ref[0, c] = (cs_t - jnp.log(dt_t)) * LOG2_E
        wend_ref[0, c] = dt_t * jnp.exp(last_t - cs_t)
        cdec_ref[0, c] = jnp.exp(last_t)


def ssd_prep(dt, dtb, alog, *, chunks_per_step):
    b, _, tp, _ = dt.shape
    L = SSD_CHUNK
    cp = chunks_per_step
    nc = tp // L
    const = lambda shape: pl.BlockSpec(shape, lambda bi, c: (0,) * len(shape))
    tab = pl.BlockSpec((1, cp, L, LANES), lambda bi, c: (bi, c, 0, 0))
    return pl.pallas_call(
        _ssd_prep_kernel,
        grid=(b, nc // cp),
        in_specs=[pl.BlockSpec((1, 1, cp * L, LANES), lambda bi, c: (bi, 0, c, 0)),
                  const((1, LANES)), const((1, LANES))],
        out_specs=[tab] * 4,
        out_shape=[jax.ShapeDtypeStruct((b, nc, L, LANES), F32)] * 4,
        compiler_params=_cparams(2),
        name="ssd_prep",
    )(dt, dtb, alog)


def _ssd_kernel(xbc_ref, cs_ref, rowp_ref, wend_ref, cdec_ref, dskip_ref, o_ref, hst):
    @pl.when(pl.program_id(1) == 0)
    def _():
        hst[...] = jnp.zeros(hst.shape, F32)

    for c in range(xbc_ref.shape[2] // SSD_CHUNK):
        _ssd_chunk(xbc_ref, cs_ref, rowp_ref, wend_ref, cdec_ref, dskip_ref, o_ref, hst, c)


def _ssd_chunk(xbc_ref, cs_ref, rowp_ref, wend_ref, cdec_ref, dskip_ref, o_ref, hst, c):
    L = SSD_CHUNK
    rs = slice(L * c, L * (c + 1))
    n_groups = hst.shape[0]
    gw = SSD_HPG * SSD_HEAD_DIM
    spg = gw // LANES
    n_xs = n_groups * spg
    row = lax.broadcasted_iota(jnp.int32, (L, L), 0)
    col = lax.broadcasted_iota(jnp.int32, (L, L), 1)
    causal = row >= col
    left_half = col < SSD_HEAD_DIM
    head_of_lane = lax.broadcasted_iota(jnp.int32, (L, gw), 1) // SSD_HEAD_DIM

    def head_rows(ref, g):
        return jnp.concatenate(
            [jnp.broadcast_to(ref[0, c, SSD_HPG * g + r:SSD_HPG * g + r + 1, :], (SSD_HEAD_DIM, L))
             for r in range(SSD_HPG)], axis=0)

    def x_slabs(g):
        return jnp.concatenate([xbc_ref[0, spg * g + k, rs, :] for k in range(spg)], axis=-1)

    scores, y_offs = [], []
    for g in range(n_groups):
        bm = xbc_ref[0, n_xs + g, rs, :]
        cm = xbc_ref[0, n_xs + n_groups + g, rs, :]
        scores.append(lax.dot_general(cm, bm, (((1,), (1,)), ((), ())), preferred_element_type=F32))
        h_prev = hst[g]
        y_offs.append(lax.dot_general(cm, h_prev.astype(BF16), (((1,), (1,)), ((), ())),
                                      preferred_element_type=F32))
        xw_t = (x_slabs(g).astype(F32).T * head_rows(wend_ref, g)).astype(BF16)
        hst[g] = head_rows(cdec_ref, g) * h_prev + jnp.dot(xw_t, bm, preferred_element_type=F32)

    for g in range(n_groups):
        xsb = x_slabs(g)
        ms, bcs = [], []
        for r in range(SSD_HPG):
            h = SSD_HPG * g + r
            bcs.append(jnp.broadcast_to(cs_ref[0, c, :, h:h + 1], (L, L)))
            ms.append((scores[g] * jnp.exp2(jnp.where(causal, bcs[r] - rowp_ref[0, c, h:h + 1, :], -jnp.inf))
                       ).astype(BF16))
        x_bd = jnp.concatenate([jnp.where(head_of_lane == r, xsb, jnp.zeros_like(xsb))
                                for r in range(SSD_HPG)], axis=0)
        y_diag = jnp.dot(jnp.concatenate(ms, axis=1), x_bd, preferred_element_type=F32)
        f_start = jnp.concatenate([jnp.exp2(jnp.where(left_half, bcs[2 * k], bcs[2 * k + 1]))
                                   for k in range(spg)], axis=-1)
        o_ref[0, rs, gw * g:gw * (g + 1)] = (
            y_diag + y_offs[g] * f_start + dskip_ref[:, gw * g:gw * (g + 1)] * xsb.astype(F32)
        ).astype(o_ref.dtype)


def ssd_core(xbc, tables, dskip):
    b, n_conv, tp, _ = xbc.shape
    d_inner = dskip.shape[1]
    n_groups = d_inner // (SSD_HPG * SSD_HEAD_DIM)
    cps = SSD_CHUNKS_PER_STEP
    L = SSD_CHUNK * cps
    const = lambda shape: pl.BlockSpec(shape, lambda bi, c: (0,) * len(shape))
    tab = pl.BlockSpec((1, cps, SSD_CHUNK, LANES), lambda bi, c: (bi, c, 0, 0))
    return pl.pallas_call(
        _ssd_kernel,
        grid=(b, tp // L),
        in_specs=[pl.BlockSpec((1, n_conv, L, LANES), lambda bi, c: (bi, 0, c, 0)),
                  tab, tab, tab, tab, const((1, d_inner))],
        out_specs=pl.BlockSpec((1, L, d_inner), lambda bi, c: (bi, c, 0)),
        out_shape=jax.ShapeDtypeStruct((b, tp, d_inner), BF16),
        scratch_shapes=[pltpu.VMEM((n_groups, SSD_HPG * SSD_HEAD_DIM, SSD_STATE), F32)],
        compiler_params=_cparams(2),
        name="ssd_core",
    )(xbc, *tables, dskip)


def _block_diag(w):
    n, di, do = w.shape
    return jnp.einsum('gij,gh->gihj', w, jnp.eye(n, dtype=w.dtype)).reshape(n * di, n * do)


def _pad_lanes(v, fill=0.0):
    return jnp.pad(v, (0, LANES - v.shape[0]), constant_values=fill).reshape(1, LANES)


def kernel(x, meta_tokens, norm_mix, norm_mlp, norm_final, ev_w_in, lru_conv_w, lru_conv_b, lru_w_a, lru_b_a, lru_w_x, lru_b_x, lru_lambda, s5_a_re, s5_a_im, s5_b_re, s5_b_im, s5_c_re, s5_c_im, s5_d, s5_log_dt, s5_w_glu, s5_b_glu, ev_w_out, ssd_w_in, ssd_conv_w, ssd_conv_b, ssd_dt_bias, ssd_a_log, ssd_d, ssd_norm, ssd_w_out, mlp_w_up, mlp_w_down):
    bsz, seq, d = x.shape
    t = seq + N_META
    unit = math.lcm(SUBLANES * LRU_LS, SUBLANES * S5_LS * S5_SC, ROW_TILE, SSD_CHUNK)
    tp = -(-t // unit) * unit
    meta = jnp.broadcast_to(meta_tokens[None].astype(x.dtype), (bsz, N_META, d))
    h = jnp.concatenate([meta, x, jnp.zeros((bsz, tp - t, d), x.dtype)], axis=1)

    lru_w = lru_conv_w.shape[-1]
    proj = norm_matmul(h, norm_mix[0], ev_w_in[0].astype(BF16), lru_conv_w[0], lru_conv_b[0], tm=ROW_TILE)
    y_lru = rglru(proj, _block_diag(lru_w_a[0]).astype(BF16), lru_b_a[0],
                  _block_diag(lru_w_x[0]).astype(BF16), lru_b_x[0], lru_lambda[0], ls=LRU_LS)
    y_s5 = s5(proj, 2 * lru_w // LANES,
              *_s5_params(s5_a_re[0], s5_a_im[0], s5_b_re[0], s5_b_im[0], s5_c_re[0], s5_c_im[0],
                          s5_d[0], s5_log_dt[0], s5_w_glu[0], s5_b_glu[0], ls=S5_LS), ls=S5_LS)
    h = outproj_slab(h, y_lru, y_s5, ev_w_out[0].astype(BF16), tm=ROW_TILE)
    h = mlp_residual(h.reshape(bsz * tp, d), norm_mlp[0], mlp_w_up[0], mlp_w_down[0],
                     norm_final, tm=ROW_TILE, tf=512,
                     final_norm=False).reshape(bsz, tp, d)

    d_inner = ssd_w_out.shape[1]
    conv_dim = ssd_conv_w.shape[-1]
    n_heads = ssd_dt_bias.shape[-1]
    w_in = ssd_w_in[0]
    w_zdt = jnp.concatenate([w_in[:, :d_inner], w_in[:, d_inner + conv_dim:],
                             jnp.zeros((d, LANES - n_heads), F32)], axis=1).astype(BF16)
    xbc = inproj_conv(h, norm_mix[1], w_in[:, d_inner:d_inner + conv_dim].astype(BF16),
                      ssd_conv_w[0], ssd_conv_b[0], tm=ROW_TILE, tn=1024)
    z, dt = inproj_zdt(h, norm_mix[1], w_zdt, tm=ROW_TILE)
    tables = ssd_prep(dt, _pad_lanes(ssd_dt_bias[0]), _pad_lanes(ssd_a_log[0]), chunks_per_step=SSD_PREP_CHUNKS)
    y = ssd_core(xbc, tables, jnp.repeat(ssd_d[0], SSD_HEAD_DIM).reshape(1, d_inner))
    h = gated_outproj(h, y, z, ssd_norm[0].reshape(1, d_inner), ssd_w_out[0].astype(BF16),
                      tm=ROW_TILE // 2, group=SSD_HPG * SSD_HEAD_DIM)
    return mlp_residual_final(h, norm_mlp[1], mlp_w_up[1], mlp_w_down[1],
                              norm_final, seq=seq, tm=FINAL_ROW_TILE, tf=512)
```

```python
import functools
import math

import jax
import jax.numpy as jnp
from jax import lax
from jax.experimental import pallas as pl
from jax.experimental.pallas import tpu as pltpu

F32 = jnp.float32
BF16 = jnp.bfloat16

LANES = 128
SUBLANES = 8
NORM_EPS = 1e-5
LOG2_E = 1.4426950408889634
N_META = 16
CONV_WIDTH = 4
LRU_C = 8.0
S5_GROUP = 16
S5_STATE = 64
SSD_HEAD_DIM = 64
SSD_STATE = 128
SSD_CHUNK = 128
SSD_HPG = 4
SSD_CHUNKS_PER_STEP = 5
SSD_PREP_CHUNKS = 5

LRU_LS = 130
S5_SC = SUBLANES
S5_LS = 65
ROW_TILE = 1280
FINAL_ROW_TILE = 1024
VMEM_LIMIT = 48 * 1024 * 1024


def _cparams(n_axes):
    return pltpu.CompilerParams(dimension_semantics=("arbitrary",) * n_axes,
                                vmem_limit_bytes=VMEM_LIMIT)


def _sigmoid(x):
    return 0.5 + 0.5 * jnp.tanh(0.5 * x)


def _silu(x):
    hx = 0.5 * x
    return hx + hx * jnp.tanh(hx)


def _gelu_tanh(x):
    return 0.5 * x * (1.0 + jnp.tanh(math.sqrt(2.0 / math.pi) * (x + 0.044715 * (x * x * x))))


def _log1p(e):
    u = 1.0 + e
    return jnp.where(u == 1.0, e, jnp.log(u) * (e / (u - 1.0)))


def _softplus(x):
    return jnp.maximum(x, 0.0) + _log1p(jnp.exp(-jnp.abs(x)))


def _rmsnorm_rows(x, w):
    ms = jnp.mean(x * x, axis=-1, keepdims=True)
    return x * lax.rsqrt(ms + NORM_EPS) * w


def _rows(tau):
    return pl.ds(pl.multiple_of(tau * SUBLANES, SUBLANES), SUBLANES)


def _norm_matmul_kernel(x_ref, nw_ref, w_ref, cw_ref, cb_ref, o_ref, rbuf, hist):
    tm = x_ref.shape[1]
    n_conv = rbuf.shape[0]
    hrows = SUBLANES

    @pl.when(pl.program_id(1) == 0)
    def _():
        hist[...] = jnp.zeros(hist.shape, F32)

    x = x_ref[0]
    rs = lax.rsqrt(jnp.mean(x * x, axis=-1, keepdims=True) + NORM_EPS)
    xb = (x * nw_ref[...]).astype(BF16)
    step = 4
    for c in range(o_ref.shape[1] // step):
        r = jnp.dot(xb, w_ref[:, LANES * step * c:LANES * step * (c + 1)], preferred_element_type=F32) * rs
        for k in range(step):
            s = step * c + k
            if s >= n_conv:
                o_ref[0, s] = r[:, LANES * k:LANES * (k + 1)]
                continue
            sl = slice(LANES * s, LANES * (s + 1))
            rbuf[s, 0:hrows, :] = hist[s]
            rbuf[s, hrows:hrows + tm, :] = r[:, LANES * k:LANES * (k + 1)]
            hist[s] = rbuf[s, tm:tm + hrows, :]
            taps = [cw_ref[m, :, sl] for m in range(CONV_WIDTH)]
            bias = cb_ref[:, sl]
            for g in range(tm // SUBLANES):
                r0 = hrows + SUBLANES * g
                acc = bias
                for m in range(CONV_WIDTH):
                    acc = acc + taps[m] * rbuf[s, r0 - m:r0 - m + SUBLANES, :]
                o_ref[0, s, SUBLANES * g:SUBLANES * (g + 1), :] = acc


def norm_matmul(h, nw, w, cw, cb, *, tm):
    b, tp, d = h.shape
    n = w.shape[1]
    wc = cw.shape[1]
    cw8 = jnp.broadcast_to(jnp.stack([cw[CONV_WIDTH - 1 - m] for m in range(CONV_WIDTH)])[:, None, :],
                           (CONV_WIDTH, SUBLANES, wc))
    cb8 = jnp.broadcast_to(cb[None, :], (SUBLANES, wc))
    const = lambda shape: pl.BlockSpec(shape, lambda bi, i: (0,) * len(shape))
    return pl.pallas_call(
        _norm_matmul_kernel,
        grid=(b, tp // tm),
        in_specs=[pl.BlockSpec((1, tm, d), lambda bi, i: (bi, i, 0)),
                  const((1, d)), const((d, n)), const((CONV_WIDTH, SUBLANES, wc)), const((SUBLANES, wc))],
        out_specs=pl.BlockSpec((1, n // LANES, tm, LANES), lambda bi, i: (bi, 0, i, 0)),
        out_shape=jax.ShapeDtypeStruct((b, n // LANES, tp, LANES), F32),
        scratch_shapes=[pltpu.VMEM((wc // LANES, SUBLANES + tm, LANES), F32),
                        pltpu.VMEM((wc // LANES, SUBLANES, LANES), F32)],
        compiler_params=_cparams(2),
        name="norm_matmul",
    )(h, nw.reshape(1, d), w, cw8, cb8)


def _inproj_conv_kernel(x_ref, nw_ref, w_ref, cw_ref, cb_ref, o_ref, xn_ref, rs, hist):
    i, j = pl.program_id(1), pl.program_id(2)
    tm = x_ref.shape[1]
    n_slab = o_ref.shape[1]
    hrows = SUBLANES
    pack = 2 * SUBLANES

    @pl.when(j == 0)
    def _():
        xn_ref[...] = _rmsnorm_rows(x_ref[0], nw_ref[...]).astype(BF16)

    @pl.when(i == 0)
    def _():
        hist[j] = jnp.zeros(hist.shape[1:], F32)

    xn = xn_ref[...]
    proj = lambda c: jnp.dot(xn, w_ref[:, 2 * LANES * c:2 * LANES * (c + 1)], preferred_element_type=F32)
    r_next = proj(0)
    for c in range(n_slab // 2):
        r = r_next
        if c + 1 < n_slab // 2:
            r_next = proj(c + 1)
        for k in range(2):
            s = 2 * c + k
            sl = slice(LANES * s, LANES * (s + 1))
            rs[s, 0:hrows, :] = hist[j, s]
            rs[s, hrows:hrows + tm, :] = r[:, LANES * k:LANES * (k + 1)]
            hist[j, s] = rs[s, tm:tm + hrows, :]
            taps = [cw_ref[m, :, sl] for m in range(CONV_WIDTH)]
            bias = cb_ref[:, sl]
            for g in range(tm // pack):
                halves = []
                for hh in range(pack // SUBLANES):
                    r0 = hrows + pack * g + SUBLANES * hh
                    acc = bias
                    for m in range(CONV_WIDTH):
                        acc = acc + taps[m] * rs[s, r0 - m:r0 - m + SUBLANES, :]
                    halves.append(acc)
                o_ref[0, s, pack * g:pack * (g + 1), :] = _silu(jnp.concatenate(halves, axis=0).astype(BF16))


def inproj_conv(h, nw, w, cw, cb, *, tm, tn):
    b, tp, d = h.shape
    n = w.shape[1]
    ns = tn // LANES
    cw8 = jnp.broadcast_to(jnp.stack([cw[CONV_WIDTH - 1 - m] for m in range(CONV_WIDTH)])[:, None, :],
                           (CONV_WIDTH, SUBLANES, n))
    cb8 = jnp.broadcast_to(cb.reshape(1, n), (SUBLANES, n))
    return pl.pallas_call(
        _inproj_conv_kernel,
        grid=(b, tp // tm, n // tn),
        in_specs=[pl.BlockSpec((1, tm, d), lambda bi, i, j: (bi, i, 0)),
                  pl.BlockSpec((1, d), lambda bi, i, j: (0, 0)),
                  pl.BlockSpec((d, tn), lambda bi, i, j: (0, j)),
                  pl.BlockSpec((CONV_WIDTH, SUBLANES, tn), lambda bi, i, j: (0, 0, j)),
                  pl.BlockSpec((SUBLANES, tn), lambda bi, i, j: (0, j))],
        out_specs=pl.BlockSpec((1, ns, tm, LANES), lambda bi, i, j: (bi, j, i, 0)),
        out_shape=jax.ShapeDtypeStruct((b, n // LANES, tp, LANES), BF16),
        scratch_shapes=[pltpu.VMEM((tm, d), BF16),
                        pltpu.VMEM((ns, SUBLANES + tm, LANES), F32),
                        pltpu.VMEM((n // tn, ns, SUBLANES, LANES), F32)],
        compiler_params=_cparams(3),
        name="inproj_conv",
    )(h, nw.reshape(1, d), w, cw8, cb8)


def _inproj_zdt_kernel(x_ref, nw_ref, w_ref, z_ref, dt_ref):
    x = x_ref[0]
    rs = lax.rsqrt(jnp.mean(x * x, axis=-1, keepdims=True) + NORM_EPS)
    xb = (x * nw_ref[...]).astype(BF16)
    n_z = z_ref.shape[1]
    step = 4
    for c in range(n_z // step):
        r = jnp.dot(xb, w_ref[:, LANES * step * c:LANES * step * (c + 1)], preferred_element_type=F32) * rs
        for k in range(step):
            z_ref[0, step * c + k] = r[:, LANES * k:LANES * (k + 1)].astype(BF16)
    dt_ref[0, 0] = jnp.dot(xb, w_ref[:, LANES * n_z:LANES * (n_z + 1)], preferred_element_type=F32) * rs


def inproj_zdt(h, nw, w, *, tm):
    b, tp, d = h.shape
    n_z = w.shape[1] // LANES - 1
    return pl.pallas_call(
        _inproj_zdt_kernel,
        grid=(b, tp // tm),
        in_specs=[pl.BlockSpec((1, tm, d), lambda bi, i: (bi, i, 0)),
                  pl.BlockSpec((1, d), lambda bi, i: (0, 0)),
                  pl.BlockSpec(w.shape, lambda bi, i: (0, 0))],
        out_specs=[pl.BlockSpec((1, n_z, tm, LANES), lambda bi, i: (bi, 0, i, 0)),
                   pl.BlockSpec((1, 1, tm, LANES), lambda bi, i: (bi, 0, i, 0))],
        out_shape=[jax.ShapeDtypeStruct((b, n_z, tp, LANES), BF16),
                   jax.ShapeDtypeStruct((b, 1, tp, LANES), F32)],
        compiler_params=_cparams(2),
        name="inproj_zdt",
    )(h, nw.reshape(1, d), w)


def _mlp_kernel(x_ref, nw_ref, wu_ref, wd_ref, fw_ref, o_ref, xn_ref, *, final_norm):
    j = pl.program_id(1)

    @pl.when(j == 0)
    def _():
        x = x_ref[...]
        xn_ref[...] = _rmsnorm_rows(x, nw_ref[...]).astype(BF16)
        o_ref[...] = x

    u = jnp.dot(xn_ref[...], wu_ref[...].astype(BF16), preferred_element_type=F32)
    a = jnp.square(jnp.maximum(u, 0.0)).astype(BF16)
    o_ref[...] += jnp.dot(a, wd_ref[...].astype(BF16), preferred_element_type=F32)

    if final_norm:
        @pl.when(j == pl.num_programs(1) - 1)
        def _():
            o_ref[...] = _rmsnorm_rows(o_ref[...], fw_ref[...])


def mlp_residual(h2, nw, w_up, w_down, fw, *, layer, tm, tf, final_norm):
    n, d = h2.shape
    dff = w_up.shape[2]
    return pl.pallas_call(
        functools.partial(_mlp_kernel, final_norm=final_norm),
        grid=(n // tm, dff // tf),
        in_specs=[pl.BlockSpec((tm, d), lambda i, j: (i, 0)),
                  pl.BlockSpec((1, d), lambda i, j: (0, 0)),
                  pl.BlockSpec((None, d, tf), lambda i, j: (layer, 0, j)),
                  pl.BlockSpec((None, tf, d), lambda i, j: (layer, j, 0)),
                  pl.BlockSpec((1, d), lambda i, j: (0, 0))],
        out_specs=pl.BlockSpec((tm, d), lambda i, j: (i, 0)),
        out_shape=jax.ShapeDtypeStruct((n, d), F32),
        scratch_shapes=[pltpu.VMEM((tm, d), BF16)],
        compiler_params=_cparams(2),
        name="mlp_final" if final_norm else "mlp",
    )(h2, nw.reshape(1, d), w_up, w_down, fw.reshape(1, d))


def _mlp_final_kernel(x_ref, nw_ref, wu_ref, wd_ref, fw_ref, o_ref, xn_ref):
    j = pl.program_id(2)

    @pl.when(j == 0)
    def _():
        x = x_ref[0]
        xn_ref[...] = _rmsnorm_rows(x, nw_ref[...]).astype(BF16)
        o_ref[0] = x

    u = jnp.dot(xn_ref[...], wu_ref[...].astype(BF16), preferred_element_type=F32)
    a = jnp.square(jnp.maximum(u, 0.0)).astype(BF16)
    o_ref[0] += jnp.dot(a, wd_ref[...].astype(BF16), preferred_element_type=F32)

    @pl.when(j == pl.num_programs(2) - 1)
    def _():
        o_ref[0] = _rmsnorm_rows(o_ref[0], fw_ref[...])


def mlp_residual_final(h, nw, w_up, w_down, fw, *, layer, seq, tm, tf):
    b, _, d = h.shape
    dff = w_up.shape[2]
    return pl.pallas_call(
        _mlp_final_kernel,
        grid=(b, seq // tm, dff // tf),
        in_specs=[pl.BlockSpec((pl.Element(1), pl.Element(tm), pl.Element(d)),
                               lambda bi, i, j: (bi, pl.multiple_of(N_META + i * tm, SUBLANES), 0)),
                  pl.BlockSpec((1, d), lambda bi, i, j: (0, 0)),
                  pl.BlockSpec((None, d, tf), lambda bi, i, j: (layer, 0, j)),
                  pl.BlockSpec((None, tf, d), lambda bi, i, j: (layer, j, 0)),
                  pl.BlockSpec((1, d), lambda bi, i, j: (0, 0))],
        out_specs=pl.BlockSpec((1, tm, d), lambda bi, i, j: (bi, i, 0)),
        out_shape=jax.ShapeDtypeStruct((b, seq, d), F32),
        scratch_shapes=[pltpu.VMEM((tm, d), BF16)],
        compiler_params=_cparams(3),
        name="mlp_final",
    )(h, nw.reshape(1, d), w_up, w_down, fw.reshape(1, d))


def _outproj_slab_kernel(h_ref, ya_ref, yb_ref, w_ref, o_ref):
    parts = [ya_ref[0, s] for s in range(ya_ref.shape[1])] + [yb_ref[0, s] for s in range(yb_ref.shape[1])]
    y = jnp.concatenate(parts, axis=-1).astype(BF16)
    o_ref[0] = h_ref[0] + jnp.dot(y, w_ref[...], preferred_element_type=F32)


def outproj_slab(h, ya, yb, w, *, tm):
    b, tp, d = h.shape
    sa, sb = ya.shape[1], yb.shape[1]
    return pl.pallas_call(
        _outproj_slab_kernel,
        grid=(b, tp // tm),
        in_specs=[pl.BlockSpec((1, tm, d), lambda bi, i: (bi, i, 0)),
                  pl.BlockSpec((1, sa, tm, LANES), lambda bi, i: (bi, 0, i, 0)),
                  pl.BlockSpec((1, sb, tm, LANES), lambda bi, i: (bi, 0, i, 0)),
                  pl.BlockSpec(w.shape, lambda bi, i: (0, 0))],
        out_specs=pl.BlockSpec((1, tm, d), lambda bi, i: (bi, i, 0)),
        out_shape=jax.ShapeDtypeStruct(h.shape, F32),
        compiler_params=_cparams(2),
        name="outproj_slab",
    )(h, ya, yb, w)


def _gated_outproj_kernel(h_ref, y_ref, z_ref, nw_ref, w_ref, o_ref, *, group):
    acc = h_ref[0]
    spg = group // LANES
    for g in range(y_ref.shape[2] // group):
        cols = slice(group * g, group * (g + 1))
        zg = jnp.concatenate([z_ref[0, spg * g + k] for k in range(spg)], axis=-1).astype(F32)
        gg = y_ref[0, :, cols].astype(F32) * _silu(zg)
        ms = jnp.mean(gg * gg, axis=-1, keepdims=True)
        part = (gg * lax.rsqrt(ms + NORM_EPS) * nw_ref[:, cols]).astype(BF16)
        acc = acc + jnp.dot(part, w_ref[cols, :], preferred_element_type=F32)
    o_ref[0] = acc


def gated_outproj(h, y, z, nw, w, *, tm, group):
    b, tp, d = h.shape
    k = y.shape[-1]
    return pl.pallas_call(
        functools.partial(_gated_outproj_kernel, group=group),
        grid=(b, tp // tm),
        in_specs=[pl.BlockSpec((1, tm, d), lambda bi, i: (bi, i, 0)),
                  pl.BlockSpec((1, tm, k), lambda bi, i: (bi, i, 0)),
                  pl.BlockSpec((1, k // LANES, tm, LANES), lambda bi, i: (bi, 0, i, 0)),
                  pl.BlockSpec((1, k), lambda bi, i: (0, 0)),
                  pl.BlockSpec(w.shape, lambda bi, i: (0, 0))],
        out_specs=pl.BlockSpec((1, tm, d), lambda bi, i: (bi, i, 0)),
        out_shape=jax.ShapeDtypeStruct(h.shape, F32),
        compiler_params=_cparams(2),
        name="gated_outproj",
    )(h, y, z, nw, w)


def _rglru_kernel(x_ref, g_ref, wa_ref, ba_ref, wx_ref, bx_ref, lam_ref, o_ref,
                  xcp, a_s, b_s, hn, carry, *, ls):
    n_slab = x_ref.shape[1]
    slabs = [slice(LANES * s, LANES * (s + 1)) for s in range(n_slab)]

    @pl.when(pl.program_id(1) == 0)
    def _():
        carry[...] = jnp.zeros(carry.shape, F32)

    def gather_body(tau, c):
        for s in range(n_slab):
            xcp[_rows(tau), slabs[s]] = x_ref[0, s, pl.ds(tau, SUBLANES, stride=ls), :]
        return c

    lax.fori_loop(0, ls, gather_body, 0, unroll=2)

    xc = xcp[...]
    xb = xc.astype(BF16)
    r = _sigmoid(jnp.dot(xb, wa_ref[...], preferred_element_type=F32) + ba_ref[...])
    i = _sigmoid(jnp.dot(xb, wx_ref[...], preferred_element_type=F32) + bx_ref[...])
    log_a = (-LRU_C) * r * _softplus(-lam_ref[...])
    a = jnp.exp(log_a)
    a_s[...] = a
    v = -jnp.tanh(log_a) * (a * a + 1.0)
    b_s[...] = jnp.where(v > 0.0, v * lax.rsqrt(v), 0.0) * (i * xc)

    def pass1(tau, c):
        out = []
        for s in range(n_slab):
            p, e = c[2 * s], c[2 * s + 1]
            av = a_s[_rows(tau), slabs[s]]
            out += [p * av, av * e + b_s[_rows(tau), slabs[s]]]
        return tuple(out)

    one = jnp.ones((SUBLANES, LANES), F32)
    zero = jnp.zeros((SUBLANES, LANES), F32)
    pe = lax.fori_loop(0, ls, pass1, (one, zero) * n_slab, unroll=2)

    h0 = []
    for s in range(n_slab):
        p, e = pe[2 * s], pe[2 * s + 1]
        c = carry[0:1, slabs[s]]
        rows = []
        for j in range(SUBLANES):
            rows.append(c)
            c = p[j:j + 1] * c + e[j:j + 1]
        carry[0:1, slabs[s]] = c
        h0.append(jnp.concatenate(rows, axis=0))

    def pass2(tau, hs):
        out = []
        for s in range(n_slab):
            h = a_s[_rows(tau), slabs[s]] * hs[s] + b_s[_rows(tau), slabs[s]]
            hn[s, pl.ds(tau, SUBLANES, stride=ls), :] = h
            out.append(h)
        return tuple(out)

    lax.fori_loop(0, ls, pass2, tuple(h0), unroll=2)
    for s in range(n_slab):
        o_ref[0, s] = hn[s] * _gelu_tanh(g_ref[0, s])


def rglru(proj, wa_bd, ba, wx_bd, bx, lam, *, ls):
    b, _, tp, _ = proj.shape
    w = lam.shape[0]
    ns = w // LANES
    tt = SUBLANES * ls
    const = lambda shape: pl.BlockSpec(shape, lambda bi, t: (0,) * len(shape))
    return pl.pallas_call(
        functools.partial(_rglru_kernel, ls=ls),
        grid=(b, tp // tt),
        in_specs=[pl.BlockSpec((1, ns, tt, LANES), lambda bi, t: (bi, 0, t, 0)),
                  pl.BlockSpec((1, ns, tt, LANES), lambda bi, t: (bi, 1, t, 0)),
                  const((w, w)), const((1, w)), const((w, w)), const((1, w)), const((1, w))],
        out_specs=pl.BlockSpec((1, ns, tt, LANES), lambda bi, t: (bi, 0, t, 0)),
        out_shape=jax.ShapeDtypeStruct((b, ns, tp, LANES), F32),
        scratch_shapes=[pltpu.VMEM((tt, w), F32), pltpu.VMEM((tt, w), F32), pltpu.VMEM((tt, w), F32),
                        pltpu.VMEM((ns, tt, LANES), F32), pltpu.VMEM((SUBLANES, w), F32)],
        compiler_params=_cparams(2),
        name="rglru",
    )(proj, proj, wa_bd, ba.reshape(1, w), wx_bd, bx.reshape(1, w), lam.reshape(1, w))


def _s5_kernel(u_ref, bend_ref, kc_ref, l8r_ref, l8i_ref, plr_ref, pli_ref, d_ref, wg_ref, bg_ref,
               o_ref, lp, st, yv, carry, *, ls):
    sc = S5_SC
    n_cs = l8r_ref.shape[1]
    stride = sc * ls
    cre = [slice(LANES * k, LANES * (k + 1)) for k in range(n_cs)]
    cim = [slice(LANES * (n_cs + k), LANES * (n_cs + k + 1)) for k in range(n_cs)]
    lanes = [slice(LANES * q, LANES * (q + 1)) for q in range(sc)]

    @pl.when(pl.program_id(2) == 0)
    def _():
        carry[...] = jnp.zeros(carry.shape, F32)

    def gather_body(tau, c):
        for sg in range(sc):
            lp[_rows(tau), lanes[sg]] = u_ref[0, 0, pl.ds(tau * sc + sg, SUBLANES, stride=stride), :]
        return c

    lax.fori_loop(0, ls, gather_body, 0)
    st[...] = jnp.dot(lp[...].astype(BF16), bend_ref[0], preferred_element_type=F32)

    lam = [(jnp.broadcast_to(l8r_ref[0, k], (SUBLANES, LANES)),
            jnp.broadcast_to(l8i_ref[0, k], (SUBLANES, LANES))) for k in range(n_cs)]

    def step(tau, k, sr, si):
        lr, li = lam[k]
        return (lr * sr - li * si + st[_rows(tau), cre[k]], lr * si + li * sr + st[_rows(tau), cim[k]])

    def pass1(tau, c):
        out = []
        for k in range(n_cs):
            out += list(step(tau, k, c[2 * k], c[2 * k + 1]))
        return tuple(out)

    zero = jnp.zeros((SUBLANES, LANES), F32)
    ends = lax.fori_loop(0, ls, pass1, (zero,) * (2 * n_cs))

    starts = []
    for k in range(n_cs):
        er, ei = ends[2 * k], ends[2 * k + 1]
        pr, pi = plr_ref[0, k], pli_ref[0, k]
        c_r, c_i = carry[0:1, cre[k]], carry[0:1, cim[k]]
        rows_r, rows_i = [], []
        for j in range(SUBLANES):
            rows_r.append(c_r)
            rows_i.append(c_i)
            c_r, c_i = (pr * c_r - pi * c_i + er[j:j + 1], pr * c_i + pi * c_r + ei[j:j + 1])
        carry[0:1, cre[k]] = c_r
        carry[0:1, cim[k]] = c_i
        starts += [jnp.concatenate(rows_r, axis=0), jnp.concatenate(rows_i, axis=0)]

    def pass2(tau, c):
        out = []
        for k in range(n_cs):
            nr, ni = step(tau, k, c[2 * k], c[2 * k + 1])
            st[_rows(tau), cre[k]] = c[2 * k]
            st[_rows(tau), cim[k]] = c[2 * k + 1]
            out += [nr, ni]
        return tuple(out)

    lax.fori_loop(0, ls, pass2, tuple(starts))

    lhs = jnp.concatenate([lp[...], st[...]], axis=1).astype(BF16)
    yv[...] = jnp.dot(lhs, kc_ref[0], preferred_element_type=F32)
    for q in range(sc):
        y = yv[:, lanes[q]] + d_ref[0] * lp[:, lanes[q]]
        y = _gelu_tanh(y)
        yv[:, lanes[q]] = y * _sigmoid(jnp.dot(y.astype(BF16), wg_ref[0], preferred_element_type=F32)
                                       + bg_ref[0])

    def scatter_body(tau, c):
        for sg in range(sc):
            o_ref[0, 0, pl.ds(tau * sc + sg, SUBLANES, stride=stride), :] = yv[_rows(tau), lanes[sg]]
        return c

    lax.fori_loop(0, ls, scatter_body, 0)


def s5(proj, first_slab, bend, kc, l8r, l8i, plr, pli, d, wg, bg, *, ls):
    b, _, tp, _ = proj.shape
    nblk, kin, nst = bend.shape
    n_cs = nst // (2 * LANES)
    tt = SUBLANES * ls * S5_SC
    nc = SUBLANES * ls
    per_blk = lambda shape: pl.BlockSpec((1,) + shape, lambda bi, gb, t: (gb,) + (0,) * len(shape))
    return pl.pallas_call(
        functools.partial(_s5_kernel, ls=ls),
        grid=(b, nblk, tp // tt),
        in_specs=[pl.BlockSpec((1, 1, tt, LANES), lambda bi, gb, t: (bi, first_slab + gb, t, 0)),
                  per_blk((kin, nst)), per_blk((kin + nst, kin)),
                  per_blk((n_cs, 1, LANES)), per_blk((n_cs, 1, LANES)),
                  per_blk((n_cs, 1, LANES)), per_blk((n_cs, 1, LANES)),
                  per_blk((1, LANES)), per_blk((LANES, LANES)), per_blk((1, LANES))],
        out_specs=pl.BlockSpec((1, 1, tt, LANES), lambda bi, gb, t: (bi, gb, t, 0)),
        out_shape=jax.ShapeDtypeStruct((b, nblk, tp, LANES), F32),
        scratch_shapes=[pltpu.VMEM((nc, kin), F32), pltpu.VMEM((nc, nst), F32),
                        pltpu.VMEM((nc, kin), F32), pltpu.VMEM((SUBLANES, nst), F32)],
        compiler_params=_cparams(3),
        name="s5",
    )(proj, bend, kc, l8r, l8i, plr, pli, d, wg, bg)


def _s5_params(a_re, a_im, b_re, b_im, c_re, c_im, d, log_dt, w_glu, b_glu, *, ls):
    g, p = a_re.shape
    sc = S5_SC
    gpb = LANES // S5_GROUP
    nblk = g // gpb
    dt = jnp.exp(log_dt)[:, None]
    mag = jnp.exp(a_re * dt)
    ar, ai = mag * jnp.cos(a_im * dt), mag * jnp.sin(a_im * dt)
    den = a_re * a_re + a_im * a_im
    fr = ((ar - 1.0) * a_re + ai * a_im) / den
    fi = (ai * a_re - (ar - 1.0) * a_im) / den
    bbar_re = fr[..., None] * b_re - fi[..., None] * b_im
    bbar_im = fr[..., None] * b_im + fi[..., None] * b_re

    def lam_pow(k):
        k = jnp.asarray(k, F32).reshape(-1, 1, 1)
        m = jnp.exp(a_re * dt * k)
        return m * jnp.cos(a_im * dt * k), m * jnp.sin(a_im * dt * k)

    pw_r, pw_i = lam_pow(jnp.arange(sc + 1))
    def group_diag(dense, rows_per_group):
        n = dense.shape[-1]
        tiled = jnp.tile(dense, (1,) * (dense.ndim - 1) + (gpb,))
        rg = lax.broadcasted_iota(jnp.int32, tiled.shape[-2:], 0) // rows_per_group
        cg = lax.broadcasted_iota(jnp.int32, tiled.shape[-2:], 1) // n
        return jnp.where(rg == cg, tiled, 0.0)

    def by_block(m):
        k, _, a, b2 = m.shape
        return jnp.transpose(m.reshape(k, nblk, gpb, a, b2), (0, 1, 2, 4, 3)).reshape(k, nblk, gpb * b2, a)

    wr = jnp.stack([pw_r[sc - 1 - s] for s in range(sc)])[..., None]
    wi = jnp.stack([pw_i[sc - 1 - s] for s in range(sc)])[..., None]
    e_re = wr * bbar_re - wi * bbar_im
    e_im = wr * bbar_im + wi * bbar_re
    rows_cat = lambda m: jnp.concatenate([m[k] for k in range(m.shape[0])], axis=-2)
    bend = jnp.concatenate([rows_cat(group_diag(by_block(e_re), S5_GROUP)),
                            rows_cat(group_diag(by_block(e_im), S5_GROUP))], axis=-1).astype(BF16)

    cl_re = c_re[None] * pw_r[:, :, None, :] - c_im[None] * pw_i[:, :, None, :]
    cl_im = c_re[None] * pw_i[:, :, None, :] + c_im[None] * pw_r[:, :, None, :]
    kl = (jnp.einsum('kgip,gpj->kgij', cl_re[:sc], bbar_re)
          - jnp.einsum('kgip,gpj->kgij', cl_im[:sc], bbar_im))
    kd = group_diag(by_block(kl), S5_GROUP)
    zero_blk = jnp.zeros_like(kd[0])
    kintra = jnp.concatenate(
        [jnp.concatenate([kd[t - s] if t >= s else zero_blk for t in range(sc)], axis=-1)
         for s in range(sc)], axis=-2)

    def out_bd(m):
        dense = jnp.transpose(m, (0, 1, 3, 2))
        d2 = group_diag(dense.reshape(sc, nblk, gpb * p, S5_GROUP), p)
        return jnp.concatenate([d2[t] for t in range(sc)], axis=-1)

    kc = jnp.concatenate([kintra, out_bd(cl_re[1:]), -out_bd(cl_im[1:])], axis=-2).astype(BF16)

    vec = lambda v: v.reshape(nblk, (gpb * p) // LANES, 1, LANES)
    pl_r, pl_i = lam_pow(jnp.asarray([sc * ls]))
    wg = group_diag(w_glu.reshape(nblk, LANES, S5_GROUP), S5_GROUP).astype(BF16)
    return (bend, kc, vec(pw_r[sc]), vec(pw_i[sc]), vec(pl_r[0]), vec(pl_i[0]),
            d.reshape(nblk, 1, LANES), wg, b_glu.reshape(nblk, 1, LANES))


def _ssd_prep_kernel(dt_ref, dtb_ref, alog_ref, cs_ref, rowp_ref, wend_ref, cdec_ref):
    L = SSD_CHUNK
    row = lax.broadcasted_iota(jnp.int32, (L, L), 0)
    col = lax.broadcasted_iota(jnp.int32, (L, L), 1)
    tri = (row >= col).astype(BF16)
    a = -jnp.exp(alog_ref[...])
    for c in range(cs_ref.shape[1]):
        dt = _softplus(dt_ref[0, 0, L * c:L * (c + 1), :] + dtb_ref[...])
        da = dt * a
        da_hi = da.astype(BF16)
        r1 = da - da_hi.astype(F32)
        da_mid = r1.astype(BF16)
        da_lo = (r1 - da_mid.astype(F32)).astype(BF16)
        cs = (jnp.dot(tri, da_hi, preferred_element_type=F32)
              + jnp.dot(tri, da_mid, preferred_element_type=F32)
              + jnp.dot(tri, da_lo, preferred_element_type=F32))
        dt_t = dt.T
        cs_t = cs.T
        last_t = jnp.broadcast_to(cs_t[:, L - 1:L], (L, L))
        cs_ref[0, c] = cs * LOG2_E
        rowp_ref[0, c] = (cs_t - jnp.log(dt_t)) * LOG2_E
        wend_ref[0, c] = dt_t * jnp.exp(last_t - cs_t)
        cdec_ref[0, c] = jnp.exp(last_t)


def ssd_prep(dt, dtb, alog, *, chunks_per_step):
    b, _, tp, _ = dt.shape
    L = SSD_CHUNK
    cp = chunks_per_step
    nc = tp // L
    const = lambda shape: pl.BlockSpec(shape, lambda bi, c: (0,) * len(shape))
    tab = pl.BlockSpec((1, cp, L, LANES), lambda bi, c: (bi, c, 0, 0))
    return pl.pallas_call(
        _ssd_prep_kernel,
        grid=(b, nc // cp),
        in_specs=[pl.BlockSpec((1, 1, cp * L, LANES), lambda bi, c: (bi, 0, c, 0)),
                  const((1, LANES)), const((1, LANES))],
        out_specs=[tab] * 4,
        out_shape=[jax.ShapeDtypeStruct((b, nc, L, LANES), F32)] * 4,
        compiler_params=_cparams(2),
        name="ssd_prep",
    )(dt, dtb, alog)


def _ssd_kernel(xbc_ref, cs_ref, rowp_ref, wend_ref, cdec_ref, dskip_ref, o_ref, hst):
    @pl.when(pl.program_id(1) == 0)
    def _():
        hst[...] = jnp.zeros(hst.shape, F32)

    for c in range(xbc_ref.shape[2] // SSD_CHUNK):
        _ssd_chunk(xbc_ref, cs_ref, rowp_ref, wend_ref, cdec_ref, dskip_ref, o_ref, hst, c)


def _ssd_chunk(xbc_ref, cs_ref, rowp_ref, wend_ref, cdec_ref, dskip_ref, o_ref, hst, c):
    L = SSD_CHUNK
    rs = slice(L * c, L * (c + 1))
    n_groups = hst.shape[0]
    gw = SSD_HPG * SSD_HEAD_DIM
    spg = gw // LANES
    n_xs = n_groups * spg
    row = lax.broadcasted_iota(jnp.int32, (L, L), 0)
    col = lax.broadcasted_iota(jnp.int32, (L, L), 1)
    causal = row >= col
    left_half = col < SSD_HEAD_DIM
    head_of_lane = lax.broadcasted_iota(jnp.int32, (L, gw), 1) // SSD_HEAD_DIM

    def head_rows(ref, g):
        return jnp.concatenate(
            [jnp.broadcast_to(ref[0, c, SSD_HPG * g + r:SSD_HPG * g + r + 1, :], (SSD_HEAD_DIM, L))
             for r in range(SSD_HPG)], axis=0)

    def x_slabs(g):
        return jnp.concatenate([xbc_ref[0, spg * g + k, rs, :] for k in range(spg)], axis=-1)

    scores, y_offs = [], []
    for g in range(n_groups):
        bm = xbc_ref[0, n_xs + g, rs, :]
        cm = xbc_ref[0, n_xs + n_groups + g, rs, :]
        scores.append(lax.dot_general(cm, bm, (((1,), (1,)), ((), ())), preferred_element_type=F32))
        h_prev = hst[g]
        y_offs.append(lax.dot_general(cm, h_prev.astype(BF16), (((1,), (1,)), ((), ())),
                                      preferred_element_type=F32))
        xw_t = (x_slabs(g).astype(F32).T * head_rows(wend_ref, g)).astype(BF16)
        hst[g] = head_rows(cdec_ref, g) * h_prev + jnp.dot(xw_t, bm, preferred_element_type=F32)

    for g in range(n_groups):
        xsb = x_slabs(g)
        ms, bcs = [], []
        for r in range(SSD_HPG):
            h = SSD_HPG * g + r
            bcs.append(jnp.broadcast_to(cs_ref[0, c, :, h:h + 1], (L, L)))
            ms.append((scores[g] * jnp.exp2(jnp.where(causal, bcs[r] - rowp_ref[0, c, h:h + 1, :], -jnp.inf))
                       ).astype(BF16))
        x_bd = jnp.concatenate([jnp.where(head_of_lane == r, xsb, jnp.zeros_like(xsb))
                                for r in range(SSD_HPG)], axis=0)
        y_diag = jnp.dot(jnp.concatenate(ms, axis=1), x_bd, preferred_element_type=F32)
        f_start = jnp.concatenate([jnp.exp2(jnp.where(left_half, bcs[2 * k], bcs[2 * k + 1]))
                                   for k in range(spg)], axis=-1)
        o_ref[0, rs, gw * g:gw * (g + 1)] = (
            y_diag + y_offs[g] * f_start + dskip_ref[:, gw * g:gw * (g + 1)] * xsb.astype(F32)
        ).astype(o_ref.dtype)


def ssd_core(xbc, tables, dskip):
    b, n_conv, tp, _ = xbc.shape
    d_inner = dskip.shape[1]
    n_groups = d_inner // (SSD_HPG * SSD_HEAD_DIM)
    cps = SSD_CHUNKS_PER_STEP
    L = SSD_CHUNK * cps
    const = lambda shape: pl.BlockSpec(shape, lambda bi, c: (0,) * len(shape))
    tab = pl.BlockSpec((1, cps, SSD_CHUNK, LANES), lambda bi, c: (bi, c, 0, 0))
    return pl.pallas_call(
        _ssd_kernel,
        grid=(b, tp // L),
        in_specs=[pl.BlockSpec((1, n_conv, L, LANES), lambda bi, c: (bi, 0, c, 0)),
                  tab, tab, tab, tab, const((1, d_inner))],
        out_specs=pl.BlockSpec((1, L, d_inner), lambda bi, c: (bi, c, 0)),
        out_shape=jax.ShapeDtypeStruct((b, tp, d_inner), BF16),
        scratch_shapes=[pltpu.VMEM((n_groups, SSD_HPG * SSD_HEAD_DIM, SSD_STATE), F32)],
        compiler_params=_cparams(2),
        name="ssd_core",
    )(xbc, *tables, dskip)


def _block_diag(w):
    n, di, do = w.shape
    return jnp.einsum('gij,gh->gihj', w, jnp.eye(n, dtype=w.dtype)).reshape(n * di, n * do)


def _pad_lanes(v, fill=0.0):
    return jnp.pad(v, (0, LANES - v.shape[0]), constant_values=fill).reshape(1, LANES)


def kernel(x, meta_tokens, norm_mix, norm_mlp, norm_final, ev_w_in, lru_conv_w, lru_conv_b, lru_w_a, lru_b_a, lru_w_x, lru_b_x, lru_lambda, s5_a_re, s5_a_im, s5_b_re, s5_b_im, s5_c_re, s5_c_im, s5_d, s5_log_dt, s5_w_glu, s5_b_glu, ev_w_out, ssd_w_in, ssd_conv_w, ssd_conv_b, ssd_dt_bias, ssd_a_log, ssd_d, ssd_norm, ssd_w_out, mlp_w_up, mlp_w_down):
    bsz, seq, d = x.shape
    t = seq + N_META
    unit = math.lcm(SUBLANES * LRU_LS, SUBLANES * S5_LS * S5_SC, ROW_TILE, SSD_CHUNK)
    tp = -(-t // unit) * unit
    meta = jnp.broadcast_to(meta_tokens[None].astype(x.dtype), (bsz, N_META, d))
    h = jnp.concatenate([meta, x, jnp.zeros((bsz, tp - t, d), x.dtype)], axis=1)

    lru_w = lru_conv_w.shape[-1]
    proj = norm_matmul(h, norm_mix[0], ev_w_in[0].astype(BF16), lru_conv_w[0], lru_conv_b[0], tm=ROW_TILE)
    y_lru = rglru(proj, _block_diag(lru_w_a[0]).astype(BF16), lru_b_a[0],
                  _block_diag(lru_w_x[0]).astype(BF16), lru_b_x[0], lru_lambda[0], ls=LRU_LS)
    y_s5 = s5(proj, 2 * lru_w // LANES,
              *_s5_params(s5_a_re[0], s5_a_im[0], s5_b_re[0], s5_b_im[0], s5_c_re[0], s5_c_im[0],
                          s5_d[0], s5_log_dt[0], s5_w_glu[0], s5_b_glu[0], ls=S5_LS), ls=S5_LS)
    h = outproj_slab(h, y_lru, y_s5, ev_w_out[0].astype(BF16), tm=ROW_TILE)
    h = mlp_residual(h.reshape(bsz * tp, d), norm_mlp[0], mlp_w_up, mlp_w_down,
                     norm_final, layer=0, tm=ROW_TILE, tf=512,
                     final_norm=False).reshape(bsz, tp, d)

    d_inner = ssd_w_out.shape[1]
    conv_dim = ssd_conv_w.shape[-1]
    n_heads = ssd_dt_bias.shape[-1]
    w_in = ssd_w_in[0]
    w_zdt = jnp.concatenate([w_in[:, :d_inner], w_in[:, d_inner + conv_dim:],
                             jnp.zeros((d, LANES - n_heads), F32)], axis=1).astype(BF16)
    xbc = inproj_conv(h, norm_mix[1], w_in[:, d_inner:d_inner + conv_dim].astype(BF16),
                      ssd_conv_w[0], ssd_conv_b[0], tm=ROW_TILE, tn=1024)
    z, dt = inproj_zdt(h, norm_mix[1], w_zdt, tm=ROW_TILE)
    tables = ssd_prep(dt, _pad_lanes(ssd_dt_bias[0]), _pad_lanes(ssd_a_log[0]), chunks_per_step=SSD_PREP_CHUNKS)
    y = ssd_core(xbc, tables, jnp.repeat(ssd_d[0], SSD_HEAD_DIM).reshape(1, d_inner))
    h = gated_outproj(h, y, z, ssd_norm[0].reshape(1, d_inner), ssd_w_out[0].astype(BF16),
                      tm=ROW_TILE // 2, group=SSD_HPG * SSD_HEAD_DIM)
    return mlp_residual_final(h, norm_mlp[1], mlp_w_up, mlp_w_down,
                              norm_final, layer=1, seq=seq, tm=FINAL_ROW_TILE, tf=512)
```

```python
import functools
import math

import jax
import jax.numpy as jnp
from jax import lax
from jax.experimental import pallas as pl
from jax.experimental.pallas import tpu as pltpu

F32 = jnp.float32
BF16 = jnp.bfloat16

LANES = 128
SUBLANES = 8
NORM_EPS = 1e-5
LOG2_E = 1.4426950408889634
N_META = 16
CONV_WIDTH = 4
LRU_C = 8.0
S5_GROUP = 16
S5_STATE = 64
SSD_HEAD_DIM = 64
SSD_STATE = 128
SSD_CHUNK = 128
SSD_HPG = 4
SSD_CHUNKS_PER_STEP = 5

LRU_LS = 130
S5_SC = SUBLANES
S5_LS = 65
ROW_TILE = 1280
FINAL_ROW_TILE = 2048
VMEM_LIMIT = 48 * 1024 * 1024
VMEM_LIMIT_LARGE = 58 * 1024 * 1024


def _cparams(n_axes, vmem=VMEM_LIMIT):
    return pltpu.CompilerParams(dimension_semantics=("arbitrary",) * n_axes,
                                vmem_limit_bytes=vmem)


def _sigmoid(x):
    return 0.5 + 0.5 * jnp.tanh(0.5 * x)


def _silu(x):
    hx = 0.5 * x
    return hx + hx * jnp.tanh(hx)


def _gelu_tanh(x):
    return 0.5 * x * (1.0 + jnp.tanh(math.sqrt(2.0 / math.pi) * (x + 0.044715 * (x * x * x))))


def _log1p(e):
    u = 1.0 + e
    return jnp.where(u == 1.0, e, jnp.log(u) * (e / (u - 1.0)))


def _softplus(x):
    return jnp.maximum(x, 0.0) + _log1p(jnp.exp(-jnp.abs(x)))


def _rmsnorm_rows(x, w):
    ms = jnp.mean(x * x, axis=-1, keepdims=True)
    return x * lax.rsqrt(ms + NORM_EPS) * w


def _rows(tau):
    return pl.ds(pl.multiple_of(tau * SUBLANES, SUBLANES), SUBLANES)


def _norm_matmul_kernel(x_ref, nw_ref, w_ref, cw_ref, cb_ref, o_ref, rbuf, hist):
    tm = x_ref.shape[1]
    n_conv = rbuf.shape[0]
    hrows = SUBLANES

    @pl.when(pl.program_id(1) == 0)
    def _():
        hist[...] = jnp.zeros(hist.shape, F32)

    x = x_ref[0]
    rs = lax.rsqrt(jnp.mean(x * x, axis=-1, keepdims=True) + NORM_EPS)
    xb = (x * nw_ref[...]).astype(BF16)
    step = 4
    for c in range(o_ref.shape[1] // step):
        r = jnp.dot(xb, w_ref[:, LANES * step * c:LANES * step * (c + 1)], preferred_element_type=F32) * rs
        for k in range(step):
            s = step * c + k
            if s >= n_conv:
                o_ref[0, s] = r[:, LANES * k:LANES * (k + 1)]
                continue
            sl = slice(LANES * s, LANES * (s + 1))
            rbuf[s, 0:hrows, :] = hist[s]
            rbuf[s, hrows:hrows + tm, :] = r[:, LANES * k:LANES * (k + 1)]
            hist[s] = rbuf[s, tm:tm + hrows, :]
            taps = [cw_ref[m, :, sl] for m in range(CONV_WIDTH)]
            bias = cb_ref[:, sl]
            for g in range(tm // SUBLANES):
                r0 = hrows + SUBLANES * g
                acc = bias
                for m in range(CONV_WIDTH):
                    acc = acc + taps[m] * rbuf[s, r0 - m:r0 - m + SUBLANES, :]
                o_ref[0, s, SUBLANES * g:SUBLANES * (g + 1), :] = acc


def norm_matmul(h, nw, w, cw, cb, *, tm):
    b, tp, d = h.shape
    n = w.shape[1]
    wc = cw.shape[1]
    cw8 = jnp.broadcast_to(jnp.stack([cw[CONV_WIDTH - 1 - m] for m in range(CONV_WIDTH)])[:, None, :],
                           (CONV_WIDTH, SUBLANES, wc))
    cb8 = jnp.broadcast_to(cb[None, :], (SUBLANES, wc))
    const = lambda shape: pl.BlockSpec(shape, lambda bi, i: (0,) * len(shape))
    return pl.pallas_call(
        _norm_matmul_kernel,
        grid=(b, tp // tm),
        in_specs=[pl.BlockSpec((1, tm, d), lambda bi, i: (bi, i, 0)),
                  const((1, d)), const((d, n)), const((CONV_WIDTH, SUBLANES, wc)), const((SUBLANES, wc))],
        out_specs=pl.BlockSpec((1, n // LANES, tm, LANES), lambda bi, i: (bi, 0, i, 0)),
        out_shape=jax.ShapeDtypeStruct((b, n // LANES, tp, LANES), F32),
        scratch_shapes=[pltpu.VMEM((wc // LANES, SUBLANES + tm, LANES), F32),
                        pltpu.VMEM((wc // LANES, SUBLANES, LANES), F32)],
        compiler_params=_cparams(2),
        name="norm_matmul",
    )(h, nw.reshape(1, d), w, cw8, cb8)


def _inproj_conv_kernel(x_ref, nw_ref, w_ref, cw_ref, cb_ref, o_ref, xn_ref, rs, hist):
    i, j = pl.program_id(1), pl.program_id(2)
    tm = x_ref.shape[1]
    n_slab = o_ref.shape[1]
    hrows = SUBLANES
    pack = 2 * SUBLANES

    @pl.when(j == 0)
    def _():
        xn_ref[...] = _rmsnorm_rows(x_ref[0], nw_ref[...]).astype(BF16)

    @pl.when(i == 0)
    def _():
        hist[j] = jnp.zeros(hist.shape[1:], F32)

    xn = xn_ref[...]
    proj = lambda c: jnp.dot(xn, w_ref[:, 2 * LANES * c:2 * LANES * (c + 1)], preferred_element_type=F32)
    r_next = proj(0)
    for c in range(n_slab // 2):
        r = r_next
        if c + 1 < n_slab // 2:
            r_next = proj(c + 1)
        for k in range(2):
            s = 2 * c + k
            sl = slice(LANES * s, LANES * (s + 1))
            rs[s, 0:hrows, :] = hist[j, s]
            rs[s, hrows:hrows + tm, :] = r[:, LANES * k:LANES * (k + 1)]
            hist[j, s] = rs[s, tm:tm + hrows, :]
            taps = [cw_ref[m, :, sl] for m in range(CONV_WIDTH)]
            bias = cb_ref[:, sl]
            for g in range(tm // pack):
                halves = []
                for hh in range(pack // SUBLANES):
                    r0 = hrows + pack * g + SUBLANES * hh
                    acc = bias
                    for m in range(CONV_WIDTH):
                        acc = acc + taps[m] * rs[s, r0 - m:r0 - m + SUBLANES, :]
                    halves.append(acc)
                o_ref[0, s, pack * g:pack * (g + 1), :] = _silu(jnp.concatenate(halves, axis=0).astype(BF16))


def inproj_conv(h, nw, w, cw, cb, *, tm, tn):
    b, tp, d = h.shape
    n = w.shape[1]
    ns = tn // LANES
    cw8 = jnp.broadcast_to(jnp.stack([cw[CONV_WIDTH - 1 - m] for m in range(CONV_WIDTH)])[:, None, :],
                           (CONV_WIDTH, SUBLANES, n))
    cb8 = jnp.broadcast_to(cb.reshape(1, n), (SUBLANES, n))
    return pl.pallas_call(
        _inproj_conv_kernel,
        grid=(b, tp // tm, n // tn),
        in_specs=[pl.BlockSpec((1, tm, d), lambda bi, i, j: (bi, i, 0)),
                  pl.BlockSpec((1, d), lambda bi, i, j: (0, 0)),
                  pl.BlockSpec((d, tn), lambda bi, i, j: (0, j)),
                  pl.BlockSpec((CONV_WIDTH, SUBLANES, tn), lambda bi, i, j: (0, 0, j)),
                  pl.BlockSpec((SUBLANES, tn), lambda bi, i, j: (0, j))],
        out_specs=pl.BlockSpec((1, ns, tm, LANES), lambda bi, i, j: (bi, j, i, 0)),
        out_shape=jax.ShapeDtypeStruct((b, n // LANES, tp, LANES), BF16),
        scratch_shapes=[pltpu.VMEM((tm, d), BF16),
                        pltpu.VMEM((ns, SUBLANES + tm, LANES), F32),
                        pltpu.VMEM((n // tn, ns, SUBLANES, LANES), F32)],
        compiler_params=_cparams(3),
        name="inproj_conv",
    )(h, nw.reshape(1, d), w, cw8, cb8)


def _ssd_tables(dt_raw, dtb, a, tri):
    L = SSD_CHUNK
    dt = _softplus(dt_raw + dtb)
    da = dt * a
    da_hi = da.astype(BF16)
    r1 = da - da_hi.astype(F32)
    da_mid = r1.astype(BF16)
    da_lo = (r1 - da_mid.astype(F32)).astype(BF16)
    cs = (jnp.dot(tri, da_hi, preferred_element_type=F32)
          + jnp.dot(tri, da_mid, preferred_element_type=F32)
          + jnp.dot(tri, da_lo, preferred_element_type=F32))
    dt_t = dt.T
    cs_t = cs.T
    last_t = jnp.broadcast_to(cs_t[:, L - 1:L], (L, L))
    return cs * LOG2_E, (cs_t - jnp.log(dt_t)) * LOG2_E, dt_t * jnp.exp(last_t - cs_t), jnp.exp(last_t)


def _inproj_zdt_kernel(x_ref, nw_ref, w_ref, dtb_ref, alog_ref, z_ref, cs_ref, rowp_ref, wend_ref, cdec_ref):
    x = x_ref[0]
    rs = lax.rsqrt(jnp.mean(x * x, axis=-1, keepdims=True) + NORM_EPS)
    xb = (x * nw_ref[...]).astype(BF16)
    n_z = z_ref.shape[1]
    dt_raw = jnp.dot(xb, w_ref[:, LANES * n_z:LANES * (n_z + 1)], preferred_element_type=F32) * rs
    L = SSD_CHUNK
    tri = (lax.broadcasted_iota(jnp.int32, (L, L), 0) >= lax.broadcasted_iota(jnp.int32, (L, L), 1)).astype(BF16)
    a = -jnp.exp(alog_ref[...])
    step = 4
    n_chunks = cs_ref.shape[1]
    for c in range(n_z // step):
        r = jnp.dot(xb, w_ref[:, LANES * step * c:LANES * step * (c + 1)], preferred_element_type=F32) * rs
        for k in range(step):
            z_ref[0, step * c + k] = r[:, LANES * k:LANES * (k + 1)].astype(BF16)
        for q in range(c * n_chunks // (n_z // step), (c + 1) * n_chunks // (n_z // step)):
            cs_ref[0, q], rowp_ref[0, q], wend_ref[0, q], cdec_ref[0, q] = _ssd_tables(
                dt_raw[L * q:L * (q + 1)], dtb_ref[...], a, tri)


def inproj_zdt(h, nw, w, dtb, alog, *, tm):
    b, tp, d = h.shape
    n_z = w.shape[1] // LANES - 1
    L = SSD_CHUNK
    const = lambda shape: pl.BlockSpec(shape, lambda bi, i: (0,) * len(shape))
    tab = pl.BlockSpec((1, tm // L, L, LANES), lambda bi, i: (bi, i, 0, 0))
    return pl.pallas_call(
        _inproj_zdt_kernel,
        grid=(b, tp // tm),
        in_specs=[pl.BlockSpec((1, tm, d), lambda bi, i: (bi, i, 0)),
                  const((1, d)), const(w.shape), const((1, LANES)), const((1, LANES))],
        out_specs=[pl.BlockSpec((1, n_z, tm, LANES), lambda bi, i: (bi, 0, i, 0)), tab, tab, tab, tab],
        out_shape=[jax.ShapeDtypeStruct((b, n_z, tp, LANES), BF16)]
        + [jax.ShapeDtypeStruct((b, tp // L, L, LANES), F32)] * 4,
        compiler_params=_cparams(2),
        name="inproj_zdt",
    )(h, nw.reshape(1, d), w, dtb, alog)


def _mlp_kernel(x_ref, nw_ref, wu_ref, wd_ref, fw_ref, o_ref, xn_ref, *, final_norm):
    j = pl.program_id(1)

    @pl.when(j == 0)
    def _():
        x = x_ref[...]
        xn_ref[...] = _rmsnorm_rows(x, nw_ref[...]).astype(BF16)
        o_ref[...] = x

    u = jnp.dot(xn_ref[...], wu_ref[...].astype(BF16), preferred_element_type=F32)
    a = jnp.square(jnp.maximum(u, 0.0)).astype(BF16)
    o_ref[...] += jnp.dot(a, wd_ref[...].astype(BF16), preferred_element_type=F32)

    if final_norm:
        @pl.when(j == pl.num_programs(1) - 1)
        def _():
            o_ref[...] = _rmsnorm_rows(o_ref[...], fw_ref[...])


def mlp_residual(h2, nw, w_up, w_down, fw, *, layer, tm, tf, final_norm):
    n, d = h2.shape
    dff = w_up.shape[2]
    return pl.pallas_call(
        functools.partial(_mlp_kernel, final_norm=final_norm),
        grid=(n // tm, dff // tf),
        in_specs=[pl.BlockSpec((tm, d), lambda i, j: (i, 0)),
                  pl.BlockSpec((1, d), lambda i, j: (0, 0)),
                  pl.BlockSpec((None, d, tf), lambda i, j: (layer, 0, j)),
                  pl.BlockSpec((None, tf, d), lambda i, j: (layer, j, 0)),
                  pl.BlockSpec((1, d), lambda i, j: (0, 0))],
        out_specs=pl.BlockSpec((tm, d), lambda i, j: (i, 0)),
        out_shape=jax.ShapeDtypeStruct((n, d), F32),
        scratch_shapes=[pltpu.VMEM((tm, d), BF16)],
        compiler_params=_cparams(2),
        name="mlp_final" if final_norm else "mlp",
    )(h2, nw.reshape(1, d), w_up, w_down, fw.reshape(1, d))


def _mlp_final_kernel(x_ref, nw_ref, wu_ref, wd_ref, fw_ref, o_ref, xn_ref):
    j = pl.program_id(2)

    @pl.when(j == 0)
    def _():
        x = x_ref[0]
        xn_ref[...] = _rmsnorm_rows(x, nw_ref[...]).astype(BF16)
        o_ref[0] = x

    u = jnp.dot(xn_ref[...], wu_ref[...].astype(BF16), preferred_element_type=F32)
    a = jnp.square(jnp.maximum(u, 0.0)).astype(BF16)
    o_ref[0] += jnp.dot(a, wd_ref[...].astype(BF16), preferred_element_type=F32)

    @pl.when(j == pl.num_programs(2) - 1)
    def _():
        o_ref[0] = _rmsnorm_rows(o_ref[0], fw_ref[...])


def mlp_residual_final(h, nw, w_up, w_down, fw, *, layer, seq, tm, tf):
    b, _, d = h.shape
    dff = w_up.shape[2]
    return pl.pallas_call(
        _mlp_final_kernel,
        grid=(b, seq // tm, dff // tf),
        in_specs=[pl.BlockSpec((pl.Element(1), pl.Element(tm), pl.Element(d)),
                               lambda bi, i, j: (bi, pl.multiple_of(N_META + i * tm, SUBLANES), 0)),
                  pl.BlockSpec((1, d), lambda bi, i, j: (0, 0)),
                  pl.BlockSpec((None, d, tf), lambda bi, i, j: (layer, 0, j)),
                  pl.BlockSpec((None, tf, d), lambda bi, i, j: (layer, j, 0)),
                  pl.BlockSpec((1, d), lambda bi, i, j: (0, 0))],
        out_specs=pl.BlockSpec((1, tm, d), lambda bi, i, j: (bi, i, 0)),
        out_shape=jax.ShapeDtypeStruct((b, seq, d), F32),
        scratch_shapes=[pltpu.VMEM((tm, d), BF16)],
        compiler_params=_cparams(3, vmem=VMEM_LIMIT_LARGE),
        name="mlp_final",
    )(h, nw.reshape(1, d), w_up, w_down, fw.reshape(1, d))


def _outproj_slab_kernel(h_ref, ya_ref, yb_ref, w_ref, o_ref):
    parts = [ya_ref[0, s] for s in range(ya_ref.shape[1])] + [yb_ref[0, s] for s in range(yb_ref.shape[1])]
    y = jnp.concatenate(parts, axis=-1).astype(BF16)
    o_ref[0] = h_ref[0] + jnp.dot(y, w_ref[...], preferred_element_type=F32)


def outproj_slab(h, ya, yb, w, *, tm):
    b, tp, d = h.shape
    sa, sb = ya.shape[1], yb.shape[1]
    return pl.pallas_call(
        _outproj_slab_kernel,
        grid=(b, tp // tm),
        in_specs=[pl.BlockSpec((1, tm, d), lambda bi, i: (bi, i, 0)),
                  pl.BlockSpec((1, sa, tm, LANES), lambda bi, i: (bi, 0, i, 0)),
                  pl.BlockSpec((1, sb, tm, LANES), lambda bi, i: (bi, 0, i, 0)),
                  pl.BlockSpec(w.shape, lambda bi, i: (0, 0))],
        out_specs=pl.BlockSpec((1, tm, d), lambda bi, i: (bi, i, 0)),
        out_shape=jax.ShapeDtypeStruct(h.shape, F32),
        compiler_params=_cparams(2),
        name="outproj_slab",
    )(h, ya, yb, w)


def _gated_outproj_kernel(h_ref, y_ref, z_ref, nw_ref, w_ref, o_ref, *, group):
    acc = h_ref[0]
    spg = group // LANES
    for g in range(y_ref.shape[2] // group):
        cols = slice(group * g, group * (g + 1))
        zg = jnp.concatenate([z_ref[0, spg * g + k] for k in range(spg)], axis=-1).astype(F32)
        gg = y_ref[0, :, cols].astype(F32) * _silu(zg)
        ms = jnp.mean(gg * gg, axis=-1, keepdims=True)
        part = (gg * lax.rsqrt(ms + NORM_EPS) * nw_ref[:, cols]).astype(BF16)
        acc = acc + jnp.dot(part, w_ref[cols, :], preferred_element_type=F32)
    o_ref[0] = acc


def gated_outproj(h, y, z, nw, w, *, tm, group):
    b, tp, d = h.shape
    k = y.shape[-1]
    return pl.pallas_call(
        functools.partial(_gated_outproj_kernel, group=group),
        grid=(b, tp // tm),
        in_specs=[pl.BlockSpec((1, tm, d), lambda bi, i: (bi, i, 0)),
                  pl.BlockSpec((1, tm, k), lambda bi, i: (bi, i, 0)),
                  pl.BlockSpec((1, k // LANES, tm, LANES), lambda bi, i: (bi, 0, i, 0)),
                  pl.BlockSpec((1, k), lambda bi, i: (0, 0)),
                  pl.BlockSpec(w.shape, lambda bi, i: (0, 0))],
        out_specs=pl.BlockSpec((1, tm, d), lambda bi, i: (bi, i, 0)),
        out_shape=jax.ShapeDtypeStruct(h.shape, F32),
        compiler_params=_cparams(2),
        name="gated_outproj",
    )(h, y, z, nw, w)


def _rglru_kernel(x_ref, g_ref, wa_ref, ba_ref, wx_ref, bx_ref, lam_ref, o_ref,
                  xcp, a_s, b_s, hn, carry, *, ls):
    n_slab = x_ref.shape[1]
    slabs = [slice(LANES * s, LANES * (s + 1)) for s in range(n_slab)]

    @pl.when(pl.program_id(1) == 0)
    def _():
        carry[...] = jnp.zeros(carry.shape, F32)

    def gather_body(tau, c):
        for s in range(n_slab):
            xcp[_rows(tau), slabs[s]] = x_ref[0, s, pl.ds(tau, SUBLANES, stride=ls), :]
        return c

    lax.fori_loop(0, ls, gather_body, 0, unroll=2)

    xc = xcp[...]
    xb = xc.astype(BF16)
    r = _sigmoid(jnp.dot(xb, wa_ref[...], preferred_element_type=F32) + ba_ref[...])
    i = _sigmoid(jnp.dot(xb, wx_ref[...], preferred_element_type=F32) + bx_ref[...])
    log_a = (-LRU_C) * r * _softplus(-lam_ref[...])
    a = jnp.exp(log_a)
    a_s[...] = a
    v = -jnp.tanh(log_a) * (a * a + 1.0)
    b_s[...] = jnp.where(v > 0.0, v * lax.rsqrt(v), 0.0) * (i * xc)

    def pass1(tau, c):
        out = []
        for s in range(n_slab):
            p, e = c[2 * s], c[2 * s + 1]
            av = a_s[_rows(tau), slabs[s]]
            out += [p * av, av * e + b_s[_rows(tau), slabs[s]]]
        return tuple(out)

    one = jnp.ones((SUBLANES, LANES), F32)
    zero = jnp.zeros((SUBLANES, LANES), F32)
    pe = lax.fori_loop(0, ls, pass1, (one, zero) * n_slab, unroll=2)

    h0 = []
    for s in range(n_slab):
        p, e = pe[2 * s], pe[2 * s + 1]
        c = carry[0:1, slabs[s]]
        rows = []
        for j in range(SUBLANES):
            rows.append(c)
            c = p[j:j + 1] * c + e[j:j + 1]
        carry[0:1, slabs[s]] = c
        h0.append(jnp.concatenate(rows, axis=0))

    def pass2(tau, hs):
        out = []
        for s in range(n_slab):
            h = a_s[_rows(tau), slabs[s]] * hs[s] + b_s[_rows(tau), slabs[s]]
            hn[s, pl.ds(tau, SUBLANES, stride=ls), :] = h
            out.append(h)
        return tuple(out)

    lax.fori_loop(0, ls, pass2, tuple(h0), unroll=2)
    for s in range(n_slab):
        o_ref[0, s] = hn[s] * _gelu_tanh(g_ref[0, s])


def rglru(proj, wa_bd, ba, wx_bd, bx, lam, *, ls):
    b, _, tp, _ = proj.shape
    w = lam.shape[0]
    ns = w // LANES
    tt = SUBLANES * ls
    const = lambda shape: pl.BlockSpec(shape, lambda bi, t: (0,) * len(shape))
    return pl.pallas_call(
        functools.partial(_rglru_kernel, ls=ls),
        grid=(b, tp // tt),
        in_specs=[pl.BlockSpec((1, ns, tt, LANES), lambda bi, t: (bi, 0, t, 0)),
                  pl.BlockSpec((1, ns, tt, LANES), lambda bi, t: (bi, 1, t, 0)),
                  const((w, w)), const((1, w)), const((w, w)), const((1, w)), const((1, w))],
        out_specs=pl.BlockSpec((1, ns, tt, LANES), lambda bi, t: (bi, 0, t, 0)),
        out_shape=jax.ShapeDtypeStruct((b, ns, tp, LANES), F32),
        scratch_shapes=[pltpu.VMEM((tt, w), F32), pltpu.VMEM((tt, w), F32), pltpu.VMEM((tt, w), F32),
                        pltpu.VMEM((ns, tt, LANES), F32), pltpu.VMEM((SUBLANES, w), F32)],
        compiler_params=_cparams(2),
        name="rglru",
    )(proj, proj, wa_bd, ba.reshape(1, w), wx_bd, bx.reshape(1, w), lam.reshape(1, w))


def _s5_kernel(u_ref, bend_ref, kc_ref, l8r_ref, l8i_ref, plr_ref, pli_ref, d_ref, wg_ref, bg_ref,
               o_ref, lp, st, yv, carry, *, ls):
    sc = S5_SC
    n_cs = l8r_ref.shape[1]
    stride = sc * ls
    cre = [slice(LANES * k, LANES * (k + 1)) for k in range(n_cs)]
    cim = [slice(LANES * (n_cs + k), LANES * (n_cs + k + 1)) for k in range(n_cs)]
    lanes = [slice(LANES * q, LANES * (q + 1)) for q in range(sc)]

    @pl.when(pl.program_id(2) == 0)
    def _():
        carry[...] = jnp.zeros(carry.shape, F32)

    def gather_body(tau, c):
        for sg in range(sc):
            lp[_rows(tau), lanes[sg]] = u_ref[0, 0, pl.ds(tau * sc + sg, SUBLANES, stride=stride), :]
        return c

    lax.fori_loop(0, ls, gather_body, 0)
    st[...] = jnp.dot(lp[...].astype(BF16), bend_ref[0], preferred_element_type=F32)

    lam = [(jnp.broadcast_to(l8r_ref[0, k], (SUBLANES, LANES)),
            jnp.broadcast_to(l8i_ref[0, k], (SUBLANES, LANES))) for k in range(n_cs)]

    def step(tau, k, sr, si):
        lr, li = lam[k]
        return (lr * sr - li * si + st[_rows(tau), cre[k]], lr * si + li * sr + st[_rows(tau), cim[k]])

    def pass1(tau, c):
        out = []
        for k in range(n_cs):
            out += list(step(tau, k, c[2 * k], c[2 * k + 1]))
        return tuple(out)

    zero = jnp.zeros((SUBLANES, LANES), F32)
    ends = lax.fori_loop(0, ls, pass1, (zero,) * (2 * n_cs))

    starts = []
    for k in range(n_cs):
        er, ei = ends[2 * k], ends[2 * k + 1]
        pr, pi = plr_ref[0, k], pli_ref[0, k]
        c_r, c_i = carry[0:1, cre[k]], carry[0:1, cim[k]]
        rows_r, rows_i = [], []
        for j in range(SUBLANES):
            rows_r.append(c_r)
            rows_i.append(c_i)
            c_r, c_i = (pr * c_r - pi * c_i + er[j:j + 1], pr * c_i + pi * c_r + ei[j:j + 1])
        carry[0:1, cre[k]] = c_r
        carry[0:1, cim[k]] = c_i
        starts += [jnp.concatenate(rows_r, axis=0), jnp.concatenate(rows_i, axis=0)]

    def pass2(tau, c):
        out = []
        for k in range(n_cs):
            nr, ni = step(tau, k, c[2 * k], c[2 * k + 1])
            st[_rows(tau), cre[k]] = c[2 * k]
            st[_rows(tau), cim[k]] = c[2 * k + 1]
            out += [nr, ni]
        return tuple(out)

    lax.fori_loop(0, ls, pass2, tuple(starts))

    lhs = jnp.concatenate([lp[...], st[...]], axis=1).astype(BF16)
    yv[...] = jnp.dot(lhs, kc_ref[0], preferred_element_type=F32)
    for q in range(sc):
        y = yv[:, lanes[q]] + d_ref[0] * lp[:, lanes[q]]
        y = _gelu_tanh(y)
        yv[:, lanes[q]] = y * _sigmoid(jnp.dot(y.astype(BF16), wg_ref[0], preferred_element_type=F32)
                                       + bg_ref[0])

    def scatter_body(tau, c):
        for sg in range(sc):
            o_ref[0, 0, pl.ds(tau * sc + sg, SUBLANES, stride=stride), :] = yv[_rows(tau), lanes[sg]]
        return c

    lax.fori_loop(0, ls, scatter_body, 0)


def s5(proj, first_slab, bend, kc, l8r, l8i, plr, pli, d, wg, bg, *, ls):
    b, _, tp, _ = proj.shape
    nblk, kin, nst = bend.shape
    n_cs = nst // (2 * LANES)
    tt = SUBLANES * ls * S5_SC
    nc = SUBLANES * ls
    per_blk = lambda shape: pl.BlockSpec((1,) + shape, lambda bi, gb, t: (gb,) + (0,) * len(shape))
    return pl.pallas_call(
        functools.partial(_s5_kernel, ls=ls),
        grid=(b, nblk, tp // tt),
        in_specs=[pl.BlockSpec((1, 1, tt, LANES), lambda bi, gb, t: (bi, first_slab + gb, t, 0)),
                  per_blk((kin, nst)), per_blk((kin + nst, kin)),
                  per_blk((n_cs, 1, LANES)), per_blk((n_cs, 1, LANES)),
                  per_blk((n_cs, 1, LANES)), per_blk((n_cs, 1, LANES)),
                  per_blk((1, LANES)), per_blk((LANES, LANES)), per_blk((1, LANES))],
        out_specs=pl.BlockSpec((1, 1, tt, LANES), lambda bi, gb, t: (bi, gb, t, 0)),
        out_shape=jax.ShapeDtypeStruct((b, nblk, tp, LANES), F32),
        scratch_shapes=[pltpu.VMEM((nc, kin), F32), pltpu.VMEM((nc, nst), F32),
                        pltpu.VMEM((nc, kin), F32), pltpu.VMEM((SUBLANES, nst), F32)],
        compiler_params=_cparams(3),
        name="s5",
    )(proj, bend, kc, l8r, l8i, plr, pli, d, wg, bg)


def _s5_params(a_re, a_im, b_re, b_im, c_re, c_im, d, log_dt, w_glu, b_glu, *, ls):
    g, p = a_re.shape
    sc = S5_SC
    gpb = LANES // S5_GROUP
    nblk = g // gpb
    dt = jnp.exp(log_dt)[:, None]
    mag = jnp.exp(a_re * dt)
    ar, ai = mag * jnp.cos(a_im * dt), mag * jnp.sin(a_im * dt)
    den = a_re * a_re + a_im * a_im
    fr = ((ar - 1.0) * a_re + ai * a_im) / den
    fi = (ai * a_re - (ar - 1.0) * a_im) / den
    bbar_re = fr[..., None] * b_re - fi[..., None] * b_im
    bbar_im = fr[..., None] * b_im + fi[..., None] * b_re

    def lam_pow(k):
        k = jnp.asarray(k, F32).reshape(-1, 1, 1)
        m = jnp.exp(a_re * dt * k)
        return m * jnp.cos(a_im * dt * k), m * jnp.sin(a_im * dt * k)

    pw_r, pw_i = lam_pow(jnp.arange(sc + 1))
    def group_diag(dense, rows_per_group):
        n = dense.shape[-1]
        tiled = jnp.tile(dense, (1,) * (dense.ndim - 1) + (gpb,))
        rg = lax.broadcasted_iota(jnp.int32, tiled.shape[-2:], 0) // rows_per_group
        cg = lax.broadcasted_iota(jnp.int32, tiled.shape[-2:], 1) // n
        return jnp.where(rg == cg, tiled, 0.0)

    def by_block(m):
        k, _, a, b2 = m.shape
        return jnp.transpose(m.reshape(k, nblk, gpb, a, b2), (0, 1, 2, 4, 3)).reshape(k, nblk, gpb * b2, a)

    wr = jnp.stack([pw_r[sc - 1 - s] for s in range(sc)])[..., None]
    wi = jnp.stack([pw_i[sc - 1 - s] for s in range(sc)])[..., None]
    e_re = wr * bbar_re - wi * bbar_im
    e_im = wr * bbar_im + wi * bbar_re
    rows_cat = lambda m: jnp.concatenate([m[k] for k in range(m.shape[0])], axis=-2)
    bend = jnp.concatenate([rows_cat(group_diag(by_block(e_re), S5_GROUP)),
                            rows_cat(group_diag(by_block(e_im), S5_GROUP))], axis=-1).astype(BF16)

    cl_re = c_re[None] * pw_r[:, :, None, :] - c_im[None] * pw_i[:, :, None, :]
    cl_im = c_re[None] * pw_i[:, :, None, :] + c_im[None] * pw_r[:, :, None, :]
    kl = (jnp.einsum('kgip,gpj->kgij', cl_re[:sc], bbar_re)
          - jnp.einsum('kgip,gpj->kgij', cl_im[:sc], bbar_im))
    kd = group_diag(by_block(kl), S5_GROUP)
    zero_blk = jnp.zeros_like(kd[0])
    kintra = jnp.concatenate(
        [jnp.concatenate([kd[t - s] if t >= s else zero_blk for t in range(sc)], axis=-1)
         for s in range(sc)], axis=-2)

    def out_bd(m):
        dense = jnp.transpose(m, (0, 1, 3, 2))
        d2 = group_diag(dense.reshape(sc, nblk, gpb * p, S5_GROUP), p)
        return jnp.concatenate([d2[t] for t in range(sc)], axis=-1)

    kc = jnp.concatenate([kintra, out_bd(cl_re[1:]), -out_bd(cl_im[1:])], axis=-2).astype(BF16)

    vec = lambda v: v.reshape(nblk, (gpb * p) // LANES, 1, LANES)
    pl_r, pl_i = lam_pow(jnp.asarray([sc * ls]))
    wg = group_diag(w_glu.reshape(nblk, LANES, S5_GROUP), S5_GROUP).astype(BF16)
    return (bend, kc, vec(pw_r[sc]), vec(pw_i[sc]), vec(pl_r[0]), vec(pl_i[0]),
            d.reshape(nblk, 1, LANES), wg, b_glu.reshape(nblk, 1, LANES))


def _ssd_kernel(xbc_ref, cs_ref, rowp_ref, wend_ref, cdec_ref, dskip_ref, o_ref, hst):
    @pl.when(pl.program_id(1) == 0)
    def _():
        hst[...] = jnp.zeros(hst.shape, F32)

    for c in range(xbc_ref.shape[2] // SSD_CHUNK):
        _ssd_chunk(xbc_ref, cs_ref, rowp_ref, wend_ref, cdec_ref, dskip_ref, o_ref, hst, c)


def _ssd_chunk(xbc_ref, cs_ref, rowp_ref, wend_ref, cdec_ref, dskip_ref, o_ref, hst, c):
    L = SSD_CHUNK
    rs = slice(L * c, L * (c + 1))
    n_groups = hst.shape[0]
    gw = SSD_HPG * SSD_HEAD_DIM
    spg = gw // LANES
    n_xs = n_groups * spg
    row = lax.broadcasted_iota(jnp.int32, (L, L), 0)
    col = lax.broadcasted_iota(jnp.int32, (L, L), 1)
    causal = row >= col
    left_half = col < SSD_HEAD_DIM
    head_of_lane = lax.broadcasted_iota(jnp.int32, (L, gw), 1) // SSD_HEAD_DIM

    def head_rows(ref, g):
        return jnp.concatenate(
            [jnp.broadcast_to(ref[0, c, SSD_HPG * g + r:SSD_HPG * g + r + 1, :], (SSD_HEAD_DIM, L))
             for r in range(SSD_HPG)], axis=0)

    def x_slabs(g):
        return jnp.concatenate([xbc_ref[0, spg * g + k, rs, :] for k in range(spg)], axis=-1)

    scores, y_offs = [], []
    for g in range(n_groups):
        bm = xbc_ref[0, n_xs + g, rs, :]
        cm = xbc_ref[0, n_xs + n_groups + g, rs, :]
        scores.append(lax.dot_general(cm, bm, (((1,), (1,)), ((), ())), preferred_element_type=F32))
        h_prev = hst[g]
        y_offs.append(lax.dot_general(cm, h_prev.astype(BF16), (((1,), (1,)), ((), ())),
                                      preferred_element_type=F32))
        xw_t = (x_slabs(g).astype(F32).T * head_rows(wend_ref, g)).astype(BF16)
        hst[g] = head_rows(cdec_ref, g) * h_prev + jnp.dot(xw_t, bm, preferred_element_type=F32)

    for g in range(n_groups):
        xsb = x_slabs(g)
        ms, bcs = [], []
        for r in range(SSD_HPG):
            h = SSD_HPG * g + r
            bcs.append(jnp.broadcast_to(cs_ref[0, c, :, h:h + 1], (L, L)))
            ms.append((scores[g] * jnp.exp2(jnp.where(causal, bcs[r] - rowp_ref[0, c, h:h + 1, :], -jnp.inf))
                       ).astype(BF16))
        x_bd = jnp.concatenate([jnp.where(head_of_lane == r, xsb, jnp.zeros_like(xsb))
                                for r in range(SSD_HPG)], axis=0)
        y_diag = jnp.dot(jnp.concatenate(ms, axis=1), x_bd, preferred_element_type=F32)
        f_start = jnp.concatenate([jnp.exp2(jnp.where(left_half, bcs[2 * k], bcs[2 * k + 1]))
                                   for k in range(spg)], axis=-1)
        o_ref[0, rs, gw * g:gw * (g + 1)] = (
            y_diag + y_offs[g] * f_start + dskip_ref[:, gw * g:gw * (g + 1)] * xsb.astype(F32)
        ).astype(o_ref.dtype)


def ssd_core(xbc, tables, dskip):
    b, n_conv, tp, _ = xbc.shape
    d_inner = dskip.shape[1]
    n_groups = d_inner // (SSD_HPG * SSD_HEAD_DIM)
    cps = SSD_CHUNKS_PER_STEP
    L = SSD_CHUNK * cps
    const = lambda shape: pl.BlockSpec(shape, lambda bi, c: (0,) * len(shape))
    tab = pl.BlockSpec((1, cps, SSD_CHUNK, LANES), lambda bi, c: (bi, c, 0, 0))
    return pl.pallas_call(
        _ssd_kernel,
        grid=(b, tp // L),
        in_specs=[pl.BlockSpec((1, n_conv, L, LANES), lambda bi, c: (bi, 0, c, 0)),
                  tab, tab, tab, tab, const((1, d_inner))],
        out_specs=pl.BlockSpec((1, L, d_inner), lambda bi, c: (bi, c, 0)),
        out_shape=jax.ShapeDtypeStruct((b, tp, d_inner), BF16),
        scratch_shapes=[pltpu.VMEM((n_groups, SSD_HPG * SSD_HEAD_DIM, SSD_STATE), F32)],
        compiler_params=_cparams(2),
        name="ssd_core",
    )(xbc, *tables, dskip)


def _block_diag(w):
    n, di, do = w.shape
    return jnp.einsum('gij,gh->gihj', w, jnp.eye(n, dtype=w.dtype)).reshape(n * di, n * do)


def _pad_lanes(v, fill=0.0):
    return jnp.pad(v, (0, LANES - v.shape[0]), constant_values=fill).reshape(1, LANES)


def kernel(x, meta_tokens, norm_mix, norm_mlp, norm_final, ev_w_in, lru_conv_w, lru_conv_b, lru_w_a, lru_b_a, lru_w_x, lru_b_x, lru_lambda, s5_a_re, s5_a_im, s5_b_re, s5_b_im, s5_c_re, s5_c_im, s5_d, s5_log_dt, s5_w_glu, s5_b_glu, ev_w_out, ssd_w_in, ssd_conv_w, ssd_conv_b, ssd_dt_bias, ssd_a_log, ssd_d, ssd_norm, ssd_w_out, mlp_w_up, mlp_w_down):
    bsz, seq, d = x.shape
    t = seq + N_META
    unit = math.lcm(SUBLANES * LRU_LS, SUBLANES * S5_LS * S5_SC, ROW_TILE, SSD_CHUNK)
    tp = -(-t // unit) * unit
    meta = jnp.broadcast_to(meta_tokens[None].astype(x.dtype), (bsz, N_META, d))
    h = jnp.concatenate([meta, x, jnp.zeros((bsz, tp - t, d), x.dtype)], axis=1)

    lru_w = lru_conv_w.shape[-1]
    proj = norm_matmul(h, norm_mix[0], ev_w_in[0].astype(BF16), lru_conv_w[0], lru_conv_b[0], tm=ROW_TILE)
    y_lru = rglru(proj, _block_diag(lru_w_a[0]).astype(BF16), lru_b_a[0],
                  _block_diag(lru_w_x[0]).astype(BF16), lru_b_x[0], lru_lambda[0], ls=LRU_LS)
    y_s5 = s5(proj, 2 * lru_w // LANES,
              *_s5_params(s5_a_re[0], s5_a_im[0], s5_b_re[0], s5_b_im[0], s5_c_re[0], s5_c_im[0],
                          s5_d[0], s5_log_dt[0], s5_w_glu[0], s5_b_glu[0], ls=S5_LS), ls=S5_LS)
    h = outproj_slab(h, y_lru, y_s5, ev_w_out[0].astype(BF16), tm=ROW_TILE)
    h = mlp_residual(h.reshape(bsz * tp, d), norm_mlp[0], mlp_w_up, mlp_w_down,
                     norm_final, layer=0, tm=ROW_TILE, tf=512,
                     final_norm=False).reshape(bsz, tp, d)

    d_inner = ssd_w_out.shape[1]
    conv_dim = ssd_conv_w.shape[-1]
    n_heads = ssd_dt_bias.shape[-1]
    w_in = ssd_w_in[0]
    w_zdt = jnp.concatenate([w_in[:, :d_inner], w_in[:, d_inner + conv_dim:],
                             jnp.zeros((d, LANES - n_heads), F32)], axis=1).astype(BF16)
    xbc = inproj_conv(h, norm_mix[1], w_in[:, d_inner:d_inner + conv_dim].astype(BF16),
                      ssd_conv_w[0], ssd_conv_b[0], tm=ROW_TILE, tn=1024)
    z, *tables = inproj_zdt(h, norm_mix[1], w_zdt, _pad_lanes(ssd_dt_bias[0]), _pad_lanes(ssd_a_log[0]),
                            tm=ROW_TILE)
    y = ssd_core(xbc, tables, jnp.repeat(ssd_d[0], SSD_HEAD_DIM).reshape(1, d_inner))
    h = gated_outproj(h, y, z, ssd_norm[0].reshape(1, d_inner), ssd_w_out[0].astype(BF16),
                      tm=ROW_TILE // 2, group=SSD_HPG * SSD_HEAD_DIM)
    return mlp_residual_final(h, norm_mlp[1], mlp_w_up, mlp_w_down,
                              norm_final, layer=1, seq=seq, tm=FINAL_ROW_TILE, tf=512)
```

```python
import functools
import math

import jax
import jax.numpy as jnp
from jax import lax
from jax.experimental import pallas as pl
from jax.experimental.pallas import tpu as pltpu

F32 = jnp.float32
BF16 = jnp.bfloat16

LANES = 128
SUBLANES = 8
NORM_EPS = 1e-5
LOG2_E = 1.4426950408889634
N_META = 16
CONV_WIDTH = 4
LRU_C = 8.0
S5_GROUP = 16
S5_STATE = 64
SSD_HEAD_DIM = 64
SSD_STATE = 128
SSD_CHUNK = 128
SSD_HPG = 4
SSD_CHUNKS_PER_STEP = 5

LRU_LS = 130
S5_SC = SUBLANES
S5_LS = 65
ROW_TILE = 1280
MLP_ROW_TILE = 2080
FINAL_ROW_TILE = 2048
VMEM_LIMIT = 48 * 1024 * 1024
VMEM_LIMIT_LARGE = 58 * 1024 * 1024


def _cparams(n_axes, vmem=VMEM_LIMIT):
    return pltpu.CompilerParams(dimension_semantics=("arbitrary",) * n_axes,
                                vmem_limit_bytes=vmem)


def _sigmoid(x):
    return 0.5 + 0.5 * jnp.tanh(0.5 * x)


def _silu(x):
    hx = 0.5 * x
    return hx + hx * jnp.tanh(hx)


def _gelu_tanh(x):
    return 0.5 * x * (1.0 + jnp.tanh(math.sqrt(2.0 / math.pi) * (x + 0.044715 * (x * x * x))))


def _log1p(e):
    u = 1.0 + e
    return jnp.where(u == 1.0, e, jnp.log(u) * (e / (u - 1.0)))


def _softplus(x):
    return jnp.maximum(x, 0.0) + _log1p(jnp.exp(-jnp.abs(x)))


def _rmsnorm_rows(x, w):
    ms = jnp.mean(x * x, axis=-1, keepdims=True)
    return x * lax.rsqrt(ms + NORM_EPS) * w


def _rows(tau):
    return pl.ds(pl.multiple_of(tau * SUBLANES, SUBLANES), SUBLANES)


def _norm_matmul_kernel(x_ref, nw_ref, w_ref, cw_ref, cb_ref, o_ref, rbuf, hist):
    tm = x_ref.shape[1]
    n_conv = rbuf.shape[0]
    hrows = SUBLANES

    @pl.when(pl.program_id(1) == 0)
    def _():
        hist[...] = jnp.zeros(hist.shape, F32)

    x = x_ref[0]
    rs = lax.rsqrt(jnp.mean(x * x, axis=-1, keepdims=True) + NORM_EPS)
    xb = (x * nw_ref[...]).astype(BF16)
    step = 4
    for c in range(o_ref.shape[1] // step):
        r = jnp.dot(xb, w_ref[:, LANES * step * c:LANES * step * (c + 1)], preferred_element_type=F32) * rs
        for k in range(step):
            s = step * c + k
            if s >= n_conv:
                o_ref[0, s] = r[:, LANES * k:LANES * (k + 1)]
                continue
            sl = slice(LANES * s, LANES * (s + 1))
            rbuf[s, 0:hrows, :] = hist[s]
            rbuf[s, hrows:hrows + tm, :] = r[:, LANES * k:LANES * (k + 1)]
            hist[s] = rbuf[s, tm:tm + hrows, :]
            taps = [cw_ref[m, :, sl] for m in range(CONV_WIDTH)]
            bias = cb_ref[:, sl]
            for g in range(tm // SUBLANES):
                r0 = hrows + SUBLANES * g
                acc = bias
                for m in range(CONV_WIDTH):
                    acc = acc + taps[m] * rbuf[s, r0 - m:r0 - m + SUBLANES, :]
                o_ref[0, s, SUBLANES * g:SUBLANES * (g + 1), :] = acc


def norm_matmul(h, nw, w, cw, cb, *, tm):
    b, tp, d = h.shape
    n = w.shape[1]
    wc = cw.shape[1]
    cw8 = jnp.broadcast_to(jnp.stack([cw[CONV_WIDTH - 1 - m] for m in range(CONV_WIDTH)])[:, None, :],
                           (CONV_WIDTH, SUBLANES, wc))
    cb8 = jnp.broadcast_to(cb[None, :], (SUBLANES, wc))
    const = lambda shape: pl.BlockSpec(shape, lambda bi, i: (0,) * len(shape))
    return pl.pallas_call(
        _norm_matmul_kernel,
        grid=(b, tp // tm),
        in_specs=[pl.BlockSpec((1, tm, d), lambda bi, i: (bi, i, 0)),
                  const((1, d)), const((d, n)), const((CONV_WIDTH, SUBLANES, wc)), const((SUBLANES, wc))],
        out_specs=pl.BlockSpec((1, n // LANES, tm, LANES), lambda bi, i: (bi, 0, i, 0)),
        out_shape=jax.ShapeDtypeStruct((b, n // LANES, tp, LANES), F32),
        scratch_shapes=[pltpu.VMEM((wc // LANES, SUBLANES + tm, LANES), F32),
                        pltpu.VMEM((wc // LANES, SUBLANES, LANES), F32)],
        compiler_params=_cparams(2),
        name="norm_matmul",
    )(h, nw.reshape(1, d), w, cw8, cb8)


def _inproj_conv_kernel(x_ref, nw_ref, w_ref, cw_ref, cb_ref, o_ref, xn_ref, rs, hist):
    i, j = pl.program_id(1), pl.program_id(2)
    tm = x_ref.shape[1]
    n_slab = o_ref.shape[1]
    hrows = SUBLANES
    pack = 2 * SUBLANES

    @pl.when(j == 0)
    def _():
        xn_ref[...] = _rmsnorm_rows(x_ref[0], nw_ref[...]).astype(BF16)

    @pl.when(i == 0)
    def _():
        hist[j] = jnp.zeros(hist.shape[1:], F32)

    xn = xn_ref[...]
    proj = lambda c: jnp.dot(xn, w_ref[:, 2 * LANES * c:2 * LANES * (c + 1)], preferred_element_type=F32)
    r_next = proj(0)
    for c in range(n_slab // 2):
        r = r_next
        if c + 1 < n_slab // 2:
            r_next = proj(c + 1)
        for k in range(2):
            s = 2 * c + k
            sl = slice(LANES * s, LANES * (s + 1))
            rs[s, 0:hrows, :] = hist[j, s]
            rs[s, hrows:hrows + tm, :] = r[:, LANES * k:LANES * (k + 1)]
            hist[j, s] = rs[s, tm:tm + hrows, :]
            taps = [cw_ref[m, :, sl] for m in range(CONV_WIDTH)]
            bias = cb_ref[:, sl]
            for g in range(tm // pack):
                halves = []
                for hh in range(pack // SUBLANES):
                    r0 = hrows + pack * g + SUBLANES * hh
                    acc = bias
                    for m in range(CONV_WIDTH):
                        acc = acc + taps[m] * rs[s, r0 - m:r0 - m + SUBLANES, :]
                    halves.append(acc)
                o_ref[0, s, pack * g:pack * (g + 1), :] = _silu(jnp.concatenate(halves, axis=0).astype(BF16))


def inproj_conv(h, nw, w, cw, cb, *, tm, tn):
    b, tp, d = h.shape
    n = w.shape[1]
    ns = tn // LANES
    cw8 = jnp.broadcast_to(jnp.stack([cw[CONV_WIDTH - 1 - m] for m in range(CONV_WIDTH)])[:, None, :],
                           (CONV_WIDTH, SUBLANES, n))
    cb8 = jnp.broadcast_to(cb.reshape(1, n), (SUBLANES, n))
    return pl.pallas_call(
        _inproj_conv_kernel,
        grid=(b, tp // tm, n // tn),
        in_specs=[pl.BlockSpec((1, tm, d), lambda bi, i, j: (bi, i, 0)),
                  pl.BlockSpec((1, d), lambda bi, i, j: (0, 0)),
                  pl.BlockSpec((d, tn), lambda bi, i, j: (0, j)),
                  pl.BlockSpec((CONV_WIDTH, SUBLANES, tn), lambda bi, i, j: (0, 0, j)),
                  pl.BlockSpec((SUBLANES, tn), lambda bi, i, j: (0, j))],
        out_specs=pl.BlockSpec((1, ns, tm, LANES), lambda bi, i, j: (bi, j, i, 0)),
        out_shape=jax.ShapeDtypeStruct((b, n // LANES, tp, LANES), BF16),
        scratch_shapes=[pltpu.VMEM((tm, d), BF16),
                        pltpu.VMEM((ns, SUBLANES + tm, LANES), F32),
                        pltpu.VMEM((n // tn, ns, SUBLANES, LANES), F32)],
        compiler_params=_cparams(3),
        name="inproj_conv",
    )(h, nw.reshape(1, d), w, cw8, cb8)


def _ssd_tables(dt_raw, dtb, a, tri):
    L = SSD_CHUNK
    dt = _softplus(dt_raw + dtb)
    da = dt * a
    da_hi = da.astype(BF16)
    r1 = da - da_hi.astype(F32)
    da_mid = r1.astype(BF16)
    da_lo = (r1 - da_mid.astype(F32)).astype(BF16)
    cs = (jnp.dot(tri, da_hi, preferred_element_type=F32)
          + jnp.dot(tri, da_mid, preferred_element_type=F32)
          + jnp.dot(tri, da_lo, preferred_element_type=F32))
    dt_t = dt.T
    cs_t = cs.T
    last_t = jnp.broadcast_to(cs_t[:, L - 1:L], (L, L))
    return cs * LOG2_E, (cs_t - jnp.log(dt_t)) * LOG2_E, dt_t * jnp.exp(last_t - cs_t), jnp.exp(last_t)


def _inproj_zdt_kernel(x_ref, nw_ref, w_ref, dtb_ref, alog_ref, z_ref, cs_ref, rowp_ref, wend_ref, cdec_ref):
    x = x_ref[0]
    rs = lax.rsqrt(jnp.mean(x * x, axis=-1, keepdims=True) + NORM_EPS)
    xb = (x * nw_ref[...]).astype(BF16)
    n_z = z_ref.shape[1]
    dt_raw = jnp.dot(xb, w_ref[:, LANES * n_z:LANES * (n_z + 1)], preferred_element_type=F32) * rs
    L = SSD_CHUNK
    tri = (lax.broadcasted_iota(jnp.int32, (L, L), 0) >= lax.broadcasted_iota(jnp.int32, (L, L), 1)).astype(BF16)
    a = -jnp.exp(alog_ref[...])
    step = 4
    n_chunks = cs_ref.shape[1]
    for c in range(n_z // step):
        r = jnp.dot(xb, w_ref[:, LANES * step * c:LANES * step * (c + 1)], preferred_element_type=F32) * rs
        for k in range(step):
            z_ref[0, step * c + k] = r[:, LANES * k:LANES * (k + 1)].astype(BF16)
        for q in range(c * n_chunks // (n_z // step), (c + 1) * n_chunks // (n_z // step)):
            cs_ref[0, q], rowp_ref[0, q], wend_ref[0, q], cdec_ref[0, q] = _ssd_tables(
                dt_raw[L * q:L * (q + 1)], dtb_ref[...], a, tri)


def inproj_zdt(h, nw, w, dtb, alog, *, tm):
    b, tp, d = h.shape
    n_z = w.shape[1] // LANES - 1
    L = SSD_CHUNK
    const = lambda shape: pl.BlockSpec(shape, lambda bi, i: (0,) * len(shape))
    tab = pl.BlockSpec((1, tm // L, L, LANES), lambda bi, i: (bi, i, 0, 0))
    return pl.pallas_call(
        _inproj_zdt_kernel,
        grid=(b, tp // tm),
        in_specs=[pl.BlockSpec((1, tm, d), lambda bi, i: (bi, i, 0)),
                  const((1, d)), const(w.shape), const((1, LANES)), const((1, LANES))],
        out_specs=[pl.BlockSpec((1, n_z, tm, LANES), lambda bi, i: (bi, 0, i, 0)), tab, tab, tab, tab],
        out_shape=[jax.ShapeDtypeStruct((b, n_z, tp, LANES), BF16)]
        + [jax.ShapeDtypeStruct((b, tp // L, L, LANES), F32)] * 4,
        compiler_params=_cparams(2),
        name="inproj_zdt",
    )(h, nw.reshape(1, d), w, dtb, alog)


def _mlp_kernel(x_ref, nw_ref, wu_ref, wd_ref, fw_ref, o_ref, xn_ref, *, final_norm):
    j = pl.program_id(1)

    @pl.when(j == 0)
    def _():
        x = x_ref[...]
        xn_ref[...] = _rmsnorm_rows(x, nw_ref[...]).astype(BF16)
        o_ref[...] = x

    u = jnp.dot(xn_ref[...], wu_ref[...].astype(BF16), preferred_element_type=F32)
    a = jnp.square(jnp.maximum(u, 0.0)).astype(BF16)
    o_ref[...] += jnp.dot(a, wd_ref[...].astype(BF16), preferred_element_type=F32)

    if final_norm:
        @pl.when(j == pl.num_programs(1) - 1)
        def _():
            o_ref[...] = _rmsnorm_rows(o_ref[...], fw_ref[...])


def mlp_residual(h2, nw, w_up, w_down, fw, *, layer, tm, tf, final_norm):
    n, d = h2.shape
    dff = w_up.shape[2]
    return pl.pallas_call(
        functools.partial(_mlp_kernel, final_norm=final_norm),
        grid=(n // tm, dff // tf),
        in_specs=[pl.BlockSpec((tm, d), lambda i, j: (i, 0)),
                  pl.BlockSpec((1, d), lambda i, j: (0, 0)),
                  pl.BlockSpec((None, d, tf), lambda i, j: (layer, 0, j)),
                  pl.BlockSpec((None, tf, d), lambda i, j: (layer, j, 0)),
                  pl.BlockSpec((1, d), lambda i, j: (0, 0))],
        out_specs=pl.BlockSpec((tm, d), lambda i, j: (i, 0)),
        out_shape=jax.ShapeDtypeStruct((n, d), F32),
        scratch_shapes=[pltpu.VMEM((tm, d), BF16)],
        compiler_params=_cparams(2, vmem=VMEM_LIMIT_LARGE),
        name="mlp_final" if final_norm else "mlp",
    )(h2, nw.reshape(1, d), w_up, w_down, fw.reshape(1, d))


def _mlp_final_kernel(x_ref, nw_ref, wu_ref, wd_ref, fw_ref, o_ref, xn_ref):
    j = pl.program_id(2)

    @pl.when(j == 0)
    def _():
        x = x_ref[0]
        xn_ref[...] = _rmsnorm_rows(x, nw_ref[...]).astype(BF16)
        o_ref[0] = x

    u = jnp.dot(xn_ref[...], wu_ref[...].astype(BF16), preferred_element_type=F32)
    a = jnp.square(jnp.maximum(u, 0.0)).astype(BF16)
    o_ref[0] += jnp.dot(a, wd_ref[...].astype(BF16), preferred_element_type=F32)

    @pl.when(j == pl.num_programs(2) - 1)
    def _():
        o_ref[0] = _rmsnorm_rows(o_ref[0], fw_ref[...])


def mlp_residual_final(h, nw, w_up, w_down, fw, *, layer, seq, tm, tf):
    b, _, d = h.shape
    dff = w_up.shape[2]
    return pl.pallas_call(
        _mlp_final_kernel,
        grid=(b, seq // tm, dff // tf),
        in_specs=[pl.BlockSpec((pl.Element(1), pl.Element(tm), pl.Element(d)),
                               lambda bi, i, j: (bi, pl.multiple_of(N_META + i * tm, SUBLANES), 0)),
                  pl.BlockSpec((1, d), lambda bi, i, j: (0, 0)),
                  pl.BlockSpec((None, d, tf), lambda bi, i, j: (layer, 0, j)),
                  pl.BlockSpec((None, tf, d), lambda bi, i, j: (layer, j, 0)),
                  pl.BlockSpec((1, d), lambda bi, i, j: (0, 0))],
        out_specs=pl.BlockSpec((1, tm, d), lambda bi, i, j: (bi, i, 0)),
        out_shape=jax.ShapeDtypeStruct((b, seq, d), F32),
        scratch_shapes=[pltpu.VMEM((tm, d), BF16)],
        compiler_params=_cparams(3, vmem=VMEM_LIMIT_LARGE),
        name="mlp_final",
    )(h, nw.reshape(1, d), w_up, w_down, fw.reshape(1, d))


def _outproj_slab_kernel(h_ref, ya_ref, yb_ref, w_ref, o_ref):
    parts = ([ya_ref[0, s].astype(BF16) for s in range(ya_ref.shape[1])]
             + [yb_ref[0, s].astype(BF16) for s in range(yb_ref.shape[1])])
    y = jnp.concatenate(parts, axis=-1)
    o_ref[0] = h_ref[0] + jnp.dot(y, w_ref[...], preferred_element_type=F32)


def outproj_slab(h, ya, yb, w, *, tm):
    b, tp, d = h.shape
    sa, sb = ya.shape[1], yb.shape[1]
    return pl.pallas_call(
        _outproj_slab_kernel,
        grid=(b, tp // tm),
        in_specs=[pl.BlockSpec((1, tm, d), lambda bi, i: (bi, i, 0)),
                  pl.BlockSpec((1, sa, tm, LANES), lambda bi, i: (bi, 0, i, 0)),
                  pl.BlockSpec((1, sb, tm, LANES), lambda bi, i: (bi, 0, i, 0)),
                  pl.BlockSpec(w.shape, lambda bi, i: (0, 0))],
        out_specs=pl.BlockSpec((1, tm, d), lambda bi, i: (bi, i, 0)),
        out_shape=jax.ShapeDtypeStruct(h.shape, F32),
        compiler_params=_cparams(2),
        name="outproj_slab",
    )(h, ya, yb, w)


def _gated_outproj_kernel(h_ref, y_ref, z_ref, nw_ref, w_ref, o_ref, *, group):
    acc = h_ref[0]
    spg = group // LANES
    for g in range(y_ref.shape[2] // group):
        cols = slice(group * g, group * (g + 1))
        zg = jnp.concatenate([z_ref[0, spg * g + k] for k in range(spg)], axis=-1).astype(F32)
        gg = y_ref[0, :, cols].astype(F32) * _silu(zg)
        ms = jnp.mean(gg * gg, axis=-1, keepdims=True)
        part = (gg * lax.rsqrt(ms + NORM_EPS) * nw_ref[:, cols]).astype(BF16)
        acc = acc + jnp.dot(part, w_ref[cols, :], preferred_element_type=F32)
    o_ref[0] = acc


def gated_outproj(h, y, z, nw, w, *, tm, group):
    b, tp, d = h.shape
    k = y.shape[-1]
    return pl.pallas_call(
        functools.partial(_gated_outproj_kernel, group=group),
        grid=(b, tp // tm),
        in_specs=[pl.BlockSpec((1, tm, d), lambda bi, i: (bi, i, 0)),
                  pl.BlockSpec((1, tm, k), lambda bi, i: (bi, i, 0)),
                  pl.BlockSpec((1, k // LANES, tm, LANES), lambda bi, i: (bi, 0, i, 0)),
                  pl.BlockSpec((1, k), lambda bi, i: (0, 0)),
                  pl.BlockSpec(w.shape, lambda bi, i: (0, 0))],
        out_specs=pl.BlockSpec((1, tm, d), lambda bi, i: (bi, i, 0)),
        out_shape=jax.ShapeDtypeStruct(h.shape, F32),
        compiler_params=_cparams(2),
        name="gated_outproj",
    )(h, y, z, nw, w)


def _rglru_kernel(x_ref, g_ref, wa_ref, ba_ref, wx_ref, bx_ref, lam_ref, o_ref,
                  xcp, a_s, b_s, hn, carry, *, ls):
    n_slab = x_ref.shape[1]
    slabs = [slice(LANES * s, LANES * (s + 1)) for s in range(n_slab)]

    @pl.when(pl.program_id(1) == 0)
    def _():
        carry[...] = jnp.zeros(carry.shape, F32)

    def gather_body(tau, c):
        for s in range(n_slab):
            xcp[_rows(tau), slabs[s]] = x_ref[0, s, pl.ds(tau, SUBLANES, stride=ls), :]
        return c

    lax.fori_loop(0, ls, gather_body, 0, unroll=2)

    xc = xcp[...]
    xb = xc.astype(BF16)
    r = _sigmoid(jnp.dot(xb, wa_ref[...], preferred_element_type=F32) + ba_ref[...])
    i = _sigmoid(jnp.dot(xb, wx_ref[...], preferred_element_type=F32) + bx_ref[...])
    log_a = (-LRU_C) * r * _softplus(-lam_ref[...])
    a = jnp.exp(log_a)
    a_s[...] = a
    v = -jnp.tanh(log_a) * (a * a + 1.0)
    b_s[...] = jnp.where(v > 0.0, v * lax.rsqrt(v), 0.0) * (i * xc)

    def pass1(tau, c):
        out = []
        for s in range(n_slab):
            p, e = c[2 * s], c[2 * s + 1]
            av = a_s[_rows(tau), slabs[s]]
            out += [p * av, av * e + b_s[_rows(tau), slabs[s]]]
        return tuple(out)

    one = jnp.ones((SUBLANES, LANES), F32)
    zero = jnp.zeros((SUBLANES, LANES), F32)
    pe = lax.fori_loop(0, ls, pass1, (one, zero) * n_slab, unroll=2)

    h0 = []
    for s in range(n_slab):
        p, e = pe[2 * s], pe[2 * s + 1]
        c = carry[0:1, slabs[s]]
        rows = []
        for j in range(SUBLANES):
            rows.append(c)
            c = p[j:j + 1] * c + e[j:j + 1]
        carry[0:1, slabs[s]] = c
        h0.append(jnp.concatenate(rows, axis=0))

    def pass2(tau, hs):
        out = []
        for s in range(n_slab):
            h = a_s[_rows(tau), slabs[s]] * hs[s] + b_s[_rows(tau), slabs[s]]
            hn[s, pl.ds(tau, SUBLANES, stride=ls), :] = h
            out.append(h)
        return tuple(out)

    lax.fori_loop(0, ls, pass2, tuple(h0), unroll=2)
    for s in range(n_slab):
        o_ref[0, s] = (hn[s] * _gelu_tanh(g_ref[0, s])).astype(o_ref.dtype)


def rglru(proj, wa_bd, ba, wx_bd, bx, lam, *, ls):
    b, _, tp, _ = proj.shape
    w = lam.shape[0]
    ns = w // LANES
    tt = SUBLANES * ls
    const = lambda shape: pl.BlockSpec(shape, lambda bi, t: (0,) * len(shape))
    return pl.pallas_call(
        functools.partial(_rglru_kernel, ls=ls),
        grid=(b, tp // tt),
        in_specs=[pl.BlockSpec((1, ns, tt, LANES), lambda bi, t: (bi, 0, t, 0)),
                  pl.BlockSpec((1, ns, tt, LANES), lambda bi, t: (bi, 1, t, 0)),
                  const((w, w)), const((1, w)), const((w, w)), const((1, w)), const((1, w))],
        out_specs=pl.BlockSpec((1, ns, tt, LANES), lambda bi, t: (bi, 0, t, 0)),
        out_shape=jax.ShapeDtypeStruct((b, ns, tp, LANES), BF16),
        scratch_shapes=[pltpu.VMEM((tt, w), F32), pltpu.VMEM((tt, w), F32), pltpu.VMEM((tt, w), F32),
                        pltpu.VMEM((ns, tt, LANES), F32), pltpu.VMEM((SUBLANES, w), F32)],
        compiler_params=_cparams(2),
        name="rglru",
    )(proj, proj, wa_bd, ba.reshape(1, w), wx_bd, bx.reshape(1, w), lam.reshape(1, w))


def _s5_kernel(u_ref, bend_ref, kc_ref, l8r_ref, l8i_ref, plr_ref, pli_ref, d_ref, wg_ref, bg_ref,
               o_ref, lp, st, yv, carry, *, ls):
    sc = S5_SC
    n_cs = l8r_ref.shape[1]
    stride = sc * ls
    cre = [slice(LANES * k, LANES * (k + 1)) for k in range(n_cs)]
    cim = [slice(LANES * (n_cs + k), LANES * (n_cs + k + 1)) for k in range(n_cs)]
    lanes = [slice(LANES * q, LANES * (q + 1)) for q in range(sc)]

    @pl.when(pl.program_id(2) == 0)
    def _():
        carry[...] = jnp.zeros(carry.shape, F32)

    def gather_body(tau, c):
        for sg in range(sc):
            lp[_rows(tau), lanes[sg]] = u_ref[0, 0, pl.ds(tau * sc + sg, SUBLANES, stride=stride), :]
        return c

    lax.fori_loop(0, ls, gather_body, 0)
    st[...] = jnp.dot(lp[...].astype(BF16), bend_ref[0], preferred_element_type=F32)

    lam = [(jnp.broadcast_to(l8r_ref[0, k], (SUBLANES, LANES)),
            jnp.broadcast_to(l8i_ref[0, k], (SUBLANES, LANES))) for k in range(n_cs)]

    def step(tau, k, sr, si):
        lr, li = lam[k]
        return (lr * sr - li * si + st[_rows(tau), cre[k]], lr * si + li * sr + st[_rows(tau), cim[k]])

    def pass1(tau, c):
        out = []
        for k in range(n_cs):
            out += list(step(tau, k, c[2 * k], c[2 * k + 1]))
        return tuple(out)

    zero = jnp.zeros((SUBLANES, LANES), F32)
    ends = lax.fori_loop(0, ls, pass1, (zero,) * (2 * n_cs))

    starts = []
    for k in range(n_cs):
        er, ei = ends[2 * k], ends[2 * k + 1]
        pr, pi = plr_ref[0, k], pli_ref[0, k]
        c_r, c_i = carry[0:1, cre[k]], carry[0:1, cim[k]]
        rows_r, rows_i = [], []
        for j in range(SUBLANES):
            rows_r.append(c_r)
            rows_i.append(c_i)
            c_r, c_i = (pr * c_r - pi * c_i + er[j:j + 1], pr * c_i + pi * c_r + ei[j:j + 1])
        carry[0:1, cre[k]] = c_r
        carry[0:1, cim[k]] = c_i
        starts += [jnp.concatenate(rows_r, axis=0), jnp.concatenate(rows_i, axis=0)]

    def pass2(tau, c):
        out = []
        for k in range(n_cs):
            nr, ni = step(tau, k, c[2 * k], c[2 * k + 1])
            st[_rows(tau), cre[k]] = c[2 * k]
            st[_rows(tau), cim[k]] = c[2 * k + 1]
            out += [nr, ni]
        return tuple(out)

    lax.fori_loop(0, ls, pass2, tuple(starts))

    lhs = jnp.concatenate([lp[...], st[...]], axis=1).astype(BF16)
    yv[...] = jnp.dot(lhs, kc_ref[0], preferred_element_type=F32)
    for q in range(sc):
        y = yv[:, lanes[q]] + d_ref[0] * lp[:, lanes[q]]
        y = _gelu_tanh(y)
        yv[:, lanes[q]] = y * _sigmoid(jnp.dot(y.astype(BF16), wg_ref[0], preferred_element_type=F32)
                                       + bg_ref[0])

    def scatter_body(tau, c):
        for sg in range(sc):
            o_ref[0, 0, pl.ds(tau * sc + sg, SUBLANES, stride=stride), :] = yv[_rows(tau), lanes[sg]]
        return c

    lax.fori_loop(0, ls, scatter_body, 0)


def s5(proj, first_slab, bend, kc, l8r, l8i, plr, pli, d, wg, bg, *, ls):
    b, _, tp, _ = proj.shape
    nblk, kin, nst = bend.shape
    n_cs = nst // (2 * LANES)
    tt = SUBLANES * ls * S5_SC
    nc = SUBLANES * ls
    per_blk = lambda shape: pl.BlockSpec((1,) + shape, lambda bi, gb, t: (gb,) + (0,) * len(shape))
    return pl.pallas_call(
        functools.partial(_s5_kernel, ls=ls),
        grid=(b, nblk, tp // tt),
        in_specs=[pl.BlockSpec((1, 1, tt, LANES), lambda bi, gb, t: (bi, first_slab + gb, t, 0)),
                  per_blk((kin, nst)), per_blk((kin + nst, kin)),
                  per_blk((n_cs, 1, LANES)), per_blk((n_cs, 1, LANES)),
                  per_blk((n_cs, 1, LANES)), per_blk((n_cs, 1, LANES)),
                  per_blk((1, LANES)), per_blk((LANES, LANES)), per_blk((1, LANES))],
        out_specs=pl.BlockSpec((1, 1, tt, LANES), lambda bi, gb, t: (bi, gb, t, 0)),
        out_shape=jax.ShapeDtypeStruct((b, nblk, tp, LANES), F32),
        scratch_shapes=[pltpu.VMEM((nc, kin), F32), pltpu.VMEM((nc, nst), F32),
                        pltpu.VMEM((nc, kin), F32), pltpu.VMEM((SUBLANES, nst), F32)],
        compiler_params=_cparams(3),
        name="s5",
    )(proj, bend, kc, l8r, l8i, plr, pli, d, wg, bg)


def _s5_params(a_re, a_im, b_re, b_im, c_re, c_im, d, log_dt, w_glu, b_glu, *, ls):
    g, p = a_re.shape
    sc = S5_SC
    gpb = LANES // S5_GROUP
    nblk = g // gpb
    dt = jnp.exp(log_dt)[:, None]
    mag = jnp.exp(a_re * dt)
    ar, ai = mag * jnp.cos(a_im * dt), mag * jnp.sin(a_im * dt)
    den = a_re * a_re + a_im * a_im
    fr = ((ar - 1.0) * a_re + ai * a_im) / den
    fi = (ai * a_re - (ar - 1.0) * a_im) / den
    bbar_re = fr[..., None] * b_re - fi[..., None] * b_im
    bbar_im = fr[..., None] * b_im + fi[..., None] * b_re

    def lam_pow(k):
        k = jnp.asarray(k, F32).reshape(-1, 1, 1)
        m = jnp.exp(a_re * dt * k)
        return m * jnp.cos(a_im * dt * k), m * jnp.sin(a_im * dt * k)

    pw_r, pw_i = lam_pow(jnp.arange(sc + 1))
    def group_diag(dense, rows_per_group):
        n = dense.shape[-1]
        tiled = jnp.tile(dense, (1,) * (dense.ndim - 1) + (gpb,))
        rg = lax.broadcasted_iota(jnp.int32, tiled.shape[-2:], 0) // rows_per_group
        cg = lax.broadcasted_iota(jnp.int32, tiled.shape[-2:], 1) // n
        return jnp.where(rg == cg, tiled, 0.0)

    def by_block(m):
        k, _, a, b2 = m.shape
        return jnp.transpose(m.reshape(k, nblk, gpb, a, b2), (0, 1, 2, 4, 3)).reshape(k, nblk, gpb * b2, a)

    wr = jnp.stack([pw_r[sc - 1 - s] for s in range(sc)])[..., None]
    wi = jnp.stack([pw_i[sc - 1 - s] for s in range(sc)])[..., None]
    e_re = wr * bbar_re - wi * bbar_im
    e_im = wr * bbar_im + wi * bbar_re
    rows_cat = lambda m: jnp.concatenate([m[k] for k in range(m.shape[0])], axis=-2)
    bend = jnp.concatenate([rows_cat(group_diag(by_block(e_re), S5_GROUP)),
                            rows_cat(group_diag(by_block(e_im), S5_GROUP))], axis=-1).astype(BF16)

    cl_re = c_re[None] * pw_r[:, :, None, :] - c_im[None] * pw_i[:, :, None, :]
    cl_im = c_re[None] * pw_i[:, :, None, :] + c_im[None] * pw_r[:, :, None, :]
    kl = (jnp.einsum('kgip,gpj->kgij', cl_re[:sc], bbar_re)
          - jnp.einsum('kgip,gpj->kgij', cl_im[:sc], bbar_im))
    kd = group_diag(by_block(kl), S5_GROUP)
    zero_blk = jnp.zeros_like(kd[0])
    kintra = jnp.concatenate(
        [jnp.concatenate([kd[t - s] if t >= s else zero_blk for t in range(sc)], axis=-1)
         for s in range(sc)], axis=-2)

    def out_bd(m):
        dense = jnp.transpose(m, (0, 1, 3, 2))
        d2 = group_diag(dense.reshape(sc, nblk, gpb * p, S5_GROUP), p)
        return jnp.concatenate([d2[t] for t in range(sc)], axis=-1)

    kc = jnp.concatenate([kintra, out_bd(cl_re[1:]), -out_bd(cl_im[1:])], axis=-2).astype(BF16)

    vec = lambda v: v.reshape(nblk, (gpb * p) // LANES, 1, LANES)
    pl_r, pl_i = lam_pow(jnp.asarray([sc * ls]))
    wg = group_diag(w_glu.reshape(nblk, LANES, S5_GROUP), S5_GROUP).astype(BF16)
    return (bend, kc, vec(pw_r[sc]), vec(pw_i[sc]), vec(pl_r[0]), vec(pl_i[0]),
            d.reshape(nblk, 1, LANES), wg, b_glu.reshape(nblk, 1, LANES))


def _ssd_kernel(xbc_ref, cs_ref, rowp_ref, wend_ref, cdec_ref, dskip_ref, o_ref, hst):
    @pl.when(pl.program_id(1) == 0)
    def _():
        hst[...] = jnp.zeros(hst.shape, F32)

    for c in range(xbc_ref.shape[2] // SSD_CHUNK):
        _ssd_chunk(xbc_ref, cs_ref, rowp_ref, wend_ref, cdec_ref, dskip_ref, o_ref, hst, c)


def _ssd_chunk(xbc_ref, cs_ref, rowp_ref, wend_ref, cdec_ref, dskip_ref, o_ref, hst, c):
    L = SSD_CHUNK
    rs = slice(L * c, L * (c + 1))
    n_groups = hst.shape[0]
    gw = SSD_HPG * SSD_HEAD_DIM
    spg = gw // LANES
    n_xs = n_groups * spg
    row = lax.broadcasted_iota(jnp.int32, (L, L), 0)
    col = lax.broadcasted_iota(jnp.int32, (L, L), 1)
    causal = row >= col
    left_half = col < SSD_HEAD_DIM
    head_of_lane = lax.broadcasted_iota(jnp.int32, (L, gw), 1) // SSD_HEAD_DIM

    def head_rows(ref, g):
        return jnp.concatenate(
            [jnp.broadcast_to(ref[0, c, SSD_HPG * g + r:SSD_HPG * g + r + 1, :], (SSD_HEAD_DIM, L))
             for r in range(SSD_HPG)], axis=0)

    def x_slabs(g):
        return jnp.concatenate([xbc_ref[0, spg * g + k, rs, :] for k in range(spg)], axis=-1)

    scores, y_offs = [], []
    for g in range(n_groups):
        bm = xbc_ref[0, n_xs + g, rs, :]
        cm = xbc_ref[0, n_xs + n_groups + g, rs, :]
        scores.append(lax.dot_general(cm, bm, (((1,), (1,)), ((), ())), preferred_element_type=F32))
        h_prev = hst[g]
        y_offs.append(lax.dot_general(cm, h_prev.astype(BF16), (((1,), (1,)), ((), ())),
                                      preferred_element_type=F32))
        xw_t = (x_slabs(g).astype(F32).T * head_rows(wend_ref, g)).astype(BF16)
        hst[g] = head_rows(cdec_ref, g) * h_prev + jnp.dot(xw_t, bm, preferred_element_type=F32)

    for g in range(n_groups):
        xsb = x_slabs(g)
        ms, bcs = [], []
        for r in range(SSD_HPG):
            h = SSD_HPG * g + r
            bcs.append(jnp.broadcast_to(cs_ref[0, c, :, h:h + 1], (L, L)))
            ms.append((scores[g] * jnp.exp2(jnp.where(causal, bcs[r] - rowp_ref[0, c, h:h + 1, :], -jnp.inf))
                       ).astype(BF16))
        x_bd = jnp.concatenate([jnp.where(head_of_lane == r, xsb, jnp.zeros_like(xsb))
                                for r in range(SSD_HPG)], axis=0)
        y_diag = jnp.dot(jnp.concatenate(ms, axis=1), x_bd, preferred_element_type=F32)
        f_start = jnp.concatenate([jnp.exp2(jnp.where(left_half, bcs[2 * k], bcs[2 * k + 1]))
                                   for k in range(spg)], axis=-1)
        o_ref[0, rs, gw * g:gw * (g + 1)] = (
            y_diag + y_offs[g] * f_start + dskip_ref[:, gw * g:gw * (g + 1)] * xsb.astype(F32)
        ).astype(o_ref.dtype)


def ssd_core(xbc, tables, dskip):
    b, n_conv, tp, _ = xbc.shape
    d_inner = dskip.shape[1]
    n_groups = d_inner // (SSD_HPG * SSD_HEAD_DIM)
    cps = SSD_CHUNKS_PER_STEP
    L = SSD_CHUNK * cps
    const = lambda shape: pl.BlockSpec(shape, lambda bi, c: (0,) * len(shape))
    tab = pl.BlockSpec((1, cps, SSD_CHUNK, LANES), lambda bi, c: (bi, c, 0, 0))
    return pl.pallas_call(
        _ssd_kernel,
        grid=(b, tp // L),
        in_specs=[pl.BlockSpec((1, n_conv, L, LANES), lambda bi, c: (bi, 0, c, 0)),
                  tab, tab, tab, tab, const((1, d_inner))],
        out_specs=pl.BlockSpec((1, L, d_inner), lambda bi, c: (bi, c, 0)),
        out_shape=jax.ShapeDtypeStruct((b, tp, d_inner), BF16),
        scratch_shapes=[pltpu.VMEM((n_groups, SSD_HPG * SSD_HEAD_DIM, SSD_STATE), F32)],
        compiler_params=_cparams(2),
        name="ssd_core",
    )(xbc, *tables, dskip)


def _block_diag(w):
    n, di, do = w.shape
    return jnp.einsum('gij,gh->gihj', w, jnp.eye(n, dtype=w.dtype)).reshape(n * di, n * do)


def _pad_lanes(v, fill=0.0):
    return jnp.pad(v, (0, LANES - v.shape[0]), constant_values=fill).reshape(1, LANES)


def kernel(x, meta_tokens, norm_mix, norm_mlp, norm_final, ev_w_in, lru_conv_w, lru_conv_b, lru_w_a, lru_b_a, lru_w_x, lru_b_x, lru_lambda, s5_a_re, s5_a_im, s5_b_re, s5_b_im, s5_c_re, s5_c_im, s5_d, s5_log_dt, s5_w_glu, s5_b_glu, ev_w_out, ssd_w_in, ssd_conv_w, ssd_conv_b, ssd_dt_bias, ssd_a_log, ssd_d, ssd_norm, ssd_w_out, mlp_w_up, mlp_w_down):
    bsz, seq, d = x.shape
    t = seq + N_META
    unit = math.lcm(SUBLANES * LRU_LS, SUBLANES * S5_LS * S5_SC, ROW_TILE, SSD_CHUNK)
    tp = -(-t // unit) * unit
    meta = jnp.broadcast_to(meta_tokens[None].astype(x.dtype), (bsz, N_META, d))
    h = jnp.concatenate([meta, x, jnp.zeros((bsz, tp - t, d), x.dtype)], axis=1)

    lru_w = lru_conv_w.shape[-1]
    proj = norm_matmul(h, norm_mix[0], ev_w_in[0].astype(BF16), lru_conv_w[0], lru_conv_b[0], tm=ROW_TILE)
    y_lru = rglru(proj, _block_diag(lru_w_a[0]).astype(BF16), lru_b_a[0],
                  _block_diag(lru_w_x[0]).astype(BF16), lru_b_x[0], lru_lambda[0], ls=LRU_LS)
    y_s5 = s5(proj, 2 * lru_w // LANES,
              *_s5_params(s5_a_re[0], s5_a_im[0], s5_b_re[0], s5_b_im[0], s5_c_re[0], s5_c_im[0],
                          s5_d[0], s5_log_dt[0], s5_w_glu[0], s5_b_glu[0], ls=S5_LS), ls=S5_LS)
    h = outproj_slab(h, y_lru, y_s5, ev_w_out[0].astype(BF16), tm=ROW_TILE)
    h = mlp_residual(h.reshape(bsz * tp, d), norm_mlp[0], mlp_w_up, mlp_w_down,
                     norm_final, layer=0, tm=MLP_ROW_TILE, tf=512,
                     final_norm=False).reshape(bsz, tp, d)

    d_inner = ssd_w_out.shape[1]
    conv_dim = ssd_conv_w.shape[-1]
    n_heads = ssd_dt_bias.shape[-1]
    w_in = ssd_w_in[0]
    w_zdt = jnp.concatenate([w_in[:, :d_inner], w_in[:, d_inner + conv_dim:],
                             jnp.zeros((d, LANES - n_heads), F32)], axis=1).astype(BF16)
    xbc = inproj_conv(h, norm_mix[1], w_in[:, d_inner:d_inner + conv_dim].astype(BF16),
                      ssd_conv_w[0], ssd_conv_b[0], tm=ROW_TILE, tn=1024)
    z, *tables = inproj_zdt(h, norm_mix[1], w_zdt, _pad_lanes(ssd_dt_bias[0]), _pad_lanes(ssd_a_log[0]),
                            tm=ROW_TILE)
    y = ssd_core(xbc, tables, jnp.repeat(ssd_d[0], SSD_HEAD_DIM).reshape(1, d_inner))
    h = gated_outproj(h, y, z, ssd_norm[0].reshape(1, d_inner), ssd_w_out[0].astype(BF16),
                      tm=ROW_TILE // 2, group=SSD_HPG * SSD_HEAD_DIM)
    return mlp_residual_final(h, norm_mlp[1], mlp_w_up, mlp_w_down,
                              norm_final, layer=1, seq=seq, tm=FINAL_ROW_TILE, tf=512)
```

```python
import functools
import math

import jax
import jax.numpy as jnp
from jax import lax
from jax.experimental import pallas as pl
from jax.experimental.pallas import tpu as pltpu

F32 = jnp.float32
BF16 = jnp.bfloat16

LANES = 128
SUBLANES = 8
NORM_EPS = 1e-5
LOG2_E = 1.4426950408889634
N_META = 16
CONV_WIDTH = 4
LRU_C = 8.0
S5_GROUP = 16
S5_STATE = 64
SSD_HEAD_DIM = 64
SSD_STATE = 128
SSD_CHUNK = 128
SSD_HPG = 4
SSD_CHUNKS_PER_STEP = 5

LRU_LS = 130
S5_SC = SUBLANES
S5_LS = 65
ROW_TILE = 1280
MLP_ROW_TILE = 2080
GATED_ROW_TILE = 640
FINAL_ROW_TILE = 2048
VMEM_LIMIT = 48 * 1024 * 1024
VMEM_LIMIT_LARGE = 58 * 1024 * 1024


def _cparams(n_axes, vmem=VMEM_LIMIT):
    return pltpu.CompilerParams(dimension_semantics=("arbitrary",) * n_axes,
                                vmem_limit_bytes=vmem)


def _sigmoid(x):
    return 0.5 + 0.5 * jnp.tanh(0.5 * x)


def _silu(x):
    hx = 0.5 * x
    return hx + hx * jnp.tanh(hx)


def _gelu_tanh(x):
    return 0.5 * x * (1.0 + jnp.tanh(math.sqrt(2.0 / math.pi) * (x + 0.044715 * (x * x * x))))


def _log1p(e):
    u = 1.0 + e
    return jnp.where(u == 1.0, e, jnp.log(u) * (e / (u - 1.0)))


def _softplus(x):
    return jnp.maximum(x, 0.0) + _log1p(jnp.exp(-jnp.abs(x)))


def _rmsnorm_rows(x, w):
    ms = jnp.mean(x * x, axis=-1, keepdims=True)
    return x * lax.rsqrt(ms + NORM_EPS) * w


def _rows(tau):
    return pl.ds(pl.multiple_of(tau * SUBLANES, SUBLANES), SUBLANES)


def _norm_matmul_kernel(x_ref, nw_ref, w_ref, cw_ref, cb_ref, o_ref, rbuf, hist):
    tm = x_ref.shape[1]
    n_conv = rbuf.shape[0]
    hrows = SUBLANES

    @pl.when(pl.program_id(1) == 0)
    def _():
        hist[...] = jnp.zeros(hist.shape, F32)

    x = x_ref[0]
    rs = lax.rsqrt(jnp.mean(x * x, axis=-1, keepdims=True) + NORM_EPS)
    xb = (x * nw_ref[...]).astype(BF16)
    step = 4
    for c in range(o_ref.shape[1] // step):
        r = jnp.dot(xb, w_ref[:, LANES * step * c:LANES * step * (c + 1)], preferred_element_type=F32) * rs
        for k in range(step):
            s = step * c + k
            if s >= n_conv:
                o_ref[0, s] = r[:, LANES * k:LANES * (k + 1)]
                continue
            sl = slice(LANES * s, LANES * (s + 1))
            rbuf[s, 0:hrows, :] = hist[s]
            rbuf[s, hrows:hrows + tm, :] = r[:, LANES * k:LANES * (k + 1)]
            hist[s] = rbuf[s, tm:tm + hrows, :]
            taps = [cw_ref[m, :, sl] for m in range(CONV_WIDTH)]
            bias = cb_ref[:, sl]
            for g in range(tm // SUBLANES):
                r0 = hrows + SUBLANES * g
                acc = bias
                for m in range(CONV_WIDTH):
                    acc = acc + taps[m] * rbuf[s, r0 - m:r0 - m + SUBLANES, :]
                o_ref[0, s, SUBLANES * g:SUBLANES * (g + 1), :] = acc


def norm_matmul(h, nw, w, cw, cb, *, tm):
    b, tp, d = h.shape
    n = w.shape[1]
    wc = cw.shape[1]
    cw8 = jnp.broadcast_to(jnp.stack([cw[CONV_WIDTH - 1 - m] for m in range(CONV_WIDTH)])[:, None, :],
                           (CONV_WIDTH, SUBLANES, wc))
    cb8 = jnp.broadcast_to(cb[None, :], (SUBLANES, wc))
    const = lambda shape: pl.BlockSpec(shape, lambda bi, i: (0,) * len(shape))
    return pl.pallas_call(
        _norm_matmul_kernel,
        grid=(b, tp // tm),
        in_specs=[pl.BlockSpec((1, tm, d), lambda bi, i: (bi, i, 0)),
                  const((1, d)), const((d, n)), const((CONV_WIDTH, SUBLANES, wc)), const((SUBLANES, wc))],
        out_specs=pl.BlockSpec((1, n // LANES, tm, LANES), lambda bi, i: (bi, 0, i, 0)),
        out_shape=jax.ShapeDtypeStruct((b, n // LANES, tp, LANES), F32),
        scratch_shapes=[pltpu.VMEM((wc // LANES, SUBLANES + tm, LANES), F32),
                        pltpu.VMEM((wc // LANES, SUBLANES, LANES), F32)],
        compiler_params=_cparams(2),
        name="norm_matmul",
    )(h, nw.reshape(1, d), w, cw8, cb8)


def _inproj_conv_kernel(x_ref, nw_ref, w_ref, cw_ref, cb_ref, o_ref, xn_ref, rs, hist):
    i, j = pl.program_id(1), pl.program_id(2)
    tm = x_ref.shape[1]
    n_slab = o_ref.shape[1]
    hrows = SUBLANES
    pack = 2 * SUBLANES

    @pl.when(j == 0)
    def _():
        xn_ref[...] = _rmsnorm_rows(x_ref[0], nw_ref[...]).astype(BF16)

    @pl.when(i == 0)
    def _():
        hist[j] = jnp.zeros(hist.shape[1:], F32)

    xn = xn_ref[...]
    proj = lambda c: jnp.dot(xn, w_ref[:, 2 * LANES * c:2 * LANES * (c + 1)], preferred_element_type=F32)
    r_next = proj(0)
    for c in range(n_slab // 2):
        r = r_next
        if c + 1 < n_slab // 2:
            r_next = proj(c + 1)
        for k in range(2):
            s = 2 * c + k
            sl = slice(LANES * s, LANES * (s + 1))
            rs[s, 0:hrows, :] = hist[j, s]
            rs[s, hrows:hrows + tm, :] = r[:, LANES * k:LANES * (k + 1)]
            hist[j, s] = rs[s, tm:tm + hrows, :]
            taps = [cw_ref[m, :, sl] for m in range(CONV_WIDTH)]
            bias = cb_ref[:, sl]
            for g in range(tm // pack):
                halves = []
                for hh in range(pack // SUBLANES):
                    r0 = hrows + pack * g + SUBLANES * hh
                    acc = bias
                    for m in range(CONV_WIDTH):
                        acc = acc + taps[m] * rs[s, r0 - m:r0 - m + SUBLANES, :]
                    halves.append(acc)
                o_ref[0, s, pack * g:pack * (g + 1), :] = _silu(jnp.concatenate(halves, axis=0).astype(BF16))


def inproj_conv(h, nw, w, cw, cb, *, tm, tn):
    b, tp, d = h.shape
    n = w.shape[1]
    ns = tn // LANES
    cw8 = jnp.broadcast_to(jnp.stack([cw[CONV_WIDTH - 1 - m] for m in range(CONV_WIDTH)])[:, None, :],
                           (CONV_WIDTH, SUBLANES, n))
    cb8 = jnp.broadcast_to(cb.reshape(1, n), (SUBLANES, n))
    return pl.pallas_call(
        _inproj_conv_kernel,
        grid=(b, tp // tm, n // tn),
        in_specs=[pl.BlockSpec((1, tm, d), lambda bi, i, j: (bi, i, 0)),
                  pl.BlockSpec((1, d), lambda bi, i, j: (0, 0)),
                  pl.BlockSpec((d, tn), lambda bi, i, j: (0, j)),
                  pl.BlockSpec((CONV_WIDTH, SUBLANES, tn), lambda bi, i, j: (0, 0, j)),
                  pl.BlockSpec((SUBLANES, tn), lambda bi, i, j: (0, j))],
        out_specs=pl.BlockSpec((1, ns, tm, LANES), lambda bi, i, j: (bi, j, i, 0)),
        out_shape=jax.ShapeDtypeStruct((b, n // LANES, tp, LANES), BF16),
        scratch_shapes=[pltpu.VMEM((tm, d), BF16),
                        pltpu.VMEM((ns, SUBLANES + tm, LANES), F32),
                        pltpu.VMEM((n // tn, ns, SUBLANES, LANES), F32)],
        compiler_params=_cparams(3),
        name="inproj_conv",
    )(h, nw.reshape(1, d), w, cw8, cb8)


def _ssd_tables(dt_raw, dtb, a, tri):
    L = SSD_CHUNK
    dt = _softplus(dt_raw + dtb)
    da = dt * a
    da_hi = da.astype(BF16)
    r1 = da - da_hi.astype(F32)
    da_mid = r1.astype(BF16)
    da_lo = (r1 - da_mid.astype(F32)).astype(BF16)
    cs = (jnp.dot(tri, da_hi, preferred_element_type=F32)
          + jnp.dot(tri, da_mid, preferred_element_type=F32)
          + jnp.dot(tri, da_lo, preferred_element_type=F32))
    dt_t = dt.T
    cs_t = cs.T
    last_t = jnp.broadcast_to(cs_t[:, L - 1:L], (L, L))
    return cs * LOG2_E, (cs_t - jnp.log(dt_t)) * LOG2_E, dt_t * jnp.exp(last_t - cs_t), jnp.exp(last_t)


def _inproj_zdt_kernel(x_ref, nw_ref, w_ref, dtb_ref, alog_ref, z_ref, cs_ref, rowp_ref, wend_ref, cdec_ref):
    x = x_ref[0]
    rs = lax.rsqrt(jnp.mean(x * x, axis=-1, keepdims=True) + NORM_EPS)
    xb = (x * nw_ref[...]).astype(BF16)
    n_z = z_ref.shape[1]
    dt_raw = jnp.dot(xb, w_ref[:, LANES * n_z:LANES * (n_z + 1)], preferred_element_type=F32) * rs
    L = SSD_CHUNK
    tri = (lax.broadcasted_iota(jnp.int32, (L, L), 0) >= lax.broadcasted_iota(jnp.int32, (L, L), 1)).astype(BF16)
    a = -jnp.exp(alog_ref[...])
    step = 4
    n_chunks = cs_ref.shape[1]
    for c in range(n_z // step):
        r = jnp.dot(xb, w_ref[:, LANES * step * c:LANES * step * (c + 1)], preferred_element_type=F32) * rs
        for k in range(step):
            z_ref[0, step * c + k] = r[:, LANES * k:LANES * (k + 1)].astype(BF16)
        for q in range(c * n_chunks // (n_z // step), (c + 1) * n_chunks // (n_z // step)):
            cs_ref[0, q], rowp_ref[0, q], wend_ref[0, q], cdec_ref[0, q] = _ssd_tables(
                dt_raw[L * q:L * (q + 1)], dtb_ref[...], a, tri)


def inproj_zdt(h, nw, w, dtb, alog, *, tm):
    b, tp, d = h.shape
    n_z = w.shape[1] // LANES - 1
    L = SSD_CHUNK
    const = lambda shape: pl.BlockSpec(shape, lambda bi, i: (0,) * len(shape))
    tab = pl.BlockSpec((1, tm // L, L, LANES), lambda bi, i: (bi, i, 0, 0))
    return pl.pallas_call(
        _inproj_zdt_kernel,
        grid=(b, tp // tm),
        in_specs=[pl.BlockSpec((1, tm, d), lambda bi, i: (bi, i, 0)),
                  const((1, d)), const(w.shape), const((1, LANES)), const((1, LANES))],
        out_specs=[pl.BlockSpec((1, n_z, tm, LANES), lambda bi, i: (bi, 0, i, 0)), tab, tab, tab, tab],
        out_shape=[jax.ShapeDtypeStruct((b, n_z, tp, LANES), BF16)]
        + [jax.ShapeDtypeStruct((b, tp // L, L, LANES), F32)] * 4,
        compiler_params=_cparams(2),
        name="inproj_zdt",
    )(h, nw.reshape(1, d), w, dtb, alog)


def _mlp_kernel(x_ref, nw_ref, wu_ref, wd_ref, fw_ref, o_ref, xn_ref, *, final_norm):
    j = pl.program_id(1)

    @pl.when(j == 0)
    def _():
        x = x_ref[...]
        xn_ref[...] = _rmsnorm_rows(x, nw_ref[...]).astype(BF16)
        o_ref[...] = x

    u = jnp.dot(xn_ref[...], wu_ref[...].astype(BF16), preferred_element_type=F32)
    a = jnp.square(jnp.maximum(u, 0.0)).astype(BF16)
    o_ref[...] += jnp.dot(a, wd_ref[...].astype(BF16), preferred_element_type=F32)

    if final_norm:
        @pl.when(j == pl.num_programs(1) - 1)
        def _():
            o_ref[...] = _rmsnorm_rows(o_ref[...], fw_ref[...])


def mlp_residual(h2, nw, w_up, w_down, fw, *, layer, tm, tf, final_norm):
    n, d = h2.shape
    dff = w_up.shape[2]
    return pl.pallas_call(
        functools.partial(_mlp_kernel, final_norm=final_norm),
        grid=(n // tm, dff // tf),
        in_specs=[pl.BlockSpec((tm, d), lambda i, j: (i, 0)),
                  pl.BlockSpec((1, d), lambda i, j: (0, 0)),
                  pl.BlockSpec((None, d, tf), lambda i, j: (layer, 0, j)),
                  pl.BlockSpec((None, tf, d), lambda i, j: (layer, j, 0)),
                  pl.BlockSpec((1, d), lambda i, j: (0, 0))],
        out_specs=pl.BlockSpec((tm, d), lambda i, j: (i, 0)),
        out_shape=jax.ShapeDtypeStruct((n, d), F32),
        scratch_shapes=[pltpu.VMEM((tm, d), BF16)],
        compiler_params=_cparams(2, vmem=VMEM_LIMIT_LARGE),
        name="mlp_final" if final_norm else "mlp",
    )(h2, nw.reshape(1, d), w_up, w_down, fw.reshape(1, d))


def _mlp_final_kernel(x_ref, nw_ref, wu_ref, wd_ref, fw_ref, o_ref, xn_ref):
    j = pl.program_id(2)

    @pl.when(j == 0)
    def _():
        x = x_ref[0]
        xn_ref[...] = _rmsnorm_rows(x, nw_ref[...]).astype(BF16)
        o_ref[0] = x

    u = jnp.dot(xn_ref[...], wu_ref[...].astype(BF16), preferred_element_type=F32)
    a = jnp.square(jnp.maximum(u, 0.0)).astype(BF16)
    o_ref[0] += jnp.dot(a, wd_ref[...].astype(BF16), preferred_element_type=F32)

    @pl.when(j == pl.num_programs(2) - 1)
    def _():
        o_ref[0] = _rmsnorm_rows(o_ref[0], fw_ref[...])


def mlp_residual_final(h, nw, w_up, w_down, fw, *, layer, seq, tm, tf):
    b, _, d = h.shape
    dff = w_up.shape[2]
    return pl.pallas_call(
        _mlp_final_kernel,
        grid=(b, seq // tm, dff // tf),
        in_specs=[pl.BlockSpec((pl.Element(1), pl.Element(tm), pl.Element(d)),
                               lambda bi, i, j: (bi, pl.multiple_of(N_META + i * tm, SUBLANES), 0)),
                  pl.BlockSpec((1, d), lambda bi, i, j: (0, 0)),
                  pl.BlockSpec((None, d, tf), lambda bi, i, j: (layer, 0, j)),
                  pl.BlockSpec((None, tf, d), lambda bi, i, j: (layer, j, 0)),
                  pl.BlockSpec((1, d), lambda bi, i, j: (0, 0))],
        out_specs=pl.BlockSpec((1, tm, d), lambda bi, i, j: (bi, i, 0)),
        out_shape=jax.ShapeDtypeStruct((b, seq, d), F32),
        scratch_shapes=[pltpu.VMEM((tm, d), BF16)],
        compiler_params=_cparams(3, vmem=VMEM_LIMIT_LARGE),
        name="mlp_final",
    )(h, nw.reshape(1, d), w_up, w_down, fw.reshape(1, d))


def _outproj_slab_kernel(h_ref, ya_ref, yb_ref, w_ref, o_ref):
    parts = ([ya_ref[0, s].astype(BF16) for s in range(ya_ref.shape[1])]
             + [yb_ref[0, s].astype(BF16) for s in range(yb_ref.shape[1])])
    y = jnp.concatenate(parts, axis=-1)
    o_ref[0] = h_ref[0] + jnp.dot(y, w_ref[...], preferred_element_type=F32)


def outproj_slab(h, ya, yb, w, *, tm):
    b, tp, d = h.shape
    sa, sb = ya.shape[1], yb.shape[1]
    return pl.pallas_call(
        _outproj_slab_kernel,
        grid=(b, tp // tm),
        in_specs=[pl.BlockSpec((1, tm, d), lambda bi, i: (bi, i, 0)),
                  pl.BlockSpec((1, sa, tm, LANES), lambda bi, i: (bi, 0, i, 0)),
                  pl.BlockSpec((1, sb, tm, LANES), lambda bi, i: (bi, 0, i, 0)),
                  pl.BlockSpec(w.shape, lambda bi, i: (0, 0))],
        out_specs=pl.BlockSpec((1, tm, d), lambda bi, i: (bi, i, 0)),
        out_shape=jax.ShapeDtypeStruct(h.shape, F32),
        compiler_params=_cparams(2),
        name="outproj_slab",
    )(h, ya, yb, w)


def _gated_outproj_kernel(h_ref, y_ref, z_ref, nw_ref, w_ref, o_ref, *, group):
    acc = h_ref[0]
    spg = group // LANES
    for g in range(y_ref.shape[2] // group):
        cols = slice(group * g, group * (g + 1))
        zg = jnp.concatenate([z_ref[0, spg * g + k] for k in range(spg)], axis=-1).astype(F32)
        gg = y_ref[0, :, cols].astype(F32) * _silu(zg)
        ms = jnp.mean(gg * gg, axis=-1, keepdims=True)
        part = (gg * lax.rsqrt(ms + NORM_EPS) * nw_ref[:, cols]).astype(BF16)
        acc = acc + jnp.dot(part, w_ref[cols, :], preferred_element_type=F32)
    o_ref[0] = acc


def gated_outproj(h, y, z, nw, w, *, tm, group):
    b, tp, d = h.shape
    k = y.shape[-1]
    return pl.pallas_call(
        functools.partial(_gated_outproj_kernel, group=group),
        grid=(b, tp // tm),
        in_specs=[pl.BlockSpec((1, tm, d), lambda bi, i: (bi, i, 0)),
                  pl.BlockSpec((1, tm, k), lambda bi, i: (bi, i, 0)),
                  pl.BlockSpec((1, k // LANES, tm, LANES), lambda bi, i: (bi, 0, i, 0)),
                  pl.BlockSpec((1, k), lambda bi, i: (0, 0)),
                  pl.BlockSpec(w.shape, lambda bi, i: (0, 0))],
        out_specs=pl.BlockSpec((1, tm, d), lambda bi, i: (bi, i, 0)),
        out_shape=jax.ShapeDtypeStruct(h.shape, F32),
        compiler_params=_cparams(2),
        name="gated_outproj",
    )(h, y, z, nw, w)


def _rglru_kernel(x_ref, g_ref, wa_ref, ba_ref, wx_ref, bx_ref, lam_ref, o_ref,
                  xcp, a_s, b_s, hn, carry, *, ls):
    n_slab = x_ref.shape[1]
    slabs = [slice(LANES * s, LANES * (s + 1)) for s in range(n_slab)]

    @pl.when(pl.program_id(1) == 0)
    def _():
        carry[...] = jnp.zeros(carry.shape, F32)

    def gather_body(tau, c):
        for s in range(n_slab):
            xcp[_rows(tau), slabs[s]] = x_ref[0, s, pl.ds(tau, SUBLANES, stride=ls), :]
        return c

    lax.fori_loop(0, ls, gather_body, 0, unroll=2)

    xc = xcp[...]
    xb = xc.astype(BF16)
    r = _sigmoid(jnp.dot(xb, wa_ref[...], preferred_element_type=F32) + ba_ref[...])
    i = _sigmoid(jnp.dot(xb, wx_ref[...], preferred_element_type=F32) + bx_ref[...])
    log_a = (-LRU_C) * r * _softplus(-lam_ref[...])
    a = jnp.exp(log_a)
    a_s[...] = a
    v = -jnp.tanh(log_a) * (a * a + 1.0)
    b_s[...] = jnp.where(v > 0.0, v * lax.rsqrt(v), 0.0) * (i * xc)

    def pass1(tau, c):
        out = []
        for s in range(n_slab):
            p, e = c[2 * s], c[2 * s + 1]
            av = a_s[_rows(tau), slabs[s]]
            out += [p * av, av * e + b_s[_rows(tau), slabs[s]]]
        return tuple(out)

    one = jnp.ones((SUBLANES, LANES), F32)
    zero = jnp.zeros((SUBLANES, LANES), F32)
    pe = lax.fori_loop(0, ls, pass1, (one, zero) * n_slab, unroll=2)

    h0 = []
    for s in range(n_slab):
        p, e = pe[2 * s], pe[2 * s + 1]
        c = carry[0:1, slabs[s]]
        rows = []
        for j in range(SUBLANES):
            rows.append(c)
            c = p[j:j + 1] * c + e[j:j + 1]
        carry[0:1, slabs[s]] = c
        h0.append(jnp.concatenate(rows, axis=0))

    def pass2(tau, hs):
        out = []
        for s in range(n_slab):
            h = a_s[_rows(tau), slabs[s]] * hs[s] + b_s[_rows(tau), slabs[s]]
            hn[s, pl.ds(tau, SUBLANES, stride=ls), :] = h
            out.append(h)
        return tuple(out)

    lax.fori_loop(0, ls, pass2, tuple(h0), unroll=2)
    for s in range(n_slab):
        o_ref[0, s] = (hn[s] * _gelu_tanh(g_ref[0, s])).astype(o_ref.dtype)


def rglru(proj, wa_bd, ba, wx_bd, bx, lam, *, ls):
    b, _, tp, _ = proj.shape
    w = lam.shape[0]
    ns = w // LANES
    tt = SUBLANES * ls
    const = lambda shape: pl.BlockSpec(shape, lambda bi, t: (0,) * len(shape))
    return pl.pallas_call(
        functools.partial(_rglru_kernel, ls=ls),
        grid=(b, tp // tt),
        in_specs=[pl.BlockSpec((1, ns, tt, LANES), lambda bi, t: (bi, 0, t, 0)),
                  pl.BlockSpec((1, ns, tt, LANES), lambda bi, t: (bi, 1, t, 0)),
                  const((w, w)), const((1, w)), const((w, w)), const((1, w)), const((1, w))],
        out_specs=pl.BlockSpec((1, ns, tt, LANES), lambda bi, t: (bi, 0, t, 0)),
        out_shape=jax.ShapeDtypeStruct((b, ns, tp, LANES), BF16),
        scratch_shapes=[pltpu.VMEM((tt, w), F32), pltpu.VMEM((tt, w), F32), pltpu.VMEM((tt, w), F32),
                        pltpu.VMEM((ns, tt, LANES), F32), pltpu.VMEM((SUBLANES, w), F32)],
        compiler_params=_cparams(2),
        name="rglru",
    )(proj, proj, wa_bd, ba.reshape(1, w), wx_bd, bx.reshape(1, w), lam.reshape(1, w))


def _s5_kernel(u_ref, bend_ref, kc_ref, l8r_ref, l8i_ref, plr_ref, pli_ref, d_ref, wg_ref, bg_ref,
               o_ref, lp, st, yv, carry, *, ls):
    sc = S5_SC
    n_cs = l8r_ref.shape[1]
    stride = sc * ls
    cre = [slice(LANES * k, LANES * (k + 1)) for k in range(n_cs)]
    cim = [slice(LANES * (n_cs + k), LANES * (n_cs + k + 1)) for k in range(n_cs)]
    lanes = [slice(LANES * q, LANES * (q + 1)) for q in range(sc)]

    @pl.when(pl.program_id(2) == 0)
    def _():
        carry[...] = jnp.zeros(carry.shape, F32)

    def gather_body(tau, c):
        for sg in range(sc):
            lp[_rows(tau), lanes[sg]] = u_ref[0, 0, pl.ds(tau * sc + sg, SUBLANES, stride=stride), :]
        return c

    lax.fori_loop(0, ls, gather_body, 0)
    st[...] = jnp.dot(lp[...].astype(BF16), bend_ref[0], preferred_element_type=F32)

    lam = [(jnp.broadcast_to(l8r_ref[0, k], (SUBLANES, LANES)),
            jnp.broadcast_to(l8i_ref[0, k], (SUBLANES, LANES))) for k in range(n_cs)]

    def step(tau, k, sr, si):
        lr, li = lam[k]
        return (lr * sr - li * si + st[_rows(tau), cre[k]], lr * si + li * sr + st[_rows(tau), cim[k]])

    def pass1(tau, c):
        out = []
        for k in range(n_cs):
            out += list(step(tau, k, c[2 * k], c[2 * k + 1]))
        return tuple(out)

    zero = jnp.zeros((SUBLANES, LANES), F32)
    ends = lax.fori_loop(0, ls, pass1, (zero,) * (2 * n_cs))

    starts = []
    for k in range(n_cs):
        er, ei = ends[2 * k], ends[2 * k + 1]
        pr, pi = plr_ref[0, k], pli_ref[0, k]
        c_r, c_i = carry[0:1, cre[k]], carry[0:1, cim[k]]
        rows_r, rows_i = [], []
        for j in range(SUBLANES):
            rows_r.append(c_r)
            rows_i.append(c_i)
            c_r, c_i = (pr * c_r - pi * c_i + er[j:j + 1], pr * c_i + pi * c_r + ei[j:j + 1])
        carry[0:1, cre[k]] = c_r
        carry[0:1, cim[k]] = c_i
        starts += [jnp.concatenate(rows_r, axis=0), jnp.concatenate(rows_i, axis=0)]

    def pass2(tau, c):
        out = []
        for k in range(n_cs):
            nr, ni = step(tau, k, c[2 * k], c[2 * k + 1])
            st[_rows(tau), cre[k]] = c[2 * k]
            st[_rows(tau), cim[k]] = c[2 * k + 1]
            out += [nr, ni]
        return tuple(out)

    lax.fori_loop(0, ls, pass2, tuple(starts))

    kin = lp.shape[1]
    lp_b, st_b = lp[...].astype(BF16), st[...].astype(BF16)
    tile = 2 * LANES
    for tcol in range(kin // tile):
        cols = slice(tile * tcol, tile * (tcol + 1))
        kk = tile * (tcol + 1)
        yv[:, cols] = (jnp.dot(lp_b[:, :kk], kc_ref[0, 0:kk, cols], preferred_element_type=F32)
                       + jnp.dot(st_b, kc_ref[0, kin:, cols], preferred_element_type=F32))
    for q in range(sc):
        y = yv[:, lanes[q]] + d_ref[0] * lp[:, lanes[q]]
        y = _gelu_tanh(y)
        yv[:, lanes[q]] = y * _sigmoid(jnp.dot(y.astype(BF16), wg_ref[0], preferred_element_type=F32)
                                       + bg_ref[0])

    def scatter_body(tau, c):
        for sg in range(sc):
            o_ref[0, 0, pl.ds(tau * sc + sg, SUBLANES, stride=stride), :] = yv[_rows(tau), lanes[sg]]
        return c

    lax.fori_loop(0, ls, scatter_body, 0)


def s5(proj, first_slab, bend, kc, l8r, l8i, plr, pli, d, wg, bg, *, ls):
    b, _, tp, _ = proj.shape
    nblk, kin, nst = bend.shape
    n_cs = nst // (2 * LANES)
    tt = SUBLANES * ls * S5_SC
    nc = SUBLANES * ls
    per_blk = lambda shape: pl.BlockSpec((1,) + shape, lambda bi, gb, t: (gb,) + (0,) * len(shape))
    return pl.pallas_call(
        functools.partial(_s5_kernel, ls=ls),
        grid=(b, nblk, tp // tt),
        in_specs=[pl.BlockSpec((1, 1, tt, LANES), lambda bi, gb, t: (bi, first_slab + gb, t, 0)),
                  per_blk((kin, nst)), per_blk((kin + nst, kin)),
                  per_blk((n_cs, 1, LANES)), per_blk((n_cs, 1, LANES)),
                  per_blk((n_cs, 1, LANES)), per_blk((n_cs, 1, LANES)),
                  per_blk((1, LANES)), per_blk((LANES, LANES)), per_blk((1, LANES))],
        out_specs=pl.BlockSpec((1, 1, tt, LANES), lambda bi, gb, t: (bi, gb, t, 0)),
        out_shape=jax.ShapeDtypeStruct((b, nblk, tp, LANES), F32),
        scratch_shapes=[pltpu.VMEM((nc, kin), F32), pltpu.VMEM((nc, nst), F32),
                        pltpu.VMEM((nc, kin), F32), pltpu.VMEM((SUBLANES, nst), F32)],
        compiler_params=_cparams(3),
        name="s5",
    )(proj, bend, kc, l8r, l8i, plr, pli, d, wg, bg)


def _s5_params(a_re, a_im, b_re, b_im, c_re, c_im, d, log_dt, w_glu, b_glu, *, ls):
    g, p = a_re.shape
    sc = S5_SC
    gpb = LANES // S5_GROUP
    nblk = g // gpb
    dt = jnp.exp(log_dt)[:, None]
    mag = jnp.exp(a_re * dt)
    ar, ai = mag * jnp.cos(a_im * dt), mag * jnp.sin(a_im * dt)
    den = a_re * a_re + a_im * a_im
    fr = ((ar - 1.0) * a_re + ai * a_im) / den
    fi = (ai * a_re - (ar - 1.0) * a_im) / den
    bbar_re = fr[..., None] * b_re - fi[..., None] * b_im
    bbar_im = fr[..., None] * b_im + fi[..., None] * b_re

    def lam_pow(k):
        k = jnp.asarray(k, F32).reshape(-1, 1, 1)
        m = jnp.exp(a_re * dt * k)
        return m * jnp.cos(a_im * dt * k), m * jnp.sin(a_im * dt * k)

    pw_r, pw_i = lam_pow(jnp.arange(sc + 1))
    def group_diag(dense, rows_per_group):
        n = dense.shape[-1]
        tiled = jnp.tile(dense, (1,) * (dense.ndim - 1) + (gpb,))
        rg = lax.broadcasted_iota(jnp.int32, tiled.shape[-2:], 0) // rows_per_group
        cg = lax.broadcasted_iota(jnp.int32, tiled.shape[-2:], 1) // n
        return jnp.where(rg == cg, tiled, 0.0)

    def by_block(m):
        k, _, a, b2 = m.shape
        return jnp.transpose(m.reshape(k, nblk, gpb, a, b2), (0, 1, 2, 4, 3)).reshape(k, nblk, gpb * b2, a)

    wr = jnp.stack([pw_r[sc - 1 - s] for s in range(sc)])[..., None]
    wi = jnp.stack([pw_i[sc - 1 - s] for s in range(sc)])[..., None]
    e_re = wr * bbar_re - wi * bbar_im
    e_im = wr * bbar_im + wi * bbar_re
    rows_cat = lambda m: jnp.concatenate([m[k] for k in range(m.shape[0])], axis=-2)
    bend = jnp.concatenate([rows_cat(group_diag(by_block(e_re), S5_GROUP)),
                            rows_cat(group_diag(by_block(e_im), S5_GROUP))], axis=-1).astype(BF16)

    cl_re = c_re[None] * pw_r[:, :, None, :] - c_im[None] * pw_i[:, :, None, :]
    cl_im = c_re[None] * pw_i[:, :, None, :] + c_im[None] * pw_r[:, :, None, :]
    kl = (jnp.einsum('kgip,gpj->kgij', cl_re[:sc], bbar_re)
          - jnp.einsum('kgip,gpj->kgij', cl_im[:sc], bbar_im))
    kd = group_diag(by_block(kl), S5_GROUP)
    zero_blk = jnp.zeros_like(kd[0])
    kintra = jnp.concatenate(
        [jnp.concatenate([kd[t - s] if t >= s else zero_blk for t in range(sc)], axis=-1)
         for s in range(sc)], axis=-2)

    def out_bd(m):
        dense = jnp.transpose(m, (0, 1, 3, 2))
        d2 = group_diag(dense.reshape(sc, nblk, gpb * p, S5_GROUP), p)
        return jnp.concatenate([d2[t] for t in range(sc)], axis=-1)

    kc = jnp.concatenate([kintra, out_bd(cl_re[1:]), -out_bd(cl_im[1:])], axis=-2).astype(BF16)

    vec = lambda v: v.reshape(nblk, (gpb * p) // LANES, 1, LANES)
    pl_r, pl_i = lam_pow(jnp.asarray([sc * ls]))
    wg = group_diag(w_glu.reshape(nblk, LANES, S5_GROUP), S5_GROUP).astype(BF16)
    return (bend, kc, vec(pw_r[sc]), vec(pw_i[sc]), vec(pl_r[0]), vec(pl_i[0]),
            d.reshape(nblk, 1, LANES), wg, b_glu.reshape(nblk, 1, LANES))


def _ssd_kernel(xbc_ref, cs_ref, rowp_ref, wend_ref, cdec_ref, dskip_ref, o_ref, hst):
    @pl.when(pl.program_id(1) == 0)
    def _():
        hst[...] = jnp.zeros(hst.shape, F32)

    for c in range(xbc_ref.shape[2] // SSD_CHUNK):
        _ssd_chunk(xbc_ref, cs_ref, rowp_ref, wend_ref, cdec_ref, dskip_ref, o_ref, hst, c)


def _ssd_chunk(xbc_ref, cs_ref, rowp_ref, wend_ref, cdec_ref, dskip_ref, o_ref, hst, c):
    L = SSD_CHUNK
    rs = slice(L * c, L * (c + 1))
    n_groups = hst.shape[0]
    gw = SSD_HPG * SSD_HEAD_DIM
    spg = gw // LANES
    n_xs = n_groups * spg
    row = lax.broadcasted_iota(jnp.int32, (L, L), 0)
    col = lax.broadcasted_iota(jnp.int32, (L, L), 1)
    causal = row >= col
    left_half = col < SSD_HEAD_DIM
    head_of_lane = lax.broadcasted_iota(jnp.int32, (L, gw), 1) // SSD_HEAD_DIM

    def head_rows(ref, g):
        return jnp.concatenate(
            [jnp.broadcast_to(ref[0, c, SSD_HPG * g + r:SSD_HPG * g + r + 1, :], (SSD_HEAD_DIM, L))
             for r in range(SSD_HPG)], axis=0)

    def x_slabs(g):
        return jnp.concatenate([xbc_ref[0, spg * g + k, rs, :] for k in range(spg)], axis=-1)

    scores, y_offs = [], []
    for g in range(n_groups):
        bm = xbc_ref[0, n_xs + g, rs, :]
        cm = xbc_ref[0, n_xs + n_groups + g, rs, :]
        scores.append(lax.dot_general(cm, bm, (((1,), (1,)), ((), ())), preferred_element_type=F32))
        h_prev = hst[g]
        y_offs.append(lax.dot_general(cm, h_prev.astype(BF16), (((1,), (1,)), ((), ())),
                                      preferred_element_type=F32))
        xw_t = (x_slabs(g).astype(F32).T * head_rows(wend_ref, g)).astype(BF16)
        hst[g] = head_rows(cdec_ref, g) * h_prev + jnp.dot(xw_t, bm, preferred_element_type=F32)

    for g in range(n_groups):
        xsb = x_slabs(g)
        ms, bcs = [], []
        for r in range(SSD_HPG):
            h = SSD_HPG * g + r
            bcs.append(jnp.broadcast_to(cs_ref[0, c, :, h:h + 1], (L, L)))
            ms.append((scores[g] * jnp.exp2(jnp.where(causal, bcs[r] - rowp_ref[0, c, h:h + 1, :], -jnp.inf))
                       ).astype(BF16))
        x_bd = jnp.concatenate([jnp.where(head_of_lane == r, xsb, jnp.zeros_like(xsb))
                                for r in range(SSD_HPG)], axis=0)
        y_diag = jnp.dot(jnp.concatenate(ms, axis=1), x_bd, preferred_element_type=F32)
        f_start = jnp.concatenate([jnp.exp2(jnp.where(left_half, bcs[2 * k], bcs[2 * k + 1]))
                                   for k in range(spg)], axis=-1)
        o_ref[0, rs, gw * g:gw * (g + 1)] = (
            y_diag + y_offs[g] * f_start + dskip_ref[:, gw * g:gw * (g + 1)] * xsb.astype(F32)
        ).astype(o_ref.dtype)


def ssd_core(xbc, tables, dskip):
    b, n_conv, tp, _ = xbc.shape
    d_inner = dskip.shape[1]
    n_groups = d_inner // (SSD_HPG * SSD_HEAD_DIM)
    cps = SSD_CHUNKS_PER_STEP
    L = SSD_CHUNK * cps
    const = lambda shape: pl.BlockSpec(shape, lambda bi, c: (0,) * len(shape))
    tab = pl.BlockSpec((1, cps, SSD_CHUNK, LANES), lambda bi, c: (bi, c, 0, 0))
    return pl.pallas_call(
        _ssd_kernel,
        grid=(b, tp // L),
        in_specs=[pl.BlockSpec((1, n_conv, L, LANES), lambda bi, c: (bi, 0, c, 0)),
                  tab, tab, tab, tab, const((1, d_inner))],
        out_specs=pl.BlockSpec((1, L, d_inner), lambda bi, c: (bi, c, 0)),
        out_shape=jax.ShapeDtypeStruct((b, tp, d_inner), BF16),
        scratch_shapes=[pltpu.VMEM((n_groups, SSD_HPG * SSD_HEAD_DIM, SSD_STATE), F32)],
        compiler_params=_cparams(2),
        name="ssd_core",
    )(xbc, *tables, dskip)


def _block_diag(w):
    n, di, do = w.shape
    return jnp.einsum('gij,gh->gihj', w, jnp.eye(n, dtype=w.dtype)).reshape(n * di, n * do)


def _pad_lanes(v, fill=0.0):
    return jnp.pad(v, (0, LANES - v.shape[0]), constant_values=fill).reshape(1, LANES)


def kernel(x, meta_tokens, norm_mix, norm_mlp, norm_final, ev_w_in, lru_conv_w, lru_conv_b, lru_w_a, lru_b_a, lru_w_x, lru_b_x, lru_lambda, s5_a_re, s5_a_im, s5_b_re, s5_b_im, s5_c_re, s5_c_im, s5_d, s5_log_dt, s5_w_glu, s5_b_glu, ev_w_out, ssd_w_in, ssd_conv_w, ssd_conv_b, ssd_dt_bias, ssd_a_log, ssd_d, ssd_norm, ssd_w_out, mlp_w_up, mlp_w_down):
    bsz, seq, d = x.shape
    t = seq + N_META
    unit = math.lcm(SUBLANES * LRU_LS, SUBLANES * S5_LS * S5_SC, ROW_TILE, SSD_CHUNK)
    tp = -(-t // unit) * unit
    meta = jnp.broadcast_to(meta_tokens[None].astype(x.dtype), (bsz, N_META, d))
    h = jnp.concatenate([meta, x, jnp.zeros((bsz, tp - t, d), x.dtype)], axis=1)

    lru_w = lru_conv_w.shape[-1]
    proj = norm_matmul(h, norm_mix[0], ev_w_in[0].astype(BF16), lru_conv_w[0], lru_conv_b[0], tm=ROW_TILE)
    y_lru = rglru(proj, _block_diag(lru_w_a[0]).astype(BF16), lru_b_a[0],
                  _block_diag(lru_w_x[0]).astype(BF16), lru_b_x[0], lru_lambda[0], ls=LRU_LS)
    y_s5 = s5(proj, 2 * lru_w // LANES,
              *_s5_params(s5_a_re[0], s5_a_im[0], s5_b_re[0], s5_b_im[0], s5_c_re[0], s5_c_im[0],
                          s5_d[0], s5_log_dt[0], s5_w_glu[0], s5_b_glu[0], ls=S5_LS), ls=S5_LS)
    h = outproj_slab(h, y_lru, y_s5, ev_w_out[0].astype(BF16), tm=ROW_TILE)
    h = mlp_residual(h.reshape(bsz * tp, d), norm_mlp[0], mlp_w_up, mlp_w_down,
                     norm_final, layer=0, tm=MLP_ROW_TILE, tf=512,
                     final_norm=False).reshape(bsz, tp, d)

    d_inner = ssd_w_out.shape[1]
    conv_dim = ssd_conv_w.shape[-1]
    n_heads = ssd_dt_bias.shape[-1]
    w_in = ssd_w_in[0]
    w_zdt = jnp.concatenate([w_in[:, :d_inner], w_in[:, d_inner + conv_dim:],
                             jnp.zeros((d, LANES - n_heads), F32)], axis=1).astype(BF16)
    xbc = inproj_conv(h, norm_mix[1], w_in[:, d_inner:d_inner + conv_dim].astype(BF16),
                      ssd_conv_w[0], ssd_conv_b[0], tm=ROW_TILE, tn=2048)
    z, *tables = inproj_zdt(h, norm_mix[1], w_zdt, _pad_lanes(ssd_dt_bias[0]), _pad_lanes(ssd_a_log[0]),
                            tm=ROW_TILE)
    y = ssd_core(xbc, tables, jnp.repeat(ssd_d[0], SSD_HEAD_DIM).reshape(1, d_inner))
    h = gated_outproj(h, y, z, ssd_norm[0].reshape(1, d_inner), ssd_w_out[0].astype(BF16),
                      tm=GATED_ROW_TILE, group=SSD_HPG * SSD_HEAD_DIM)
    return mlp_residual_final(h, norm_mlp[1], mlp_w_up, mlp_w_down,
                              norm_final, layer=1, seq=seq, tm=FINAL_ROW_TILE, tf=512)
```

```python
import functools
import math

import jax
import jax.numpy as jnp
from jax import lax
from jax.experimental import pallas as pl
from jax.experimental.pallas import tpu as pltpu

F32 = jnp.float32
BF16 = jnp.bfloat16

LANES = 128
SUBLANES = 8
NORM_EPS = 1e-5
LOG2_E = 1.4426950408889634
N_META = 16
CONV_WIDTH = 4
LRU_C = 8.0
S5_GROUP = 16
S5_STATE = 64
SSD_HEAD_DIM = 64
SSD_STATE = 128
SSD_CHUNK = 128
SSD_HPG = 4
SSD_CHUNKS_PER_STEP = 5

LRU_LS = 130
S5_SC = SUBLANES
S5_LS = 65
ROW_TILE = 1280
MLP_ROW_TILE = 2080
GATED_ROW_TILE = 640
FINAL_ROW_TILE = 2048
VMEM_LIMIT = 48 * 1024 * 1024
VMEM_LIMIT_LARGE = 58 * 1024 * 1024


def _cparams(n_axes, vmem=VMEM_LIMIT):
    return pltpu.CompilerParams(dimension_semantics=("arbitrary",) * n_axes,
                                vmem_limit_bytes=vmem)


def _sigmoid(x):
    return 0.5 + 0.5 * jnp.tanh(0.5 * x)


def _silu(x):
    hx = 0.5 * x
    return hx + hx * jnp.tanh(hx)


def _gelu_tanh(x):
    return 0.5 * x * (1.0 + jnp.tanh(math.sqrt(2.0 / math.pi) * (x + 0.044715 * (x * x * x))))


def _log1p(e):
    u = 1.0 + e
    return jnp.where(u == 1.0, e, jnp.log(u) * (e / (u - 1.0)))


def _softplus(x):
    return jnp.maximum(x, 0.0) + _log1p(jnp.exp(-jnp.abs(x)))


def _rmsnorm_rows(x, w):
    ms = jnp.mean(x * x, axis=-1, keepdims=True)
    return x * lax.rsqrt(ms + NORM_EPS) * w


def _rows(tau):
    return pl.ds(pl.multiple_of(tau * SUBLANES, SUBLANES), SUBLANES)


def _seq_specs(seq, tm, d):
    x_spec = pl.BlockSpec(
        (pl.Element(1), pl.Element(tm), pl.Element(d)),
        lambda bi, i: (bi, pl.multiple_of(jnp.clip(i * tm - N_META, 0, seq - tm), SUBLANES), 0))
    edge_spec = pl.BlockSpec((1, tm, d), lambda bi, i: (bi, 0, 0), pipeline_mode=pl.Buffered(1))
    return [x_spec, edge_spec, edge_spec]


def _seq_tile(x_ref, first_ref, last_ref):
    i = pl.program_id(1)
    return jnp.where(i == 0, first_ref[0], jnp.where(i == pl.num_programs(1) - 1, last_ref[0], x_ref[0]))


def _norm_matmul_kernel(x_ref, first_ref, last_ref, nw_ref, w_ref, cw_ref, cb_ref, o_ref, rbuf, hist):
    tm = x_ref.shape[1]
    n_conv = rbuf.shape[0]
    hrows = SUBLANES

    @pl.when(pl.program_id(1) == 0)
    def _():
        hist[...] = jnp.zeros(hist.shape, F32)

    x = _seq_tile(x_ref, first_ref, last_ref)
    rs = lax.rsqrt(jnp.mean(x * x, axis=-1, keepdims=True) + NORM_EPS)
    xb = (x * nw_ref[...]).astype(BF16)
    step = 4
    for c in range(o_ref.shape[1] // step):
        r = jnp.dot(xb, w_ref[:, LANES * step * c:LANES * step * (c + 1)], preferred_element_type=F32) * rs
        for k in range(step):
            s = step * c + k
            if s >= n_conv:
                o_ref[0, s] = r[:, LANES * k:LANES * (k + 1)]
                continue
            sl = slice(LANES * s, LANES * (s + 1))
            rbuf[s, 0:hrows, :] = hist[s]
            rbuf[s, hrows:hrows + tm, :] = r[:, LANES * k:LANES * (k + 1)]
            hist[s] = rbuf[s, tm:tm + hrows, :]
            taps = [cw_ref[m, :, sl] for m in range(CONV_WIDTH)]
            bias = cb_ref[:, sl]
            for g in range(tm // SUBLANES):
                r0 = hrows + SUBLANES * g
                acc = bias
                for m in range(CONV_WIDTH):
                    acc = acc + taps[m] * rbuf[s, r0 - m:r0 - m + SUBLANES, :]
                o_ref[0, s, SUBLANES * g:SUBLANES * (g + 1), :] = acc


def norm_matmul(x, first, last, nw, w, cw, cb, *, tp, tm):
    b, seq, d = x.shape
    n = w.shape[1]
    wc = cw.shape[1]
    cw8 = jnp.broadcast_to(jnp.stack([cw[CONV_WIDTH - 1 - m] for m in range(CONV_WIDTH)])[:, None, :],
                           (CONV_WIDTH, SUBLANES, wc))
    cb8 = jnp.broadcast_to(cb[None, :], (SUBLANES, wc))
    const = lambda shape: pl.BlockSpec(shape, lambda bi, i: (0,) * len(shape))
    return pl.pallas_call(
        _norm_matmul_kernel,
        grid=(b, tp // tm),
        in_specs=_seq_specs(seq, tm, d) + [
            const((1, d)), const((d, n)), const((CONV_WIDTH, SUBLANES, wc)), const((SUBLANES, wc))],
        out_specs=pl.BlockSpec((1, n // LANES, tm, LANES), lambda bi, i: (bi, 0, i, 0)),
        out_shape=jax.ShapeDtypeStruct((b, n // LANES, tp, LANES), F32),
        scratch_shapes=[pltpu.VMEM((wc // LANES, SUBLANES + tm, LANES), F32),
                        pltpu.VMEM((wc // LANES, SUBLANES, LANES), F32)],
        compiler_params=_cparams(2),
        name="norm_matmul",
    )(x, first, last, nw.reshape(1, d), w, cw8, cb8)


def _inproj_conv_kernel(x_ref, nw_ref, w_ref, cw_ref, cb_ref, o_ref, xn_ref, rs, hist):
    i, j = pl.program_id(1), pl.program_id(2)
    tm = x_ref.shape[1]
    n_slab = o_ref.shape[1]
    hrows = SUBLANES
    pack = 2 * SUBLANES

    @pl.when(j == 0)
    def _():
        xn_ref[...] = _rmsnorm_rows(x_ref[0], nw_ref[...]).astype(BF16)

    @pl.when(i == 0)
    def _():
        hist[j] = jnp.zeros(hist.shape[1:], F32)

    xn = xn_ref[...]
    proj = lambda c: jnp.dot(xn, w_ref[:, 2 * LANES * c:2 * LANES * (c + 1)], preferred_element_type=F32)
    r_next = proj(0)
    for c in range(n_slab // 2):
        r = r_next
        if c + 1 < n_slab // 2:
            r_next = proj(c + 1)
        for k in range(2):
            s = 2 * c + k
            sl = slice(LANES * s, LANES * (s + 1))
            rs[s, 0:hrows, :] = hist[j, s]
            rs[s, hrows:hrows + tm, :] = r[:, LANES * k:LANES * (k + 1)]
            hist[j, s] = rs[s, tm:tm + hrows, :]
            taps = [cw_ref[m, :, sl] for m in range(CONV_WIDTH)]
            bias = cb_ref[:, sl]
            for g in range(tm // pack):
                halves = []
                for hh in range(pack // SUBLANES):
                    r0 = hrows + pack * g + SUBLANES * hh
                    acc = bias
                    for m in range(CONV_WIDTH):
                        acc = acc + taps[m] * rs[s, r0 - m:r0 - m + SUBLANES, :]
                    halves.append(acc)
                o_ref[0, s, pack * g:pack * (g + 1), :] = _silu(jnp.concatenate(halves, axis=0).astype(BF16))


def inproj_conv(h, nw, w, cw, cb, *, tm, tn):
    b, tp, d = h.shape
    n = w.shape[1]
    ns = tn // LANES
    cw8 = jnp.broadcast_to(jnp.stack([cw[CONV_WIDTH - 1 - m] for m in range(CONV_WIDTH)])[:, None, :],
                           (CONV_WIDTH, SUBLANES, n))
    cb8 = jnp.broadcast_to(cb.reshape(1, n), (SUBLANES, n))
    return pl.pallas_call(
        _inproj_conv_kernel,
        grid=(b, tp // tm, n // tn),
        in_specs=[pl.BlockSpec((1, tm, d), lambda bi, i, j: (bi, i, 0)),
                  pl.BlockSpec((1, d), lambda bi, i, j: (0, 0)),
                  pl.BlockSpec((d, tn), lambda bi, i, j: (0, j)),
                  pl.BlockSpec((CONV_WIDTH, SUBLANES, tn), lambda bi, i, j: (0, 0, j)),
                  pl.BlockSpec((SUBLANES, tn), lambda bi, i, j: (0, j))],
        out_specs=pl.BlockSpec((1, ns, tm, LANES), lambda bi, i, j: (bi, j, i, 0)),
        out_shape=jax.ShapeDtypeStruct((b, n // LANES, tp, LANES), BF16),
        scratch_shapes=[pltpu.VMEM((tm, d), BF16),
                        pltpu.VMEM((ns, SUBLANES + tm, LANES), F32),
                        pltpu.VMEM((n // tn, ns, SUBLANES, LANES), F32)],
        compiler_params=_cparams(3),
        name="inproj_conv",
    )(h, nw.reshape(1, d), w, cw8, cb8)


def _ssd_tables(dt_raw, dtb, a, tri):
    L = SSD_CHUNK
    dt = _softplus(dt_raw + dtb)
    da = dt * a
    da_hi = da.astype(BF16)
    r1 = da - da_hi.astype(F32)
    da_mid = r1.astype(BF16)
    da_lo = (r1 - da_mid.astype(F32)).astype(BF16)
    cs = (jnp.dot(tri, da_hi, preferred_element_type=F32)
          + jnp.dot(tri, da_mid, preferred_element_type=F32)
          + jnp.dot(tri, da_lo, preferred_element_type=F32))
    dt_t = dt.T
    cs_t = cs.T
    last_t = jnp.broadcast_to(cs_t[:, L - 1:L], (L, L))
    return cs * LOG2_E, (cs_t - jnp.log(dt_t)) * LOG2_E, dt_t * jnp.exp(last_t - cs_t), jnp.exp(last_t)


def _inproj_zdt_kernel(x_ref, nw_ref, w_ref, dtb_ref, alog_ref, z_ref, cs_ref, rowp_ref, wend_ref, cdec_ref):
    x = x_ref[0]
    rs = lax.rsqrt(jnp.mean(x * x, axis=-1, keepdims=True) + NORM_EPS)
    xb = (x * nw_ref[...]).astype(BF16)
    n_z = z_ref.shape[1]
    dt_raw = jnp.dot(xb, w_ref[:, LANES * n_z:LANES * (n_z + 1)], preferred_element_type=F32) * rs
    L = SSD_CHUNK
    tri = (lax.broadcasted_iota(jnp.int32, (L, L), 0) >= lax.broadcasted_iota(jnp.int32, (L, L), 1)).astype(BF16)
    a = -jnp.exp(alog_ref[...])
    step = 4
    n_chunks = cs_ref.shape[1]
    for c in range(n_z // step):
        r = jnp.dot(xb, w_ref[:, LANES * step * c:LANES * step * (c + 1)], preferred_element_type=F32) * rs
        for k in range(step):
            z_ref[0, step * c + k] = r[:, LANES * k:LANES * (k + 1)].astype(BF16)
        for q in range(c * n_chunks // (n_z // step), (c + 1) * n_chunks // (n_z // step)):
            cs_ref[0, q], rowp_ref[0, q], wend_ref[0, q], cdec_ref[0, q] = _ssd_tables(
                dt_raw[L * q:L * (q + 1)], dtb_ref[...], a, tri)


def inproj_zdt(h, nw, w, dtb, alog, *, tm):
    b, tp, d = h.shape
    n_z = w.shape[1] // LANES - 1
    L = SSD_CHUNK
    const = lambda shape: pl.BlockSpec(shape, lambda bi, i: (0,) * len(shape))
    tab = pl.BlockSpec((1, tm // L, L, LANES), lambda bi, i: (bi, i, 0, 0))
    return pl.pallas_call(
        _inproj_zdt_kernel,
        grid=(b, tp // tm),
        in_specs=[pl.BlockSpec((1, tm, d), lambda bi, i: (bi, i, 0)),
                  const((1, d)), const(w.shape), const((1, LANES)), const((1, LANES))],
        out_specs=[pl.BlockSpec((1, n_z, tm, LANES), lambda bi, i: (bi, 0, i, 0)), tab, tab, tab, tab],
        out_shape=[jax.ShapeDtypeStruct((b, n_z, tp, LANES), BF16)]
        + [jax.ShapeDtypeStruct((b, tp // L, L, LANES), F32)] * 4,
        compiler_params=_cparams(2),
        name="inproj_zdt",
    )(h, nw.reshape(1, d), w, dtb, alog)


def _mlp_kernel(x_ref, nw_ref, wu_ref, wd_ref, fw_ref, o_ref, xn_ref, *, final_norm):
    j = pl.program_id(1)

    @pl.when(j == 0)
    def _():
        x = x_ref[...]
        xn_ref[...] = _rmsnorm_rows(x, nw_ref[...]).astype(BF16)
        o_ref[...] = x

    u = jnp.dot(xn_ref[...], wu_ref[...].astype(BF16), preferred_element_type=F32)
    a = jnp.square(jnp.maximum(u, 0.0)).astype(BF16)
    o_ref[...] += jnp.dot(a, wd_ref[...].astype(BF16), preferred_element_type=F32)

    if final_norm:
        @pl.when(j == pl.num_programs(1) - 1)
        def _():
            o_ref[...] = _rmsnorm_rows(o_ref[...], fw_ref[...])


def mlp_residual(h2, nw, w_up, w_down, fw, *, layer, tm, tf, final_norm):
    n, d = h2.shape
    dff = w_up.shape[2]
    return pl.pallas_call(
        functools.partial(_mlp_kernel, final_norm=final_norm),
        grid=(n // tm, dff // tf),
        in_specs=[pl.BlockSpec((tm, d), lambda i, j: (i, 0)),
                  pl.BlockSpec((1, d), lambda i, j: (0, 0)),
                  pl.BlockSpec((None, d, tf), lambda i, j: (layer, 0, j)),
                  pl.BlockSpec((None, tf, d), lambda i, j: (layer, j, 0)),
                  pl.BlockSpec((1, d), lambda i, j: (0, 0))],
        out_specs=pl.BlockSpec((tm, d), lambda i, j: (i, 0)),
        out_shape=jax.ShapeDtypeStruct((n, d), F32),
        scratch_shapes=[pltpu.VMEM((tm, d), BF16)],
        compiler_params=_cparams(2, vmem=VMEM_LIMIT_LARGE),
        name="mlp_final" if final_norm else "mlp",
    )(h2, nw.reshape(1, d), w_up, w_down, fw.reshape(1, d))


def _mlp_final_kernel(x_ref, nw_ref, wu_ref, wd_ref, fw_ref, o_ref, xn_ref):
    j = pl.program_id(2)

    @pl.when(j == 0)
    def _():
        x = x_ref[0]
        xn_ref[...] = _rmsnorm_rows(x, nw_ref[...]).astype(BF16)
        o_ref[0] = x

    u = jnp.dot(xn_ref[...], wu_ref[...].astype(BF16), preferred_element_type=F32)
    a = jnp.square(jnp.maximum(u, 0.0)).astype(BF16)
    o_ref[0] += jnp.dot(a, wd_ref[...].astype(BF16), preferred_element_type=F32)

    @pl.when(j == pl.num_programs(2) - 1)
    def _():
        o_ref[0] = _rmsnorm_rows(o_ref[0], fw_ref[...])


def mlp_residual_final(h, nw, w_up, w_down, fw, *, layer, seq, tm, tf):
    b, _, d = h.shape
    dff = w_up.shape[2]
    return pl.pallas_call(
        _mlp_final_kernel,
        grid=(b, seq // tm, dff // tf),
        in_specs=[pl.BlockSpec((pl.Element(1), pl.Element(tm), pl.Element(d)),
                               lambda bi, i, j: (bi, pl.multiple_of(N_META + i * tm, SUBLANES), 0)),
                  pl.BlockSpec((1, d), lambda bi, i, j: (0, 0)),
                  pl.BlockSpec((None, d, tf), lambda bi, i, j: (layer, 0, j)),
                  pl.BlockSpec((None, tf, d), lambda bi, i, j: (layer, j, 0)),
                  pl.BlockSpec((1, d), lambda bi, i, j: (0, 0))],
        out_specs=pl.BlockSpec((1, tm, d), lambda bi, i, j: (bi, i, 0)),
        out_shape=jax.ShapeDtypeStruct((b, seq, d), F32),
        scratch_shapes=[pltpu.VMEM((tm, d), BF16)],
        compiler_params=_cparams(3, vmem=VMEM_LIMIT_LARGE),
        name="mlp_final",
    )(h, nw.reshape(1, d), w_up, w_down, fw.reshape(1, d))


def _outproj_slab_kernel(x_ref, first_ref, last_ref, ya_ref, yb_ref, w_ref, o_ref):
    parts = ([ya_ref[0, s].astype(BF16) for s in range(ya_ref.shape[1])]
             + [yb_ref[0, s].astype(BF16) for s in range(yb_ref.shape[1])])
    y = jnp.concatenate(parts, axis=-1)
    o_ref[0] = _seq_tile(x_ref, first_ref, last_ref) + jnp.dot(y, w_ref[...], preferred_element_type=F32)


def outproj_slab(x, first, last, ya, yb, w, *, tm):
    b, seq, d = x.shape
    sa, sb, tp = ya.shape[1], yb.shape[1], ya.shape[2]
    return pl.pallas_call(
        _outproj_slab_kernel,
        grid=(b, tp // tm),
        in_specs=_seq_specs(seq, tm, d) + [
            pl.BlockSpec((1, sa, tm, LANES), lambda bi, i: (bi, 0, i, 0)),
            pl.BlockSpec((1, sb, tm, LANES), lambda bi, i: (bi, 0, i, 0)),
            pl.BlockSpec(w.shape, lambda bi, i: (0, 0))],
        out_specs=pl.BlockSpec((1, tm, d), lambda bi, i: (bi, i, 0)),
        out_shape=jax.ShapeDtypeStruct((b, tp, d), F32),
        compiler_params=_cparams(2),
        name="outproj_slab",
    )(x, first, last, ya, yb, w)


def _gated_outproj_kernel(h_ref, y_ref, z_ref, nw_ref, w_ref, o_ref, *, group):
    acc = h_ref[0]
    spg = group // LANES
    for g in range(y_ref.shape[2] // group):
        cols = slice(group * g, group * (g + 1))
        zg = jnp.concatenate([z_ref[0, spg * g + k] for k in range(spg)], axis=-1).astype(F32)
        gg = y_ref[0, :, cols].astype(F32) * _silu(zg)
        ms = jnp.mean(gg * gg, axis=-1, keepdims=True)
        part = (gg * lax.rsqrt(ms + NORM_EPS) * nw_ref[:, cols]).astype(BF16)
        acc = acc + jnp.dot(part, w_ref[cols, :], preferred_element_type=F32)
    o_ref[0] = acc


def gated_outproj(h, y, z, nw, w, *, tm, group):
    b, tp, d = h.shape
    k = y.shape[-1]
    return pl.pallas_call(
        functools.partial(_gated_outproj_kernel, group=group),
        grid=(b, tp // tm),
        in_specs=[pl.BlockSpec((1, tm, d), lambda bi, i: (bi, i, 0)),
                  pl.BlockSpec((1, tm, k), lambda bi, i: (bi, i, 0)),
                  pl.BlockSpec((1, k // LANES, tm, LANES), lambda bi, i: (bi, 0, i, 0)),
                  pl.BlockSpec((1, k), lambda bi, i: (0, 0)),
                  pl.BlockSpec(w.shape, lambda bi, i: (0, 0))],
        out_specs=pl.BlockSpec((1, tm, d), lambda bi, i: (bi, i, 0)),
        out_shape=jax.ShapeDtypeStruct(h.shape, F32),
        compiler_params=_cparams(2),
        name="gated_outproj",
    )(h, y, z, nw, w)


def _rglru_kernel(x_ref, g_ref, wa_ref, ba_ref, wx_ref, bx_ref, lam_ref, o_ref,
                  xcp, a_s, b_s, hn, carry, *, ls):
    n_slab = x_ref.shape[1]
    slabs = [slice(LANES * s, LANES * (s + 1)) for s in range(n_slab)]

    @pl.when(pl.program_id(1) == 0)
    def _():
        carry[...] = jnp.zeros(carry.shape, F32)

    def gather_body(tau, c):
        for s in range(n_slab):
            xcp[_rows(tau), slabs[s]] = x_ref[0, s, pl.ds(tau, SUBLANES, stride=ls), :]
        return c

    lax.fori_loop(0, ls, gather_body, 0, unroll=2)

    xc = xcp[...]
    xb = xc.astype(BF16)
    r = _sigmoid(jnp.dot(xb, wa_ref[...], preferred_element_type=F32) + ba_ref[...])
    i = _sigmoid(jnp.dot(xb, wx_ref[...], preferred_element_type=F32) + bx_ref[...])
    log_a = (-LRU_C) * r * _softplus(-lam_ref[...])
    a = jnp.exp(log_a)
    a_s[...] = a
    v = -jnp.tanh(log_a) * (a * a + 1.0)
    b_s[...] = jnp.where(v > 0.0, v * lax.rsqrt(v), 0.0) * (i * xc)

    def pass1(tau, c):
        out = []
        for s in range(n_slab):
            p, e = c[2 * s], c[2 * s + 1]
            av = a_s[_rows(tau), slabs[s]]
            out += [p * av, av * e + b_s[_rows(tau), slabs[s]]]
        return tuple(out)

    one = jnp.ones((SUBLANES, LANES), F32)
    zero = jnp.zeros((SUBLANES, LANES), F32)
    pe = lax.fori_loop(0, ls, pass1, (one, zero) * n_slab, unroll=2)

    h0 = []
    for s in range(n_slab):
        p, e = pe[2 * s], pe[2 * s + 1]
        c = carry[0:1, slabs[s]]
        rows = []
        for j in range(SUBLANES):
            rows.append(c)
            c = p[j:j + 1] * c + e[j:j + 1]
        carry[0:1, slabs[s]] = c
        h0.append(jnp.concatenate(rows, axis=0))

    def pass2(tau, hs):
        out = []
        for s in range(n_slab):
            h = a_s[_rows(tau), slabs[s]] * hs[s] + b_s[_rows(tau), slabs[s]]
            hn[s, pl.ds(tau, SUBLANES, stride=ls), :] = h
            out.append(h)
        return tuple(out)

    lax.fori_loop(0, ls, pass2, tuple(h0), unroll=2)
    for s in range(n_slab):
        o_ref[0, s] = (hn[s] * _gelu_tanh(g_ref[0, s])).astype(o_ref.dtype)


def rglru(proj, wa_bd, ba, wx_bd, bx, lam, *, ls):
    b, _, tp, _ = proj.shape
    w = lam.shape[0]
    ns = w // LANES
    tt = SUBLANES * ls
    const = lambda shape: pl.BlockSpec(shape, lambda bi, t: (0,) * len(shape))
    return pl.pallas_call(
        functools.partial(_rglru_kernel, ls=ls),
        grid=(b, tp // tt),
        in_specs=[pl.BlockSpec((1, ns, tt, LANES), lambda bi, t: (bi, 0, t, 0)),
                  pl.BlockSpec((1, ns, tt, LANES), lambda bi, t: (bi, 1, t, 0)),
                  const((w, w)), const((1, w)), const((w, w)), const((1, w)), const((1, w))],
        out_specs=pl.BlockSpec((1, ns, tt, LANES), lambda bi, t: (bi, 0, t, 0)),
        out_shape=jax.ShapeDtypeStruct((b, ns, tp, LANES), BF16),
        scratch_shapes=[pltpu.VMEM((tt, w), F32), pltpu.VMEM((tt, w), F32), pltpu.VMEM((tt, w), F32),
                        pltpu.VMEM((ns, tt, LANES), F32), pltpu.VMEM((SUBLANES, w), F32)],
        compiler_params=_cparams(2),
        name="rglru",
    )(proj, proj, wa_bd, ba.reshape(1, w), wx_bd, bx.reshape(1, w), lam.reshape(1, w))


def _s5_kernel(u_ref, bend_ref, kc_ref, l8r_ref, l8i_ref, plr_ref, pli_ref, d_ref, wg_ref, bg_ref,
               o_ref, lp, st, yv, carry, *, ls):
    sc = S5_SC
    n_cs = l8r_ref.shape[1]
    stride = sc * ls
    cre = [slice(LANES * k, LANES * (k + 1)) for k in range(n_cs)]
    cim = [slice(LANES * (n_cs + k), LANES * (n_cs + k + 1)) for k in range(n_cs)]
    lanes = [slice(LANES * q, LANES * (q + 1)) for q in range(sc)]

    @pl.when(pl.program_id(2) == 0)
    def _():
        carry[...] = jnp.zeros(carry.shape, F32)

    def gather_body(tau, c):
        for sg in range(sc):
            lp[_rows(tau), lanes[sg]] = u_ref[0, 0, pl.ds(tau * sc + sg, SUBLANES, stride=stride), :]
        return c

    lax.fori_loop(0, ls, gather_body, 0)
    st[...] = jnp.dot(lp[...].astype(BF16), bend_ref[0], preferred_element_type=F32)

    lam = [(jnp.broadcast_to(l8r_ref[0, k], (SUBLANES, LANES)),
            jnp.broadcast_to(l8i_ref[0, k], (SUBLANES, LANES))) for k in range(n_cs)]

    def step(tau, k, sr, si):
        lr, li = lam[k]
        return (lr * sr - li * si + st[_rows(tau), cre[k]], lr * si + li * sr + st[_rows(tau), cim[k]])

    def pass1(tau, c):
        out = []
        for k in range(n_cs):
            out += list(step(tau, k, c[2 * k], c[2 * k + 1]))
        return tuple(out)

    zero = jnp.zeros((SUBLANES, LANES), F32)
    ends = lax.fori_loop(0, ls, pass1, (zero,) * (2 * n_cs))

    starts = []
    for k in range(n_cs):
        er, ei = ends[2 * k], ends[2 * k + 1]
        pr, pi = plr_ref[0, k], pli_ref[0, k]
        c_r, c_i = carry[0:1, cre[k]], carry[0:1, cim[k]]
        rows_r, rows_i = [], []
        for j in range(SUBLANES):
            rows_r.append(c_r)
            rows_i.append(c_i)
            c_r, c_i = (pr * c_r - pi * c_i + er[j:j + 1], pr * c_i + pi * c_r + ei[j:j + 1])
        carry[0:1, cre[k]] = c_r
        carry[0:1, cim[k]] = c_i
        starts += [jnp.concatenate(rows_r, axis=0), jnp.concatenate(rows_i, axis=0)]

    def pass2(tau, c):
        out = []
        for k in range(n_cs):
            nr, ni = step(tau, k, c[2 * k], c[2 * k + 1])
            st[_rows(tau), cre[k]] = c[2 * k]
            st[_rows(tau), cim[k]] = c[2 * k + 1]
            out += [nr, ni]
        return tuple(out)

    lax.fori_loop(0, ls, pass2, tuple(starts))

    kin = lp.shape[1]
    lp_b, st_b = lp[...].astype(BF16), st[...].astype(BF16)
    tile = 2 * LANES
    for tcol in range(kin // tile):
        cols = slice(tile * tcol, tile * (tcol + 1))
        kk = tile * (tcol + 1)
        yv[:, cols] = (jnp.dot(lp_b[:, :kk], kc_ref[0, 0:kk, cols], preferred_element_type=F32)
                       + jnp.dot(st_b, kc_ref[0, kin:, cols], preferred_element_type=F32))
    for q in range(sc):
        y = yv[:, lanes[q]] + d_ref[0] * lp[:, lanes[q]]
        y = _gelu_tanh(y)
        yv[:, lanes[q]] = y * _sigmoid(jnp.dot(y.astype(BF16), wg_ref[0], preferred_element_type=F32)
                                       + bg_ref[0])

    def scatter_body(tau, c):
        for sg in range(sc):
            o_ref[0, 0, pl.ds(tau * sc + sg, SUBLANES, stride=stride), :] = yv[_rows(tau), lanes[sg]]
        return c

    lax.fori_loop(0, ls, scatter_body, 0)


def s5(proj, first_slab, bend, kc, l8r, l8i, plr, pli, d, wg, bg, *, ls):
    b, _, tp, _ = proj.shape
    nblk, kin, nst = bend.shape
    n_cs = nst // (2 * LANES)
    tt = SUBLANES * ls * S5_SC
    nc = SUBLANES * ls
    per_blk = lambda shape: pl.BlockSpec((1,) + shape, lambda bi, gb, t: (gb,) + (0,) * len(shape))
    return pl.pallas_call(
        functools.partial(_s5_kernel, ls=ls),
        grid=(b, nblk, tp // tt),
        in_specs=[pl.BlockSpec((1, 1, tt, LANES), lambda bi, gb, t: (bi, first_slab + gb, t, 0)),
                  per_blk((kin, nst)), per_blk((kin + nst, kin)),
                  per_blk((n_cs, 1, LANES)), per_blk((n_cs, 1, LANES)),
                  per_blk((n_cs, 1, LANES)), per_blk((n_cs, 1, LANES)),
                  per_blk((1, LANES)), per_blk((LANES, LANES)), per_blk((1, LANES))],
        out_specs=pl.BlockSpec((1, 1, tt, LANES), lambda bi, gb, t: (bi, gb, t, 0)),
        out_shape=jax.ShapeDtypeStruct((b, nblk, tp, LANES), F32),
        scratch_shapes=[pltpu.VMEM((nc, kin), F32), pltpu.VMEM((nc, nst), F32),
                        pltpu.VMEM((nc, kin), F32), pltpu.VMEM((SUBLANES, nst), F32)],
        compiler_params=_cparams(3),
        name="s5",
    )(proj, bend, kc, l8r, l8i, plr, pli, d, wg, bg)


def _s5_params(a_re, a_im, b_re, b_im, c_re, c_im, d, log_dt, w_glu, b_glu, *, ls):
    g, p = a_re.shape
    sc = S5_SC
    gpb = LANES // S5_GROUP
    nblk = g // gpb
    dt = jnp.exp(log_dt)[:, None]
    mag = jnp.exp(a_re * dt)
    ar, ai = mag * jnp.cos(a_im * dt), mag * jnp.sin(a_im * dt)
    den = a_re * a_re + a_im * a_im
    fr = ((ar - 1.0) * a_re + ai * a_im) / den
    fi = (ai * a_re - (ar - 1.0) * a_im) / den
    bbar_re = fr[..., None] * b_re - fi[..., None] * b_im
    bbar_im = fr[..., None] * b_im + fi[..., None] * b_re

    def lam_pow(k):
        k = jnp.asarray(k, F32).reshape(-1, 1, 1)
        m = jnp.exp(a_re * dt * k)
        return m * jnp.cos(a_im * dt * k), m * jnp.sin(a_im * dt * k)

    pw_r, pw_i = lam_pow(jnp.arange(sc + 1))
    def group_diag(dense, rows_per_group):
        n = dense.shape[-1]
        tiled = jnp.tile(dense, (1,) * (dense.ndim - 1) + (gpb,))
        rg = lax.broadcasted_iota(jnp.int32, tiled.shape[-2:], 0) // rows_per_group
        cg = lax.broadcasted_iota(jnp.int32, tiled.shape[-2:], 1) // n
        return jnp.where(rg == cg, tiled, 0.0)

    def by_block(m):
        k, _, a, b2 = m.shape
        return jnp.transpose(m.reshape(k, nblk, gpb, a, b2), (0, 1, 2, 4, 3)).reshape(k, nblk, gpb * b2, a)

    wr = jnp.stack([pw_r[sc - 1 - s] for s in range(sc)])[..., None]
    wi = jnp.stack([pw_i[sc - 1 - s] for s in range(sc)])[..., None]
    e_re = wr * bbar_re - wi * bbar_im
    e_im = wr * bbar_im + wi * bbar_re
    rows_cat = lambda m: jnp.concatenate([m[k] for k in range(m.shape[0])], axis=-2)
    bend = jnp.concatenate([rows_cat(group_diag(by_block(e_re).astype(BF16), S5_GROUP)),
                            rows_cat(group_diag(by_block(e_im).astype(BF16), S5_GROUP))], axis=-1)

    cl_re = c_re[None] * pw_r[:, :, None, :] - c_im[None] * pw_i[:, :, None, :]
    cl_im = c_re[None] * pw_i[:, :, None, :] + c_im[None] * pw_r[:, :, None, :]
    kl = (jnp.einsum('kgip,gpj->kgij', cl_re[:sc], bbar_re)
          - jnp.einsum('kgip,gpj->kgij', cl_im[:sc], bbar_im))
    kd = group_diag(by_block(kl).astype(BF16), S5_GROUP)
    zero_blk = jnp.zeros_like(kd[0])
    kintra = jnp.concatenate(
        [jnp.concatenate([kd[t - s] if t >= s else zero_blk for t in range(sc)], axis=-1)
         for s in range(sc)], axis=-2)

    def out_bd(m):
        dense = jnp.transpose(m, (0, 1, 3, 2))
        d2 = group_diag(dense.reshape(sc, nblk, gpb * p, S5_GROUP).astype(BF16), p)
        return jnp.concatenate([d2[t] for t in range(sc)], axis=-1)

    kc = jnp.concatenate([kintra, out_bd(cl_re[1:]), out_bd(-cl_im[1:])], axis=-2)

    vec = lambda v: v.reshape(nblk, (gpb * p) // LANES, 1, LANES)
    pl_r, pl_i = lam_pow(jnp.asarray([sc * ls]))
    wg = group_diag(w_glu.reshape(nblk, LANES, S5_GROUP), S5_GROUP).astype(BF16)
    return (bend, kc, vec(pw_r[sc]), vec(pw_i[sc]), vec(pl_r[0]), vec(pl_i[0]),
            d.reshape(nblk, 1, LANES), wg, b_glu.reshape(nblk, 1, LANES))


def _ssd_kernel(xbc_ref, cs_ref, rowp_ref, wend_ref, cdec_ref, dskip_ref, o_ref, hst):
    @pl.when(pl.program_id(1) == 0)
    def _():
        hst[...] = jnp.zeros(hst.shape, F32)

    for c in range(xbc_ref.shape[2] // SSD_CHUNK):
        _ssd_chunk(xbc_ref, cs_ref, rowp_ref, wend_ref, cdec_ref, dskip_ref, o_ref, hst, c)


def _ssd_chunk(xbc_ref, cs_ref, rowp_ref, wend_ref, cdec_ref, dskip_ref, o_ref, hst, c):
    L = SSD_CHUNK
    rs = slice(L * c, L * (c + 1))
    n_groups = hst.shape[0]
    gw = SSD_HPG * SSD_HEAD_DIM
    spg = gw // LANES
    n_xs = n_groups * spg
    row = lax.broadcasted_iota(jnp.int32, (L, L), 0)
    col = lax.broadcasted_iota(jnp.int32, (L, L), 1)
    causal = row >= col
    left_half = col < SSD_HEAD_DIM
    head_of_lane = lax.broadcasted_iota(jnp.int32, (L, gw), 1) // SSD_HEAD_DIM

    def head_rows(ref, g):
        return jnp.concatenate(
            [jnp.broadcast_to(ref[0, c, SSD_HPG * g + r:SSD_HPG * g + r + 1, :], (SSD_HEAD_DIM, L))
             for r in range(SSD_HPG)], axis=0)

    def x_slabs(g):
        return jnp.concatenate([xbc_ref[0, spg * g + k, rs, :] for k in range(spg)], axis=-1)

    scores, y_offs = [], []
    for g in range(n_groups):
        bm = xbc_ref[0, n_xs + g, rs, :]
        cm = xbc_ref[0, n_xs + n_groups + g, rs, :]
        scores.append(lax.dot_general(cm, bm, (((1,), (1,)), ((), ())), preferred_element_type=F32))
        h_prev = hst[g]
        y_offs.append(lax.dot_general(cm, h_prev.astype(BF16), (((1,), (1,)), ((), ())),
                                      preferred_element_type=F32))
        xw_t = (x_slabs(g).astype(F32).T * head_rows(wend_ref, g)).astype(BF16)
        hst[g] = head_rows(cdec_ref, g) * h_prev + jnp.dot(xw_t, bm, preferred_element_type=F32)

    for g in range(n_groups):
        xsb = x_slabs(g)
        ms, bcs = [], []
        for r in range(SSD_HPG):
            h = SSD_HPG * g + r
            bcs.append(jnp.broadcast_to(cs_ref[0, c, :, h:h + 1], (L, L)))
            ms.append((scores[g] * jnp.exp2(jnp.where(causal, bcs[r] - rowp_ref[0, c, h:h + 1, :], -jnp.inf))
                       ).astype(BF16))
        x_bd = jnp.concatenate([jnp.where(head_of_lane == r, xsb, jnp.zeros_like(xsb))
                                for r in range(SSD_HPG)], axis=0)
        y_diag = jnp.dot(jnp.concatenate(ms, axis=1), x_bd, preferred_element_type=F32)
        f_start = jnp.concatenate([jnp.exp2(jnp.where(left_half, bcs[2 * k], bcs[2 * k + 1]))
                                   for k in range(spg)], axis=-1)
        o_ref[0, rs, gw * g:gw * (g + 1)] = (
            y_diag + y_offs[g] * f_start + dskip_ref[:, gw * g:gw * (g + 1)] * xsb.astype(F32)
        ).astype(o_ref.dtype)


def ssd_core(xbc, tables, dskip):
    b, n_conv, tp, _ = xbc.shape
    d_inner = dskip.shape[1]
    n_groups = d_inner // (SSD_HPG * SSD_HEAD_DIM)
    cps = SSD_CHUNKS_PER_STEP
    L = SSD_CHUNK * cps
    const = lambda shape: pl.BlockSpec(shape, lambda bi, c: (0,) * len(shape))
    tab = pl.BlockSpec((1, cps, SSD_CHUNK, LANES), lambda bi, c: (bi, c, 0, 0))
    return pl.pallas_call(
        _ssd_kernel,
        grid=(b, tp // L),
        in_specs=[pl.BlockSpec((1, n_conv, L, LANES), lambda bi, c: (bi, 0, c, 0)),
                  tab, tab, tab, tab, const((1, d_inner))],
        out_specs=pl.BlockSpec((1, L, d_inner), lambda bi, c: (bi, c, 0)),
        out_shape=jax.ShapeDtypeStruct((b, tp, d_inner), BF16),
        scratch_shapes=[pltpu.VMEM((n_groups, SSD_HPG * SSD_HEAD_DIM, SSD_STATE), F32)],
        compiler_params=_cparams(2),
        name="ssd_core",
    )(xbc, *tables, dskip)


def _block_diag(w):
    n, di, do = w.shape
    return jnp.einsum('gij,gh->gihj', w, jnp.eye(n, dtype=w.dtype)).reshape(n * di, n * do)


def _pad_lanes(v, fill=0.0):
    return jnp.pad(v, (0, LANES - v.shape[0]), constant_values=fill).reshape(1, LANES)


def kernel(x, meta_tokens, norm_mix, norm_mlp, norm_final, ev_w_in, lru_conv_w, lru_conv_b, lru_w_a, lru_b_a, lru_w_x, lru_b_x, lru_lambda, s5_a_re, s5_a_im, s5_b_re, s5_b_im, s5_c_re, s5_c_im, s5_d, s5_log_dt, s5_w_glu, s5_b_glu, ev_w_out, ssd_w_in, ssd_conv_w, ssd_conv_b, ssd_dt_bias, ssd_a_log, ssd_d, ssd_norm, ssd_w_out, mlp_w_up, mlp_w_down):
    bsz, seq, d = x.shape
    t = seq + N_META
    unit = math.lcm(SUBLANES * LRU_LS, SUBLANES * S5_LS * S5_SC, ROW_TILE, SSD_CHUNK)
    tp = -(-t // unit) * unit
    tm = ROW_TILE
    assert tp - t < tm <= seq
    meta = jnp.broadcast_to(meta_tokens[None].astype(x.dtype), (bsz, N_META, d))
    first = jnp.concatenate([meta, x[:, :tm - N_META]], axis=1)
    last = jnp.concatenate([x[:, tp - tm - N_META:], jnp.zeros((bsz, tp - t, d), x.dtype)], axis=1)

    lru_w = lru_conv_w.shape[-1]
    proj = norm_matmul(x, first, last, norm_mix[0], ev_w_in[0].astype(BF16), lru_conv_w[0], lru_conv_b[0],
                       tp=tp, tm=tm)
    y_lru = rglru(proj, _block_diag(lru_w_a[0]).astype(BF16), lru_b_a[0],
                  _block_diag(lru_w_x[0]).astype(BF16), lru_b_x[0], lru_lambda[0], ls=LRU_LS)
    y_s5 = s5(proj, 2 * lru_w // LANES,
              *_s5_params(s5_a_re[0], s5_a_im[0], s5_b_re[0], s5_b_im[0], s5_c_re[0], s5_c_im[0],
                          s5_d[0], s5_log_dt[0], s5_w_glu[0], s5_b_glu[0], ls=S5_LS), ls=S5_LS)
    h = outproj_slab(x, first, last, y_lru, y_s5, ev_w_out[0].astype(BF16), tm=tm)
    h = mlp_residual(h.reshape(bsz * tp, d), norm_mlp[0], mlp_w_up, mlp_w_down,
                     norm_final, layer=0, tm=MLP_ROW_TILE, tf=512,
                     final_norm=False).reshape(bsz, tp, d)

    d_inner = ssd_w_out.shape[1]
    conv_dim = ssd_conv_w.shape[-1]
    n_heads = ssd_dt_bias.shape[-1]
    w_in = ssd_w_in[0]
    w_zdt = jnp.concatenate([w_in[:, :d_inner], w_in[:, d_inner + conv_dim:],
                             jnp.zeros((d, LANES - n_heads), F32)], axis=1).astype(BF16)
    xbc = inproj_conv(h, norm_mix[1], w_in[:, d_inner:d_inner + conv_dim].astype(BF16),
                      ssd_conv_w[0], ssd_conv_b[0], tm=ROW_TILE, tn=2048)
    z, *tables = inproj_zdt(h, norm_mix[1], w_zdt, _pad_lanes(ssd_dt_bias[0]), _pad_lanes(ssd_a_log[0]),
                            tm=ROW_TILE)
    y = ssd_core(xbc, tables, jnp.repeat(ssd_d[0], SSD_HEAD_DIM).reshape(1, d_inner))
    h = gated_outproj(h, y, z, ssd_norm[0].reshape(1, d_inner), ssd_w_out[0].astype(BF16),
                      tm=GATED_ROW_TILE, group=SSD_HPG * SSD_HEAD_DIM)
    return mlp_residual_final(h, norm_mlp[1], mlp_w_up, mlp_w_down,
                              norm_final, layer=1, seq=seq, tm=FINAL_ROW_TILE, tf=512)
```

```python
import functools
import math

import jax
import jax.numpy as jnp
from jax import lax
from jax.experimental import pallas as pl
from jax.experimental.pallas import tpu as pltpu

F32 = jnp.float32
BF16 = jnp.bfloat16

LANES = 128
SUBLANES = 8
NORM_EPS = 1e-5
LOG2_E = 1.4426950408889634
N_META = 16
CONV_WIDTH = 4
LRU_C = 8.0
S5_GROUP = 16
S5_STATE = 64
SSD_HEAD_DIM = 64
SSD_STATE = 128
SSD_CHUNK = 128
SSD_HPG = 4
SSD_CHUNKS_PER_STEP = 5

LRU_LS = 130
LRU_UNROLL = 5
S5_SC = SUBLANES
S5_LS = 65
S5_UNROLL = 5
ROW_TILE = 1280
MLP_ROW_TILE = 2080
GATED_ROW_TILE = 640
FINAL_ROW_TILE = 2048
VMEM_LIMIT = 48 * 1024 * 1024
VMEM_LIMIT_LARGE = 58 * 1024 * 1024


def _cparams(n_axes, vmem=VMEM_LIMIT):
    return pltpu.CompilerParams(dimension_semantics=("arbitrary",) * n_axes,
                                vmem_limit_bytes=vmem)


def _sigmoid(x):
    return 0.5 + 0.5 * jnp.tanh(0.5 * x)


def _silu(x):
    hx = 0.5 * x
    return hx + hx * jnp.tanh(hx)


def _gelu_tanh(x):
    return 0.5 * x * (1.0 + jnp.tanh(math.sqrt(2.0 / math.pi) * (x + 0.044715 * (x * x * x))))


def _log1p(e):
    u = 1.0 + e
    return jnp.where(u == 1.0, e, jnp.log(u) * (e / (u - 1.0)))


def _softplus(x):
    return jnp.maximum(x, 0.0) + _log1p(jnp.exp(-jnp.abs(x)))


def _rmsnorm_rows(x, w):
    ms = jnp.mean(x * x, axis=-1, keepdims=True)
    return x * lax.rsqrt(ms + NORM_EPS) * w


def _rows(tau):
    return pl.ds(pl.multiple_of(tau * SUBLANES, SUBLANES), SUBLANES)


def _seq_specs(seq, tm, d):
    x_spec = pl.BlockSpec(
        (pl.Element(1), pl.Element(tm), pl.Element(d)),
        lambda bi, i: (bi, pl.multiple_of(jnp.clip(i * tm - N_META, 0, seq - tm), SUBLANES), 0))
    edge_spec = pl.BlockSpec((1, tm, d), lambda bi, i: (bi, 0, 0), pipeline_mode=pl.Buffered(1))
    return [x_spec, edge_spec, edge_spec]


def _seq_tile(x_ref, first_ref, last_ref):
    i = pl.program_id(1)
    return jnp.where(i == 0, first_ref[0], jnp.where(i == pl.num_programs(1) - 1, last_ref[0], x_ref[0]))


def _norm_matmul_kernel(x_ref, first_ref, last_ref, nw_ref, w_ref, cw_ref, cb_ref, o_ref, rbuf, hist):
    tm = x_ref.shape[1]
    n_conv = rbuf.shape[0]
    hrows = SUBLANES

    @pl.when(pl.program_id(1) == 0)
    def _():
        hist[...] = jnp.zeros(hist.shape, F32)

    x = _seq_tile(x_ref, first_ref, last_ref)
    rs = lax.rsqrt(jnp.mean(x * x, axis=-1, keepdims=True) + NORM_EPS)
    xb = (x * nw_ref[...]).astype(BF16)
    step = 4
    for c in range(o_ref.shape[1] // step):
        r = jnp.dot(xb, w_ref[:, LANES * step * c:LANES * step * (c + 1)], preferred_element_type=F32) * rs
        for k in range(step):
            s = step * c + k
            if s >= n_conv:
                o_ref[0, s] = r[:, LANES * k:LANES * (k + 1)]
                continue
            sl = slice(LANES * s, LANES * (s + 1))
            rbuf[s, 0:hrows, :] = hist[s]
            rbuf[s, hrows:hrows + tm, :] = r[:, LANES * k:LANES * (k + 1)]
            hist[s] = rbuf[s, tm:tm + hrows, :]
            taps = [cw_ref[m, :, sl] for m in range(CONV_WIDTH)]
            bias = cb_ref[:, sl]
            for g in range(tm // SUBLANES):
                r0 = hrows + SUBLANES * g
                acc = bias
                for m in range(CONV_WIDTH):
                    acc = acc + taps[m] * rbuf[s, r0 - m:r0 - m + SUBLANES, :]
                o_ref[0, s, SUBLANES * g:SUBLANES * (g + 1), :] = acc


def norm_matmul(x, first, last, nw, w, cw, cb, *, tp, tm):
    b, seq, d = x.shape
    n = w.shape[1]
    wc = cw.shape[1]
    cw8 = jnp.broadcast_to(jnp.stack([cw[CONV_WIDTH - 1 - m] for m in range(CONV_WIDTH)])[:, None, :],
                           (CONV_WIDTH, SUBLANES, wc))
    cb8 = jnp.broadcast_to(cb[None, :], (SUBLANES, wc))
    const = lambda shape: pl.BlockSpec(shape, lambda bi, i: (0,) * len(shape))
    return pl.pallas_call(
        _norm_matmul_kernel,
        grid=(b, tp // tm),
        in_specs=_seq_specs(seq, tm, d) + [
            const((1, d)), const((d, n)), const((CONV_WIDTH, SUBLANES, wc)), const((SUBLANES, wc))],
        out_specs=pl.BlockSpec((1, n // LANES, tm, LANES), lambda bi, i: (bi, 0, i, 0)),
        out_shape=jax.ShapeDtypeStruct((b, n // LANES, tp, LANES), F32),
        scratch_shapes=[pltpu.VMEM((wc // LANES, SUBLANES + tm, LANES), F32),
                        pltpu.VMEM((wc // LANES, SUBLANES, LANES), F32)],
        compiler_params=_cparams(2),
        name="norm_matmul",
    )(x, first, last, nw.reshape(1, d), w, cw8, cb8)


def _inproj_conv_kernel(x_ref, nw_ref, w_ref, cw_ref, cb_ref, o_ref, xn_ref, rs, hist):
    i, j = pl.program_id(1), pl.program_id(2)
    tm = x_ref.shape[1]
    n_slab = o_ref.shape[1]
    hrows = SUBLANES
    pack = 2 * SUBLANES

    @pl.when(j == 0)
    def _():
        xn_ref[...] = _rmsnorm_rows(x_ref[0], nw_ref[...]).astype(BF16)

    @pl.when(i == 0)
    def _():
        hist[j] = jnp.zeros(hist.shape[1:], F32)

    xn = xn_ref[...]
    proj = lambda c: jnp.dot(xn, w_ref[:, 2 * LANES * c:2 * LANES * (c + 1)], preferred_element_type=F32)
    r_next = proj(0)
    for c in range(n_slab // 2):
        r = r_next
        if c + 1 < n_slab // 2:
            r_next = proj(c + 1)
        for k in range(2):
            s = 2 * c + k
            sl = slice(LANES * s, LANES * (s + 1))
            rs[s, 0:hrows, :] = hist[j, s]
            rs[s, hrows:hrows + tm, :] = r[:, LANES * k:LANES * (k + 1)]
            hist[j, s] = rs[s, tm:tm + hrows, :]
            taps = [cw_ref[m, :, sl] for m in range(CONV_WIDTH)]
            bias = cb_ref[:, sl]
            for g in range(tm // pack):
                halves = []
                for hh in range(pack // SUBLANES):
                    r0 = hrows + pack * g + SUBLANES * hh
                    acc = bias
                    for m in range(CONV_WIDTH):
                        acc = acc + taps[m] * rs[s, r0 - m:r0 - m + SUBLANES, :]
                    halves.append(acc)
                o_ref[0, s, pack * g:pack * (g + 1), :] = _silu(jnp.concatenate(halves, axis=0).astype(BF16))


def inproj_conv(h, nw, w, cw, cb, *, tm, tn):
    b, tp, d = h.shape
    n = w.shape[1]
    ns = tn // LANES
    cw8 = jnp.broadcast_to(jnp.stack([cw[CONV_WIDTH - 1 - m] for m in range(CONV_WIDTH)])[:, None, :],
                           (CONV_WIDTH, SUBLANES, n))
    cb8 = jnp.broadcast_to(cb.reshape(1, n), (SUBLANES, n))
    return pl.pallas_call(
        _inproj_conv_kernel,
        grid=(b, tp // tm, n // tn),
        in_specs=[pl.BlockSpec((1, tm, d), lambda bi, i, j: (bi, i, 0)),
                  pl.BlockSpec((1, d), lambda bi, i, j: (0, 0)),
                  pl.BlockSpec((d, tn), lambda bi, i, j: (0, j)),
                  pl.BlockSpec((CONV_WIDTH, SUBLANES, tn), lambda bi, i, j: (0, 0, j)),
                  pl.BlockSpec((SUBLANES, tn), lambda bi, i, j: (0, j))],
        out_specs=pl.BlockSpec((1, ns, tm, LANES), lambda bi, i, j: (bi, j, i, 0)),
        out_shape=jax.ShapeDtypeStruct((b, n // LANES, tp, LANES), BF16),
        scratch_shapes=[pltpu.VMEM((tm, d), BF16),
                        pltpu.VMEM((ns, SUBLANES + tm, LANES), F32),
                        pltpu.VMEM((n // tn, ns, SUBLANES, LANES), F32)],
        compiler_params=_cparams(3),
        name="inproj_conv",
    )(h, nw.reshape(1, d), w, cw8, cb8)


def _ssd_tables(dt_raw, dtb, a, tri):
    L = SSD_CHUNK
    dt = _softplus(dt_raw + dtb)
    da = dt * a
    da_hi = da.astype(BF16)
    r1 = da - da_hi.astype(F32)
    da_mid = r1.astype(BF16)
    da_lo = (r1 - da_mid.astype(F32)).astype(BF16)
    cs = (jnp.dot(tri, da_hi, preferred_element_type=F32)
          + jnp.dot(tri, da_mid, preferred_element_type=F32)
          + jnp.dot(tri, da_lo, preferred_element_type=F32))
    dt_t = dt.T
    cs_t = cs.T
    last_t = jnp.broadcast_to(cs_t[:, L - 1:L], (L, L))
    return cs * LOG2_E, (cs_t - jnp.log(dt_t)) * LOG2_E, dt_t * jnp.exp(last_t - cs_t), jnp.exp(last_t)


def _inproj_zdt_kernel(x_ref, nw_ref, w_ref, dtb_ref, alog_ref, z_ref, cs_ref, rowp_ref, wend_ref, cdec_ref):
    x = x_ref[0]
    rs = lax.rsqrt(jnp.mean(x * x, axis=-1, keepdims=True) + NORM_EPS)
    xb = (x * nw_ref[...]).astype(BF16)
    n_z = z_ref.shape[1]
    dt_raw = jnp.dot(xb, w_ref[:, LANES * n_z:LANES * (n_z + 1)], preferred_element_type=F32) * rs
    L = SSD_CHUNK
    tri = (lax.broadcasted_iota(jnp.int32, (L, L), 0) >= lax.broadcasted_iota(jnp.int32, (L, L), 1)).astype(BF16)
    a = -jnp.exp(alog_ref[...])
    step = 4
    n_chunks = cs_ref.shape[1]
    for c in range(n_z // step):
        r = jnp.dot(xb, w_ref[:, LANES * step * c:LANES * step * (c + 1)], preferred_element_type=F32) * rs
        for k in range(step):
            z_ref[0, step * c + k] = r[:, LANES * k:LANES * (k + 1)].astype(BF16)
        for q in range(c * n_chunks // (n_z // step), (c + 1) * n_chunks // (n_z // step)):
            cs_ref[0, q], rowp_ref[0, q], wend_ref[0, q], cdec_ref[0, q] = _ssd_tables(
                dt_raw[L * q:L * (q + 1)], dtb_ref[...], a, tri)


def inproj_zdt(h, nw, w, dtb, alog, *, tm):
    b, tp, d = h.shape
    n_z = w.shape[1] // LANES - 1
    L = SSD_CHUNK
    const = lambda shape: pl.BlockSpec(shape, lambda bi, i: (0,) * len(shape))
    tab = pl.BlockSpec((1, tm // L, L, LANES), lambda bi, i: (bi, i, 0, 0))
    return pl.pallas_call(
        _inproj_zdt_kernel,
        grid=(b, tp // tm),
        in_specs=[pl.BlockSpec((1, tm, d), lambda bi, i: (bi, i, 0)),
                  const((1, d)), const(w.shape), const((1, LANES)), const((1, LANES))],
        out_specs=[pl.BlockSpec((1, n_z, tm, LANES), lambda bi, i: (bi, 0, i, 0)), tab, tab, tab, tab],
        out_shape=[jax.ShapeDtypeStruct((b, n_z, tp, LANES), BF16)]
        + [jax.ShapeDtypeStruct((b, tp // L, L, LANES), F32)] * 4,
        compiler_params=_cparams(2),
        name="inproj_zdt",
    )(h, nw.reshape(1, d), w, dtb, alog)


def _mlp_kernel(x_ref, nw_ref, wu_ref, wd_ref, fw_ref, o_ref, xn_ref, *, final_norm):
    j = pl.program_id(1)

    @pl.when(j == 0)
    def _():
        x = x_ref[...]
        xn_ref[...] = _rmsnorm_rows(x, nw_ref[...]).astype(BF16)
        o_ref[...] = x

    u = jnp.dot(xn_ref[...], wu_ref[...].astype(BF16), preferred_element_type=F32)
    a = jnp.square(jnp.maximum(u, 0.0)).astype(BF16)
    o_ref[...] += jnp.dot(a, wd_ref[...].astype(BF16), preferred_element_type=F32)

    if final_norm:
        @pl.when(j == pl.num_programs(1) - 1)
        def _():
            o_ref[...] = _rmsnorm_rows(o_ref[...], fw_ref[...])


def mlp_residual(h2, nw, w_up, w_down, fw, *, layer, tm, tf, final_norm):
    n, d = h2.shape
    dff = w_up.shape[2]
    return pl.pallas_call(
        functools.partial(_mlp_kernel, final_norm=final_norm),
        grid=(n // tm, dff // tf),
        in_specs=[pl.BlockSpec((tm, d), lambda i, j: (i, 0)),
                  pl.BlockSpec((1, d), lambda i, j: (0, 0)),
                  pl.BlockSpec((None, d, tf), lambda i, j: (layer, 0, j)),
                  pl.BlockSpec((None, tf, d), lambda i, j: (layer, j, 0)),
                  pl.BlockSpec((1, d), lambda i, j: (0, 0))],
        out_specs=pl.BlockSpec((tm, d), lambda i, j: (i, 0)),
        out_shape=jax.ShapeDtypeStruct((n, d), F32),
        scratch_shapes=[pltpu.VMEM((tm, d), BF16)],
        compiler_params=_cparams(2, vmem=VMEM_LIMIT_LARGE),
        name="mlp_final" if final_norm else "mlp",
    )(h2, nw.reshape(1, d), w_up, w_down, fw.reshape(1, d))


def _mlp_final_kernel(x_ref, nw_ref, wu_ref, wd_ref, fw_ref, o_ref, xn_ref):
    j = pl.program_id(2)

    @pl.when(j == 0)
    def _():
        x = x_ref[0]
        xn_ref[...] = _rmsnorm_rows(x, nw_ref[...]).astype(BF16)
        o_ref[0] = x

    u = jnp.dot(xn_ref[...], wu_ref[...].astype(BF16), preferred_element_type=F32)
    a = jnp.square(jnp.maximum(u, 0.0)).astype(BF16)
    o_ref[0] += jnp.dot(a, wd_ref[...].astype(BF16), preferred_element_type=F32)

    @pl.when(j == pl.num_programs(2) - 1)
    def _():
        o_ref[0] = _rmsnorm_rows(o_ref[0], fw_ref[...])


def mlp_residual_final(h, nw, w_up, w_down, fw, *, layer, seq, tm, tf):
    b, _, d = h.shape
    dff = w_up.shape[2]
    return pl.pallas_call(
        _mlp_final_kernel,
        grid=(b, seq // tm, dff // tf),
        in_specs=[pl.BlockSpec((pl.Element(1), pl.Element(tm), pl.Element(d)),
                               lambda bi, i, j: (bi, pl.multiple_of(N_META + i * tm, SUBLANES), 0)),
                  pl.BlockSpec((1, d), lambda bi, i, j: (0, 0)),
                  pl.BlockSpec((None, d, tf), lambda bi, i, j: (layer, 0, j)),
                  pl.BlockSpec((None, tf, d), lambda bi, i, j: (layer, j, 0)),
                  pl.BlockSpec((1, d), lambda bi, i, j: (0, 0))],
        out_specs=pl.BlockSpec((1, tm, d), lambda bi, i, j: (bi, i, 0)),
        out_shape=jax.ShapeDtypeStruct((b, seq, d), F32),
        scratch_shapes=[pltpu.VMEM((tm, d), BF16)],
        compiler_params=_cparams(3, vmem=VMEM_LIMIT_LARGE),
        name="mlp_final",
    )(h, nw.reshape(1, d), w_up, w_down, fw.reshape(1, d))


def _outproj_slab_kernel(x_ref, first_ref, last_ref, ya_ref, yb_ref, w_ref, o_ref):
    parts = ([ya_ref[0, s].astype(BF16) for s in range(ya_ref.shape[1])]
             + [yb_ref[0, s].astype(BF16) for s in range(yb_ref.shape[1])])
    y = jnp.concatenate(parts, axis=-1)
    o_ref[0] = _seq_tile(x_ref, first_ref, last_ref) + jnp.dot(y, w_ref[...], preferred_element_type=F32)


def outproj_slab(x, first, last, ya, yb, w, *, tm):
    b, seq, d = x.shape
    sa, sb, tp = ya.shape[1], yb.shape[1], ya.shape[2]
    return pl.pallas_call(
        _outproj_slab_kernel,
        grid=(b, tp // tm),
        in_specs=_seq_specs(seq, tm, d) + [
            pl.BlockSpec((1, sa, tm, LANES), lambda bi, i: (bi, 0, i, 0)),
            pl.BlockSpec((1, sb, tm, LANES), lambda bi, i: (bi, 0, i, 0)),
            pl.BlockSpec(w.shape, lambda bi, i: (0, 0))],
        out_specs=pl.BlockSpec((1, tm, d), lambda bi, i: (bi, i, 0)),
        out_shape=jax.ShapeDtypeStruct((b, tp, d), F32),
        compiler_params=_cparams(2),
        name="outproj_slab",
    )(x, first, last, ya, yb, w)


def _gated_outproj_kernel(h_ref, y_ref, z_ref, nw_ref, w_ref, o_ref, *, group):
    acc = h_ref[0]
    spg = group // LANES
    for g in range(y_ref.shape[2] // group):
        cols = slice(group * g, group * (g + 1))
        zg = jnp.concatenate([z_ref[0, spg * g + k] for k in range(spg)], axis=-1).astype(F32)
        gg = y_ref[0, :, cols].astype(F32) * _silu(zg)
        ms = jnp.mean(gg * gg, axis=-1, keepdims=True)
        part = (gg * lax.rsqrt(ms + NORM_EPS) * nw_ref[:, cols]).astype(BF16)
        acc = acc + jnp.dot(part, w_ref[cols, :], preferred_element_type=F32)
    o_ref[0] = acc


def gated_outproj(h, y, z, nw, w, *, tm, group):
    b, tp, d = h.shape
    k = y.shape[-1]
    return pl.pallas_call(
        functools.partial(_gated_outproj_kernel, group=group),
        grid=(b, tp // tm),
        in_specs=[pl.BlockSpec((1, tm, d), lambda bi, i: (bi, i, 0)),
                  pl.BlockSpec((1, tm, k), lambda bi, i: (bi, i, 0)),
                  pl.BlockSpec((1, k // LANES, tm, LANES), lambda bi, i: (bi, 0, i, 0)),
                  pl.BlockSpec((1, k), lambda bi, i: (0, 0)),
                  pl.BlockSpec(w.shape, lambda bi, i: (0, 0))],
        out_specs=pl.BlockSpec((1, tm, d), lambda bi, i: (bi, i, 0)),
        out_shape=jax.ShapeDtypeStruct(h.shape, F32),
        compiler_params=_cparams(2),
        name="gated_outproj",
    )(h, y, z, nw, w)


def _rglru_kernel(x_ref, g_ref, wa_ref, ba_ref, wx_ref, bx_ref, lam_ref, o_ref,
                  xcp, a_s, b_s, hn, carry, *, ls):
    n_slab = x_ref.shape[1]
    slabs = [slice(LANES * s, LANES * (s + 1)) for s in range(n_slab)]

    @pl.when(pl.program_id(1) == 0)
    def _():
        carry[...] = jnp.zeros(carry.shape, F32)

    def gather_body(tau, c):
        for s in range(n_slab):
            xcp[_rows(tau), slabs[s]] = x_ref[0, s, pl.ds(tau, SUBLANES, stride=ls), :]
        return c

    lax.fori_loop(0, ls, gather_body, 0, unroll=LRU_UNROLL)

    xc = xcp[...]
    xb = xc.astype(BF16)
    r = _sigmoid(jnp.dot(xb, wa_ref[...], preferred_element_type=F32) + ba_ref[...])
    i = _sigmoid(jnp.dot(xb, wx_ref[...], preferred_element_type=F32) + bx_ref[...])
    log_a = (-LRU_C) * r * _softplus(-lam_ref[...])
    a = jnp.exp(log_a)
    a_s[...] = a
    v = -jnp.tanh(log_a) * (a * a + 1.0)
    b_s[...] = jnp.where(v > 0.0, v * lax.rsqrt(v), 0.0) * (i * xc)

    def pass1(tau, c):
        out = []
        for s in range(n_slab):
            p, e = c[2 * s], c[2 * s + 1]
            av = a_s[_rows(tau), slabs[s]]
            out += [p * av, av * e + b_s[_rows(tau), slabs[s]]]
        return tuple(out)

    one = jnp.ones((SUBLANES, LANES), F32)
    zero = jnp.zeros((SUBLANES, LANES), F32)
    pe = lax.fori_loop(0, ls, pass1, (one, zero) * n_slab, unroll=LRU_UNROLL)

    h0 = []
    for s in range(n_slab):
        p, e = pe[2 * s], pe[2 * s + 1]
        c = carry[0:1, slabs[s]]
        rows = []
        for j in range(SUBLANES):
            rows.append(c)
            c = p[j:j + 1] * c + e[j:j + 1]
        carry[0:1, slabs[s]] = c
        h0.append(jnp.concatenate(rows, axis=0))

    def pass2(tau, hs):
        out = []
        for s in range(n_slab):
            h = a_s[_rows(tau), slabs[s]] * hs[s] + b_s[_rows(tau), slabs[s]]
            hn[s, pl.ds(tau, SUBLANES, stride=ls), :] = h
            out.append(h)
        return tuple(out)

    lax.fori_loop(0, ls, pass2, tuple(h0), unroll=LRU_UNROLL)
    for s in range(n_slab):
        o_ref[0, s] = (hn[s] * _gelu_tanh(g_ref[0, s])).astype(o_ref.dtype)


def rglru(proj, wa_bd, ba, wx_bd, bx, lam, *, ls):
    b, _, tp, _ = proj.shape
    w = lam.shape[0]
    ns = w // LANES
    tt = SUBLANES * ls
    const = lambda shape: pl.BlockSpec(shape, lambda bi, t: (0,) * len(shape))
    return pl.pallas_call(
        functools.partial(_rglru_kernel, ls=ls),
        grid=(b, tp // tt),
        in_specs=[pl.BlockSpec((1, ns, tt, LANES), lambda bi, t: (bi, 0, t, 0)),
                  pl.BlockSpec((1, ns, tt, LANES), lambda bi, t: (bi, 1, t, 0)),
                  const((w, w)), const((1, w)), const((w, w)), const((1, w)), const((1, w))],
        out_specs=pl.BlockSpec((1, ns, tt, LANES), lambda bi, t: (bi, 0, t, 0)),
        out_shape=jax.ShapeDtypeStruct((b, ns, tp, LANES), BF16),
        scratch_shapes=[pltpu.VMEM((tt, w), F32), pltpu.VMEM((tt, w), F32), pltpu.VMEM((tt, w), F32),
                        pltpu.VMEM((ns, tt, LANES), F32), pltpu.VMEM((SUBLANES, w), F32)],
        compiler_params=_cparams(2),
        name="rglru",
    )(proj, proj, wa_bd, ba.reshape(1, w), wx_bd, bx.reshape(1, w), lam.reshape(1, w))


def _s5_kernel(u_ref, bend_ref, kc_ref, l8r_ref, l8i_ref, plr_ref, pli_ref, d_ref, wg_ref, bg_ref,
               o_ref, lp, st, yv, carry, *, ls):
    sc = S5_SC
    n_cs = l8r_ref.shape[1]
    stride = sc * ls
    cre = [slice(LANES * k, LANES * (k + 1)) for k in range(n_cs)]
    cim = [slice(LANES * (n_cs + k), LANES * (n_cs + k + 1)) for k in range(n_cs)]
    lanes = [slice(LANES * q, LANES * (q + 1)) for q in range(sc)]

    @pl.when(pl.program_id(2) == 0)
    def _():
        carry[...] = jnp.zeros(carry.shape, F32)

    def gather_body(tau, c):
        for sg in range(sc):
            lp[_rows(tau), lanes[sg]] = u_ref[0, 0, pl.ds(tau * sc + sg, SUBLANES, stride=stride), :]
        return c

    lax.fori_loop(0, ls, gather_body, 0, unroll=S5_UNROLL)
    st[...] =jnp.dot(lp[...].astype(BF16), bend_ref[0], preferred_element_type=F32)

    lam = [(jnp.broadcast_to(l8r_ref[0, k], (SUBLANES, LANES)),
            jnp.broadcast_to(l8i_ref[0, k], (SUBLANES, LANES))) for k in range(n_cs)]

    def step(tau, k, sr, si):
        lr, li = lam[k]
        return (lr * sr - li * si + st[_rows(tau), cre[k]], lr * si + li * sr + st[_rows(tau), cim[k]])

    def pass1(tau, c):
        out = []
        for k in range(n_cs):
            out += list(step(tau, k, c[2 * k], c[2 * k + 1]))
        return tuple(out)

    zero = jnp.zeros((SUBLANES, LANES), F32)
    ends = lax.fori_loop(0, ls, pass1, (zero,) * (2 * n_cs), unroll=S5_UNROLL)

    starts = []
    for k in range(n_cs):
        er, ei = ends[2 * k], ends[2 * k + 1]
        pr, pi = plr_ref[0, k], pli_ref[0, k]
        c_r, c_i = carry[0:1, cre[k]], carry[0:1, cim[k]]
        rows_r, rows_i = [], []
        for j in range(SUBLANES):
            rows_r.append(c_r)
            rows_i.append(c_i)
            c_r, c_i = (pr * c_r - pi * c_i + er[j:j + 1], pr * c_i + pi * c_r + ei[j:j + 1])
        carry[0:1, cre[k]] = c_r
        carry[0:1, cim[k]] = c_i
        starts += [jnp.concatenate(rows_r, axis=0), jnp.concatenate(rows_i, axis=0)]

    def pass2(tau, c):
        out = []
        for k in range(n_cs):
            nr, ni = step(tau, k, c[2 * k], c[2 * k + 1])
            st[_rows(tau), cre[k]] = c[2 * k]
            st[_rows(tau), cim[k]] = c[2 * k + 1]
            out += [nr, ni]
        return tuple(out)

    lax.fori_loop(0, ls, pass2, tuple(starts), unroll=S5_UNROLL)

    kin = lp.shape[1]
    lp_b, st_b = lp[...].astype(BF16), st[...].astype(BF16)
    tile = 2 * LANES
    for tcol in range(kin // tile):
        cols = slice(tile * tcol, tile * (tcol + 1))
        kk = tile * (tcol + 1)
        yv[:, cols] = (jnp.dot(lp_b[:, :kk], kc_ref[0, 0:kk, cols], preferred_element_type=F32)
                       + jnp.dot(st_b, kc_ref[0, kin:, cols], preferred_element_type=F32))
    for q in range(sc):
        y = yv[:, lanes[q]] + d_ref[0] * lp[:, lanes[q]]
        y = _gelu_tanh(y)
        yv[:, lanes[q]] = y * _sigmoid(jnp.dot(y.astype(BF16), wg_ref[0], preferred_element_type=F32)
                                       + bg_ref[0])

    def scatter_body(tau, c):
        for sg in range(sc):
            o_ref[0, 0, pl.ds(tau * sc + sg, SUBLANES, stride=stride), :] = yv[_rows(tau), lanes[sg]]
        return c

    lax.fori_loop(0, ls, scatter_body, 0, unroll=S5_UNROLL)


def s5(proj, first_slab, bend, kc, l8r, l8i, plr, pli, d, wg, bg, *, ls):
    b, _, tp, _ = proj.shape
    nblk, kin, nst = bend.shape
    n_cs = nst // (2 * LANES)
    tt = SUBLANES * ls * S5_SC
    nc = SUBLANES * ls
    per_blk = lambda shape: pl.BlockSpec((1,) + shape, lambda bi, gb, t: (gb,) + (0,) * len(shape))
    return pl.pallas_call(
        functools.partial(_s5_kernel, ls=ls),
        grid=(b, nblk, tp // tt),
        in_specs=[pl.BlockSpec((1, 1, tt, LANES), lambda bi, gb, t: (bi, first_slab + gb, t, 0)),
                  per_blk((kin, nst)), per_blk((kin + nst, kin)),
                  per_blk((n_cs, 1, LANES)), per_blk((n_cs, 1, LANES)),
                  per_blk((n_cs, 1, LANES)), per_blk((n_cs, 1, LANES)),
                  per_blk((1, LANES)), per_blk((LANES, LANES)), per_blk((1, LANES))],
        out_specs=pl.BlockSpec((1, 1, tt, LANES), lambda bi, gb, t: (bi, gb, t, 0)),
        out_shape=jax.ShapeDtypeStruct((b, nblk, tp, LANES), F32),
        scratch_shapes=[pltpu.VMEM((nc, kin), F32), pltpu.VMEM((nc, nst), F32),
                        pltpu.VMEM((nc, kin), F32), pltpu.VMEM((SUBLANES, nst), F32)],
        compiler_params=_cparams(3),
        name="s5",
    )(proj, bend, kc, l8r, l8i, plr, pli, d, wg, bg)


def _s5_params(a_re, a_im, b_re, b_im, c_re, c_im, d, log_dt, w_glu, b_glu, *, ls):
    g, p = a_re.shape
    sc = S5_SC
    gpb = LANES // S5_GROUP
    nblk = g // gpb
    dt = jnp.exp(log_dt)[:, None]
    mag = jnp.exp(a_re * dt)
    ar, ai = mag * jnp.cos(a_im * dt), mag * jnp.sin(a_im * dt)
    den = a_re * a_re + a_im * a_im
    fr = ((ar - 1.0) * a_re + ai * a_im) / den
    fi = (ai * a_re - (ar - 1.0) * a_im) / den
    bbar_re = fr[..., None] * b_re - fi[..., None] * b_im
    bbar_im = fr[..., None] * b_im + fi[..., None] * b_re

    def lam_pow(k):
        k = jnp.asarray(k, F32).reshape(-1, 1, 1)
        m = jnp.exp(a_re * dt * k)
        return m * jnp.cos(a_im * dt * k), m * jnp.sin(a_im * dt * k)

    pw_r, pw_i = lam_pow(jnp.arange(sc + 1))
    def group_diag(dense, rows_per_group):
        n = dense.shape[-1]
        tiled = jnp.tile(dense, (1,) * (dense.ndim - 1) + (gpb,))
        rg = lax.broadcasted_iota(jnp.int32, tiled.shape[-2:], 0) // rows_per_group
        cg = lax.broadcasted_iota(jnp.int32, tiled.shape[-2:], 1) // n
        return jnp.where(rg == cg, tiled, 0.0)

    def by_block(m):
        k, _, a, b2 = m.shape
        return jnp.transpose(m.reshape(k, nblk, gpb, a, b2), (0, 1, 2, 4, 3)).reshape(k, nblk, gpb * b2, a)

    wr = jnp.stack([pw_r[sc - 1 - s] for s in range(sc)])[..., None]
    wi = jnp.stack([pw_i[sc - 1 - s] for s in range(sc)])[..., None]
    e_re = wr * bbar_re - wi * bbar_im
    e_im = wr * bbar_im + wi * bbar_re
    rows_cat = lambda m: jnp.concatenate([m[k] for k in range(m.shape[0])], axis=-2)
    bend = jnp.concatenate([rows_cat(group_diag(by_block(e_re).astype(BF16), S5_GROUP)),
                            rows_cat(group_diag(by_block(e_im).astype(BF16), S5_GROUP))], axis=-1)

    cl_re = c_re[None] * pw_r[:, :, None, :] - c_im[None] * pw_i[:, :, None, :]
    cl_im = c_re[None] * pw_i[:, :, None, :] + c_im[None] * pw_r[:, :, None, :]
    kl = (jnp.einsum('kgip,gpj->kgij', cl_re[:sc], bbar_re)
          - jnp.einsum('kgip,gpj->kgij', cl_im[:sc], bbar_im))
    kd = group_diag(by_block(kl).astype(BF16), S5_GROUP)
    zero_blk = jnp.zeros_like(kd[0])
    kintra = jnp.concatenate(
        [jnp.concatenate([kd[t - s] if t >= s else zero_blk for t in range(sc)], axis=-1)
         for s in range(sc)], axis=-2)

    def out_bd(m):
        dense = jnp.transpose(m, (0, 1, 3, 2))
        d2 = group_diag(dense.reshape(sc, nblk, gpb * p, S5_GROUP).astype(BF16), p)
        return jnp.concatenate([d2[t] for t in range(sc)], axis=-1)

    kc = jnp.concatenate([kintra, out_bd(cl_re[1:]), out_bd(-cl_im[1:])], axis=-2)

    vec = lambda v: v.reshape(nblk, (gpb * p) // LANES, 1, LANES)
    pl_r, pl_i = lam_pow(jnp.asarray([sc * ls]))
    wg = group_diag(w_glu.reshape(nblk, LANES, S5_GROUP), S5_GROUP).astype(BF16)
    return (bend, kc, vec(pw_r[sc]), vec(pw_i[sc]), vec(pl_r[0]), vec(pl_i[0]),
            d.reshape(nblk, 1, LANES), wg, b_glu.reshape(nblk, 1, LANES))


def _ssd_kernel(xbc_ref, cs_ref, rowp_ref, wend_ref, cdec_ref, dskip_ref, o_ref, hst):
    @pl.when(pl.program_id(1) == 0)
    def _():
        hst[...] = jnp.zeros(hst.shape, F32)

    for c in range(xbc_ref.shape[2] // SSD_CHUNK):
        _ssd_chunk(xbc_ref, cs_ref, rowp_ref, wend_ref, cdec_ref, dskip_ref, o_ref, hst, c)


def _ssd_chunk(xbc_ref, cs_ref, rowp_ref, wend_ref, cdec_ref, dskip_ref, o_ref, hst, c):
    L = SSD_CHUNK
    rs = slice(L * c, L * (c + 1))
    n_groups = hst.shape[0]
    gw = SSD_HPG * SSD_HEAD_DIM
    spg = gw // LANES
    n_xs = n_groups * spg
    row = lax.broadcasted_iota(jnp.int32, (L, L), 0)
    col = lax.broadcasted_iota(jnp.int32, (L, L), 1)
    causal = row >= col
    left_half = col < SSD_HEAD_DIM
    head_of_lane = lax.broadcasted_iota(jnp.int32, (L, gw), 1) // SSD_HEAD_DIM

    def head_rows(ref, g):
        return jnp.concatenate(
            [jnp.broadcast_to(ref[0, c, SSD_HPG * g + r:SSD_HPG * g + r + 1, :], (SSD_HEAD_DIM, L))
             for r in range(SSD_HPG)], axis=0)

    def x_slabs(g):
        return jnp.concatenate([xbc_ref[0, spg * g + k, rs, :] for k in range(spg)], axis=-1)

    scores, y_offs = [], []
    for g in range(n_groups):
        bm = xbc_ref[0, n_xs + g, rs, :]
        cm = xbc_ref[0, n_xs + n_groups + g, rs, :]
        scores.append(lax.dot_general(cm, bm, (((1,), (1,)), ((), ())), preferred_element_type=F32))
        h_prev = hst[g]
        y_offs.append(lax.dot_general(cm, h_prev.astype(BF16), (((1,), (1,)), ((), ())),
                                      preferred_element_type=F32))
        xw_t = (x_slabs(g).astype(F32).T * head_rows(wend_ref, g)).astype(BF16)
        hst[g] = head_rows(cdec_ref, g) * h_prev + jnp.dot(xw_t, bm, preferred_element_type=F32)

    for g in range(n_groups):
        xsb = x_slabs(g)
        ms, bcs = [], []
        for r in range(SSD_HPG):
            h = SSD_HPG * g + r
            bcs.append(jnp.broadcast_to(cs_ref[0, c, :, h:h + 1], (L, L)))
            ms.append((scores[g] * jnp.exp2(jnp.where(causal, bcs[r] - rowp_ref[0, c, h:h + 1, :], -jnp.inf))
                       ).astype(BF16))
        x_bd = jnp.concatenate([jnp.where(head_of_lane == r, xsb, jnp.zeros_like(xsb))
                                for r in range(SSD_HPG)], axis=0)
        y_diag = jnp.dot(jnp.concatenate(ms, axis=1), x_bd, preferred_element_type=F32)
        f_start = jnp.concatenate([jnp.exp2(jnp.where(left_half, bcs[2 * k], bcs[2 * k + 1]))
                                   for k in range(spg)], axis=-1)
        o_ref[0, rs, gw * g:gw * (g + 1)] = (
            y_diag + y_offs[g] * f_start + dskip_ref[:, gw * g:gw * (g + 1)] * xsb.astype(F32)
        ).astype(o_ref.dtype)


def ssd_core(xbc, tables, dskip):
    b, n_conv, tp, _ = xbc.shape
    d_inner = dskip.shape[1]
    n_groups = d_inner // (SSD_HPG * SSD_HEAD_DIM)
    cps = SSD_CHUNKS_PER_STEP
    L = SSD_CHUNK * cps
    const = lambda shape: pl.BlockSpec(shape, lambda bi, c: (0,) * len(shape))
    tab = pl.BlockSpec((1, cps, SSD_CHUNK, LANES), lambda bi, c: (bi, c, 0, 0))
    return pl.pallas_call(
        _ssd_kernel,
        grid=(b, tp // L),
        in_specs=[pl.BlockSpec((1, n_conv, L, LANES), lambda bi, c: (bi, 0, c, 0)),
                  tab, tab, tab, tab, const((1, d_inner))],
        out_specs=pl.BlockSpec((1, L, d_inner), lambda bi, c: (bi, c, 0)),
        out_shape=jax.ShapeDtypeStruct((b, tp, d_inner), BF16),
        scratch_shapes=[pltpu.VMEM((n_groups, SSD_HPG * SSD_HEAD_DIM, SSD_STATE), F32)],
        compiler_params=_cparams(2),
        name="ssd_core",
    )(xbc, *tables, dskip)


def _block_diag(w):
    n, di, do = w.shape
    return jnp.einsum('gij,gh->gihj', w, jnp.eye(n, dtype=w.dtype)).reshape(n * di, n * do)


def _pad_lanes(v, fill=0.0):
    return jnp.pad(v, (0, LANES - v.shape[0]), constant_values=fill).reshape(1, LANES)


def kernel(x, meta_tokens, norm_mix, norm_mlp, norm_final, ev_w_in, lru_conv_w, lru_conv_b, lru_w_a, lru_b_a, lru_w_x, lru_b_x, lru_lambda, s5_a_re, s5_a_im, s5_b_re, s5_b_im, s5_c_re, s5_c_im, s5_d, s5_log_dt, s5_w_glu, s5_b_glu, ev_w_out, ssd_w_in, ssd_conv_w, ssd_conv_b, ssd_dt_bias, ssd_a_log, ssd_d, ssd_norm, ssd_w_out, mlp_w_up, mlp_w_down):
    bsz, seq, d = x.shape
    t = seq + N_META
    unit = math.lcm(SUBLANES * LRU_LS, SUBLANES * S5_LS * S5_SC, ROW_TILE, SSD_CHUNK)
    tp = -(-t // unit) * unit
    tm = ROW_TILE
    assert tp - t < tm <= seq
    meta = jnp.broadcast_to(meta_tokens[None].astype(x.dtype), (bsz, N_META, d))
    first = jnp.concatenate([meta, x[:, :tm - N_META]], axis=1)
    last = jnp.concatenate([x[:, tp - tm - N_META:], jnp.zeros((bsz, tp - t, d), x.dtype)], axis=1)

    lru_w = lru_conv_w.shape[-1]
    proj = norm_matmul(x, first, last, norm_mix[0], ev_w_in[0].astype(BF16), lru_conv_w[0], lru_conv_b[0],
                       tp=tp, tm=tm)
    y_lru = rglru(proj, _block_diag(lru_w_a[0]).astype(BF16), lru_b_a[0],
                  _block_diag(lru_w_x[0]).astype(BF16), lru_b_x[0], lru_lambda[0], ls=LRU_LS)
    y_s5 = s5(proj, 2 * lru_w // LANES,
              *_s5_params(s5_a_re[0], s5_a_im[0], s5_b_re[0], s5_b_im[0], s5_c_re[0], s5_c_im[0],
                          s5_d[0], s5_log_dt[0], s5_w_glu[0], s5_b_glu[0], ls=S5_LS), ls=S5_LS)
    h = outproj_slab(x, first, last, y_lru, y_s5, ev_w_out[0].astype(BF16), tm=tm)
    h = mlp_residual(h.reshape(bsz * tp, d), norm_mlp[0], mlp_w_up, mlp_w_down,
                     norm_final, layer=0, tm=MLP_ROW_TILE, tf=512,
                     final_norm=False).reshape(bsz, tp, d)

    d_inner = ssd_w_out.shape[1]
    conv_dim = ssd_conv_w.shape[-1]
    n_heads = ssd_dt_bias.shape[-1]
    w_in = ssd_w_in[0]
    w_zdt = jnp.concatenate([w_in[:, :d_inner], w_in[:, d_inner + conv_dim:],
                             jnp.zeros((d, LANES - n_heads), F32)], axis=1).astype(BF16)
    xbc = inproj_conv(h, norm_mix[1], w_in[:, d_inner:d_inner + conv_dim].astype(BF16),
                      ssd_conv_w[0], ssd_conv_b[0], tm=ROW_TILE, tn=2048)
    z, *tables = inproj_zdt(h, norm_mix[1], w_zdt, _pad_lanes(ssd_dt_bias[0]), _pad_lanes(ssd_a_log[0]),
                            tm=ROW_TILE)
    y = ssd_core(xbc, tables, jnp.repeat(ssd_d[0], SSD_HEAD_DIM).reshape(1, d_inner))
    h = gated_outproj(h, y, z, ssd_norm[0].reshape(1, d_inner), ssd_w_out[0].astype(BF16),
                      tm=GATED_ROW_TILE, group=SSD_HPG * SSD_HEAD_DIM)
    return mlp_residual_final(h, norm_mlp[1], mlp_w_up, mlp_w_down,
                              norm_final, layer=1, seq=seq, tm=FINAL_ROW_TILE, tf=512)
```

```python
import functools
import math

import jax
import jax.numpy as jnp
from jax import lax
from jax.experimental import pallas as pl
from jax.experimental.pallas import tpu as pltpu

F32 = jnp.float32
BF16 = jnp.bfloat16

LANES = 128
SUBLANES = 8
NORM_EPS = 1e-5
LOG2_E = 1.4426950408889634
N_META = 16
CONV_WIDTH = 4
LRU_C = 8.0
S5_GROUP = 16
S5_STATE = 64
SSD_HEAD_DIM = 64
SSD_STATE = 128
SSD_CHUNK = 128
SSD_HPG = 4
SSD_CHUNKS_PER_STEP = 5

LRU_LS = 130
LRU_UNROLL = 5
S5_SC = SUBLANES
S5_LS = 65
S5_UNROLL = 5
ROW_TILE = 1280
INPROJ_ROW_SPLIT = 5
MLP_ROW_TILE = 2080
GATED_ROW_TILE = 640
FINAL_ROW_TILE = 2048
VMEM_LIMIT = 48 * 1024 * 1024
VMEM_LIMIT_LARGE = 58 * 1024 * 1024


def _cparams(n_axes, vmem=VMEM_LIMIT):
    return pltpu.CompilerParams(dimension_semantics=("arbitrary",) * n_axes,
                                vmem_limit_bytes=vmem)


def _sigmoid(x):
    return 0.5 + 0.5 * jnp.tanh(0.5 * x)


def _silu(x):
    hx = 0.5 * x
    return hx + hx * jnp.tanh(hx)


def _gelu_tanh(x):
    return 0.5 * x * (1.0 + jnp.tanh(math.sqrt(2.0 / math.pi) * (x + 0.044715 * (x * x * x))))


def _log1p(e):
    u = 1.0 + e
    return jnp.where(u == 1.0, e, jnp.log(u) * (e / (u - 1.0)))


def _softplus(x):
    return jnp.maximum(x, 0.0) + _log1p(jnp.exp(-jnp.abs(x)))


def _rmsnorm_rows(x, w):
    ms = jnp.mean(x * x, axis=-1, keepdims=True)
    return x * lax.rsqrt(ms + NORM_EPS) * w


def _rows(tau):
    return pl.ds(pl.multiple_of(tau * SUBLANES, SUBLANES), SUBLANES)


def _seq_specs(seq, tm, d):
    x_spec = pl.BlockSpec(
        (pl.Element(1), pl.Element(tm), pl.Element(d)),
        lambda bi, i: (bi, pl.multiple_of(jnp.clip(i * tm - N_META, 0, seq - tm), SUBLANES), 0))
    edge_spec = pl.BlockSpec((1, tm, d), lambda bi, i: (bi, 0, 0), pipeline_mode=pl.Buffered(1))
    return [x_spec, edge_spec, edge_spec]


def _seq_tile(x_ref, first_ref, last_ref):
    i = pl.program_id(1)
    return jnp.where(i == 0, first_ref[0], jnp.where(i == pl.num_programs(1) - 1, last_ref[0], x_ref[0]))


def _norm_matmul_kernel(x_ref, first_ref, last_ref, nw_ref, w_ref, cw_ref, cb_ref, o_ref, rbuf, hist):
    tm = x_ref.shape[1]
    n_conv = rbuf.shape[0]
    hrows = SUBLANES

    @pl.when(pl.program_id(1) == 0)
    def _():
        hist[...] = jnp.zeros(hist.shape, F32)

    x = _seq_tile(x_ref, first_ref, last_ref)
    rs = lax.rsqrt(jnp.mean(x * x, axis=-1, keepdims=True) + NORM_EPS)
    xb = (x * nw_ref[...]).astype(BF16)
    step = 4
    for c in range(o_ref.shape[1] // step):
        r = jnp.dot(xb, w_ref[:, LANES * step * c:LANES * step * (c + 1)], preferred_element_type=F32) * rs
        for k in range(step):
            s = step * c + k
            if s >= n_conv:
                o_ref[0, s] = r[:, LANES * k:LANES * (k + 1)]
                continue
            sl = slice(LANES * s, LANES * (s + 1))
            rbuf[s, 0:hrows, :] = hist[s]
            rbuf[s, hrows:hrows + tm, :] = r[:, LANES * k:LANES * (k + 1)]
            hist[s] = rbuf[s, tm:tm + hrows, :]
            taps = [cw_ref[m, :, sl] for m in range(CONV_WIDTH)]
            bias = cb_ref[:, sl]
            for g in range(tm // SUBLANES):
                r0 = hrows + SUBLANES * g
                acc = bias
                for m in range(CONV_WIDTH):
                    acc = acc + taps[m] * rbuf[s, r0 - m:r0 - m + SUBLANES, :]
                o_ref[0, s, SUBLANES * g:SUBLANES * (g + 1), :] = acc


def norm_matmul(x, first, last, nw, w, cw, cb, *, tp, tm):
    b, seq, d = x.shape
    n = w.shape[1]
    wc = cw.shape[1]
    cw8 = jnp.broadcast_to(jnp.stack([cw[CONV_WIDTH - 1 - m] for m in range(CONV_WIDTH)])[:, None, :],
                           (CONV_WIDTH, SUBLANES, wc))
    cb8 = jnp.broadcast_to(cb[None, :], (SUBLANES, wc))
    const = lambda shape: pl.BlockSpec(shape, lambda bi, i: (0,) * len(shape))
    return pl.pallas_call(
        _norm_matmul_kernel,
        grid=(b, tp // tm),
        in_specs=_seq_specs(seq, tm, d) + [
            const((1, d)), const((d, n)), const((CONV_WIDTH, SUBLANES, wc)), const((SUBLANES, wc))],
        out_specs=pl.BlockSpec((1, n // LANES, tm, LANES), lambda bi, i: (bi, 0, i, 0)),
        out_shape=jax.ShapeDtypeStruct((b, n // LANES, tp, LANES), F32),
        scratch_shapes=[pltpu.VMEM((wc // LANES, SUBLANES + tm, LANES), F32),
                        pltpu.VMEM((wc // LANES, SUBLANES, LANES), F32)],
        compiler_params=_cparams(2),
        name="norm_matmul",
    )(x, first, last, nw.reshape(1, d), w, cw8, cb8)


def _inproj_conv_kernel(x_ref, nw_ref, w_ref, cw_ref, cb_ref, o_ref, xn_ref, rs, hist):
    i, j = pl.program_id(1), pl.program_id(2)
    tm = x_ref.shape[1]
    n_slab = o_ref.shape[1]
    hrows = SUBLANES
    pack = 2 * SUBLANES

    @pl.when(j == 0)
    def _():
        xn_ref[...] = _rmsnorm_rows(x_ref[0], nw_ref[...]).astype(BF16)

    @pl.when(i == 0)
    def _():
        hist[j] = jnp.zeros(hist.shape[1:], F32)

    n_split = INPROJ_ROW_SPLIT
    hm = tm // n_split
    for c in range(n_slab // 2):
        wc = w_ref[:, 2 * LANES * c:2 * LANES * (c + 1)]
        parts = [jnp.dot(xn_ref[hm * q:hm * (q + 1), :], wc, preferred_element_type=F32)
                 for q in range(n_split)]
        for k in range(2):
            s = 2 * c + k
            sl = slice(LANES * s, LANES * (s + 1))
            rs[s, 0:hrows, :] = hist[j, s]
            for q in range(n_split):
                rs[s, hrows + hm * q:hrows + hm * (q + 1), :] = parts[q][:, LANES * k:LANES * (k + 1)]
            hist[j, s] = rs[s, tm:tm + hrows, :]
            taps = [cw_ref[m, :, sl] for m in range(CONV_WIDTH)]
            bias = cb_ref[:, sl]
            for g in range(tm // pack):
                halves = []
                for hh in range(pack // SUBLANES):
                    r0 = hrows + pack * g + SUBLANES * hh
                    acc = bias
                    for m in range(CONV_WIDTH):
                        acc = acc + taps[m] * rs[s, r0 - m:r0 - m + SUBLANES, :]
                    halves.append(acc)
                o_ref[0, s, pack * g:pack * (g + 1), :] = _silu(jnp.concatenate(halves, axis=0).astype(BF16))


def inproj_conv(h, nw, w, cw, cb, *, tm, tn):
    b, tp, d = h.shape
    n = w.shape[1]
    ns = tn // LANES
    cw8 = jnp.broadcast_to(jnp.stack([cw[CONV_WIDTH - 1 - m] for m in range(CONV_WIDTH)])[:, None, :],
                           (CONV_WIDTH, SUBLANES, n))
    cb8 = jnp.broadcast_to(cb.reshape(1, n), (SUBLANES, n))
    return pl.pallas_call(
        _inproj_conv_kernel,
        grid=(b, tp // tm, n // tn),
        in_specs=[pl.BlockSpec((1, tm, d), lambda bi, i, j: (bi, i, 0)),
                  pl.BlockSpec((1, d), lambda bi, i, j: (0, 0)),
                  pl.BlockSpec((d, tn), lambda bi, i, j: (0, j)),
                  pl.BlockSpec((CONV_WIDTH, SUBLANES, tn), lambda bi, i, j: (0, 0, j)),
                  pl.BlockSpec((SUBLANES, tn), lambda bi, i, j: (0, j))],
        out_specs=pl.BlockSpec((1, ns, tm, LANES), lambda bi, i, j: (bi, j, i, 0)),
        out_shape=jax.ShapeDtypeStruct((b, n // LANES, tp, LANES), BF16),
        scratch_shapes=[pltpu.VMEM((tm, d), BF16),
                        pltpu.VMEM((ns, SUBLANES + tm, LANES), F32),
                        pltpu.VMEM((n // tn, ns, SUBLANES, LANES), F32)],
        compiler_params=_cparams(3),
        name="inproj_conv",
    )(h, nw.reshape(1, d), w, cw8, cb8)


def _ssd_tables(dt_raw, dtb, a, tri):
    L = SSD_CHUNK
    dt = _softplus(dt_raw + dtb)
    da = dt * a
    da_hi = da.astype(BF16)
    r1 = da - da_hi.astype(F32)
    da_mid = r1.astype(BF16)
    da_lo = (r1 - da_mid.astype(F32)).astype(BF16)
    cs = (jnp.dot(tri, da_hi, preferred_element_type=F32)
          + jnp.dot(tri, da_mid, preferred_element_type=F32)
          + jnp.dot(tri, da_lo, preferred_element_type=F32))
    dt_t = dt.T
    cs_t = cs.T
    last_t = jnp.broadcast_to(cs_t[:, L - 1:L], (L, L))
    return cs * LOG2_E, (cs_t - jnp.log(dt_t)) * LOG2_E, dt_t * jnp.exp(last_t - cs_t), jnp.exp(last_t)


def _inproj_zdt_kernel(x_ref, nw_ref, w_ref, dtb_ref, alog_ref, z_ref, cs_ref, rowp_ref, wend_ref, cdec_ref):
    x = x_ref[0]
    rs = lax.rsqrt(jnp.mean(x * x, axis=-1, keepdims=True) + NORM_EPS)
    xb = (x * nw_ref[...]).astype(BF16)
    n_z = z_ref.shape[1]
    dt_raw = jnp.dot(xb, w_ref[:, LANES * n_z:LANES * (n_z + 1)], preferred_element_type=F32) * rs
    L = SSD_CHUNK
    tri = (lax.broadcasted_iota(jnp.int32, (L, L), 0) >= lax.broadcasted_iota(jnp.int32, (L, L), 1)).astype(BF16)
    a = -jnp.exp(alog_ref[...])
    step = 4
    n_chunks = cs_ref.shape[1]
    for c in range(n_z // step):
        r = jnp.dot(xb, w_ref[:, LANES * step * c:LANES * step * (c + 1)], preferred_element_type=F32) * rs
        for k in range(step):
            z_ref[0, step * c + k] = r[:, LANES * k:LANES * (k + 1)].astype(BF16)
        for q in range(c * n_chunks // (n_z // step), (c + 1) * n_chunks // (n_z // step)):
            cs_ref[0, q], rowp_ref[0, q], wend_ref[0, q], cdec_ref[0, q] = _ssd_tables(
                dt_raw[L * q:L * (q + 1)], dtb_ref[...], a, tri)


def inproj_zdt(h, nw, w, dtb, alog, *, tm):
    b, tp, d = h.shape
    n_z = w.shape[1] // LANES - 1
    L = SSD_CHUNK
    const = lambda shape: pl.BlockSpec(shape, lambda bi, i: (0,) * len(shape))
    tab = pl.BlockSpec((1, tm // L, L, LANES), lambda bi, i: (bi, i, 0, 0))
    return pl.pallas_call(
        _inproj_zdt_kernel,
        grid=(b, tp // tm),
        in_specs=[pl.BlockSpec((1, tm, d), lambda bi, i: (bi, i, 0)),
                  const((1, d)), const(w.shape), const((1, LANES)), const((1, LANES))],
        out_specs=[pl.BlockSpec((1, n_z, tm, LANES), lambda bi, i: (bi, 0, i, 0)), tab, tab, tab, tab],
        out_shape=[jax.ShapeDtypeStruct((b, n_z, tp, LANES), BF16)]
        + [jax.ShapeDtypeStruct((b, tp // L, L, LANES), F32)] * 4,
        compiler_params=_cparams(2),
        name="inproj_zdt",
    )(h, nw.reshape(1, d), w, dtb, alog)


def _mlp_kernel(x_ref, nw_ref, wu_ref, wd_ref, fw_ref, o_ref, xn_ref, *, final_norm):
    j = pl.program_id(1)

    @pl.when(j == 0)
    def _():
        x = x_ref[...]
        xn_ref[...] = _rmsnorm_rows(x, nw_ref[...]).astype(BF16)
        o_ref[...] = x

    u = jnp.dot(xn_ref[...], wu_ref[...].astype(BF16), preferred_element_type=F32)
    a = jnp.square(jnp.maximum(u, 0.0)).astype(BF16)
    o_ref[...] += jnp.dot(a, wd_ref[...].astype(BF16), preferred_element_type=F32)

    if final_norm:
        @pl.when(j == pl.num_programs(1) - 1)
        def _():
            o_ref[...] = _rmsnorm_rows(o_ref[...], fw_ref[...])


def mlp_residual(h2, nw, w_up, w_down, fw, *, layer, tm, tf, final_norm):
    n, d = h2.shape
    dff = w_up.shape[2]
    return pl.pallas_call(
        functools.partial(_mlp_kernel, final_norm=final_norm),
        grid=(n // tm, dff // tf),
        in_specs=[pl.BlockSpec((tm, d), lambda i, j: (i, 0)),
                  pl.BlockSpec((1, d), lambda i, j: (0, 0)),
                  pl.BlockSpec((None, d, tf), lambda i, j: (layer, 0, j)),
                  pl.BlockSpec((None, tf, d), lambda i, j: (layer, j, 0)),
                  pl.BlockSpec((1, d), lambda i, j: (0, 0))],
        out_specs=pl.BlockSpec((tm, d), lambda i, j: (i, 0)),
        out_shape=jax.ShapeDtypeStruct((n, d), F32),
        scratch_shapes=[pltpu.VMEM((tm, d), BF16)],
        compiler_params=_cparams(2, vmem=VMEM_LIMIT_LARGE),
        name="mlp_final" if final_norm else "mlp",
    )(h2, nw.reshape(1, d), w_up, w_down, fw.reshape(1, d))


def _mlp_final_kernel(x_ref, nw_ref, wu_ref, wd_ref, fw_ref, o_ref, xn_ref):
    j = pl.program_id(2)

    @pl.when(j == 0)
    def _():
        x = x_ref[0]
        xn_ref[...] = _rmsnorm_rows(x, nw_ref[...]).astype(BF16)
        o_ref[0] = x

    u = jnp.dot(xn_ref[...], wu_ref[...].astype(BF16), preferred_element_type=F32)
    a = jnp.square(jnp.maximum(u, 0.0)).astype(BF16)
    o_ref[0] += jnp.dot(a, wd_ref[...].astype(BF16), preferred_element_type=F32)

    @pl.when(j == pl.num_programs(2) - 1)
    def _():
        o_ref[0] = _rmsnorm_rows(o_ref[0], fw_ref[...])


def mlp_residual_final(h, nw, w_up, w_down, fw, *, layer, seq, tm, tf):
    b, _, d = h.shape
    dff = w_up.shape[2]
    return pl.pallas_call(
        _mlp_final_kernel,
        grid=(b, seq // tm, dff // tf),
        in_specs=[pl.BlockSpec((pl.Element(1), pl.Element(tm), pl.Element(d)),
                               lambda bi, i, j: (bi, pl.multiple_of(N_META + i * tm, SUBLANES), 0)),
                  pl.BlockSpec((1, d), lambda bi, i, j: (0, 0)),
                  pl.BlockSpec((None, d, tf), lambda bi, i, j: (layer, 0, j)),
                  pl.BlockSpec((None, tf, d), lambda bi, i, j: (layer, j, 0)),
                  pl.BlockSpec((1, d), lambda bi, i, j: (0, 0))],
        out_specs=pl.BlockSpec((1, tm, d), lambda bi, i, j: (bi, i, 0)),
        out_shape=jax.ShapeDtypeStruct((b, seq, d), F32),
        scratch_shapes=[pltpu.VMEM((tm, d), BF16)],
        compiler_params=_cparams(3, vmem=VMEM_LIMIT_LARGE),
        name="mlp_final",
    )(h, nw.reshape(1, d), w_up, w_down, fw.reshape(1, d))


def _outproj_slab_kernel(x_ref, first_ref, last_ref, ya_ref, yb_ref, w_ref, o_ref):
    parts = ([ya_ref[0, s].astype(BF16) for s in range(ya_ref.shape[1])]
             + [yb_ref[0, s].astype(BF16) for s in range(yb_ref.shape[1])])
    y = jnp.concatenate(parts, axis=-1)
    o_ref[0] = _seq_tile(x_ref, first_ref, last_ref) + jnp.dot(y, w_ref[...], preferred_element_type=F32)


def outproj_slab(x, first, last, ya, yb, w, *, tm):
    b, seq, d = x.shape
    sa, sb, tp = ya.shape[1], yb.shape[1], ya.shape[2]
    return pl.pallas_call(
        _outproj_slab_kernel,
        grid=(b, tp // tm),
        in_specs=_seq_specs(seq, tm, d) + [
            pl.BlockSpec((1, sa, tm, LANES), lambda bi, i: (bi, 0, i, 0)),
            pl.BlockSpec((1, sb, tm, LANES), lambda bi, i: (bi, 0, i, 0)),
            pl.BlockSpec(w.shape, lambda bi, i: (0, 0))],
        out_specs=pl.BlockSpec((1, tm, d), lambda bi, i: (bi, i, 0)),
        out_shape=jax.ShapeDtypeStruct((b, tp, d), F32),
        compiler_params=_cparams(2),
        name="outproj_slab",
    )(x, first, last, ya, yb, w)


def _gated_outproj_kernel(h_ref, y_ref, z_ref, nw_ref, w_ref, o_ref, *, group):
    acc = h_ref[0]
    spg = group // LANES
    for g in range(y_ref.shape[2] // group):
        cols = slice(group * g, group * (g + 1))
        zg = jnp.concatenate([z_ref[0, spg * g + k] for k in range(spg)], axis=-1).astype(F32)
        gg = y_ref[0, :, cols].astype(F32) * _silu(zg)
        ms = jnp.mean(gg * gg, axis=-1, keepdims=True)
        part = (gg * lax.rsqrt(ms + NORM_EPS) * nw_ref[:, cols]).astype(BF16)
        acc = acc + jnp.dot(part, w_ref[cols, :], preferred_element_type=F32)
    o_ref[0] = acc


def gated_outproj(h, y, z, nw, w, *, tm, group):
    b, tp, d = h.shape
    k = y.shape[-1]
    return pl.pallas_call(
        functools.partial(_gated_outproj_kernel, group=group),
        grid=(b, tp // tm),
        in_specs=[pl.BlockSpec((1, tm, d), lambda bi, i: (bi, i, 0)),
                  pl.BlockSpec((1, tm, k), lambda bi, i: (bi, i, 0)),
                  pl.BlockSpec((1, k // LANES, tm, LANES), lambda bi, i: (bi, 0, i, 0)),
                  pl.BlockSpec((1, k), lambda bi, i: (0, 0)),
                  pl.BlockSpec(w.shape, lambda bi, i: (0, 0))],
        out_specs=pl.BlockSpec((1, tm, d), lambda bi, i: (bi, i, 0)),
        out_shape=jax.ShapeDtypeStruct(h.shape, F32),
        compiler_params=_cparams(2),
        name="gated_outproj",
    )(h, y, z, nw, w)


def _rglru_kernel(x_ref, g_ref, wa_ref, ba_ref, wx_ref, bx_ref, lam_ref, o_ref,
                  xcp, a_s, b_s, hn, carry, *, ls):
    n_slab = x_ref.shape[1]
    slabs = [slice(LANES * s, LANES * (s + 1)) for s in range(n_slab)]

    @pl.when(pl.program_id(1) == 0)
    def _():
        carry[...] = jnp.zeros(carry.shape, F32)

    def gather_body(tau, c):
        for s in range(n_slab):
            xcp[_rows(tau), slabs[s]] = x_ref[0, s, pl.ds(tau, SUBLANES, stride=ls), :]
        return c

    lax.fori_loop(0, ls, gather_body, 0, unroll=LRU_UNROLL)

    xc = xcp[...]
    xb = xc.astype(BF16)
    r = _sigmoid(jnp.dot(xb, wa_ref[...], preferred_element_type=F32) + ba_ref[...])
    i = _sigmoid(jnp.dot(xb, wx_ref[...], preferred_element_type=F32) + bx_ref[...])
    log_a = (-LRU_C) * r * _softplus(-lam_ref[...])
    a = jnp.exp(log_a)
    a_s[...] = a
    v = -jnp.tanh(log_a) * (a * a + 1.0)
    b_s[...] = jnp.where(v > 0.0, v * lax.rsqrt(v), 0.0) * (i * xc)

    def pass1(tau, c):
        out = []
        for s in range(n_slab):
            p, e = c[2 * s], c[2 * s + 1]
            av = a_s[_rows(tau), slabs[s]]
            out += [p * av, av * e + b_s[_rows(tau), slabs[s]]]
        return tuple(out)

    one = jnp.ones((SUBLANES, LANES), F32)
    zero = jnp.zeros((SUBLANES, LANES), F32)
    pe = lax.fori_loop(0, ls, pass1, (one, zero) * n_slab, unroll=LRU_UNROLL)

    h0 = []
    for s in range(n_slab):
        p, e = pe[2 * s], pe[2 * s + 1]
        c = carry[0:1, slabs[s]]
        rows = []
        for j in range(SUBLANES):
            rows.append(c)
            c = p[j:j + 1] * c + e[j:j + 1]
        carry[0:1, slabs[s]] = c
        h0.append(jnp.concatenate(rows, axis=0))

    def pass2(tau, hs):
        out = []
        for s in range(n_slab):
            h = a_s[_rows(tau), slabs[s]] * hs[s] + b_s[_rows(tau), slabs[s]]
            hn[s, pl.ds(tau, SUBLANES, stride=ls), :] = h
            out.append(h)
        return tuple(out)

    lax.fori_loop(0, ls, pass2, tuple(h0), unroll=LRU_UNROLL)
    for s in range(n_slab):
        o_ref[0, s] = (hn[s] * _gelu_tanh(g_ref[0, s])).astype(o_ref.dtype)


def rglru(proj, wa_bd, ba, wx_bd, bx, lam, *, ls):
    b, _, tp, _ = proj.shape
    w = lam.shape[0]
    ns = w // LANES
    tt = SUBLANES * ls
    const = lambda shape: pl.BlockSpec(shape, lambda bi, t: (0,) * len(shape))
    return pl.pallas_call(
        functools.partial(_rglru_kernel, ls=ls),
        grid=(b, tp // tt),
        in_specs=[pl.BlockSpec((1, ns, tt, LANES), lambda bi, t: (bi, 0, t, 0)),
                  pl.BlockSpec((1, ns, tt, LANES), lambda bi, t: (bi, 1, t, 0)),
                  const((w, w)), const((1, w)), const((w, w)), const((1, w)), const((1, w))],
        out_specs=pl.BlockSpec((1, ns, tt, LANES), lambda bi, t: (bi, 0, t, 0)),
        out_shape=jax.ShapeDtypeStruct((b, ns, tp, LANES), BF16),
        scratch_shapes=[pltpu.VMEM((tt, w), F32), pltpu.VMEM((tt, w), F32), pltpu.VMEM((tt, w), F32),
                        pltpu.VMEM((ns, tt, LANES), F32), pltpu.VMEM((SUBLANES, w), F32)],
        compiler_params=_cparams(2),
        name="rglru",
    )(proj, proj, wa_bd, ba.reshape(1, w), wx_bd, bx.reshape(1, w), lam.reshape(1, w))


def _s5_kernel(u_ref, bend_ref, kc_ref, l8r_ref, l8i_ref, plr_ref, pli_ref, d_ref, wg_ref, bg_ref,
               o_ref, lp, st, yv, carry, *, ls):
    sc = S5_SC
    n_cs = l8r_ref.shape[1]
    stride = sc * ls
    cre = [slice(LANES * k, LANES * (k + 1)) for k in range(n_cs)]
    cim = [slice(LANES * (n_cs + k), LANES * (n_cs + k + 1)) for k in range(n_cs)]
    lanes = [slice(LANES * q, LANES * (q + 1)) for q in range(sc)]

    @pl.when(pl.program_id(2) == 0)
    def _():
        carry[...] = jnp.zeros(carry.shape, F32)

    def gather_body(tau, c):
        for sg in range(sc):
            lp[_rows(tau), lanes[sg]] = u_ref[0, 0, pl.ds(tau * sc + sg, SUBLANES, stride=stride), :]
        return c

    lax.fori_loop(0, ls, gather_body, 0, unroll=S5_UNROLL)
    st[...] =jnp.dot(lp[...].astype(BF16), bend_ref[0], preferred_element_type=F32)

    lam = [(jnp.broadcast_to(l8r_ref[0, k], (SUBLANES, LANES)),
            jnp.broadcast_to(l8i_ref[0, k], (SUBLANES, LANES))) for k in range(n_cs)]

    def step(tau, k, sr, si):
        lr, li = lam[k]
        return (lr * sr - li * si + st[_rows(tau), cre[k]], lr * si + li * sr + st[_rows(tau), cim[k]])

    def pass1(tau, c):
        out = []
        for k in range(n_cs):
            out += list(step(tau, k, c[2 * k], c[2 * k + 1]))
        return tuple(out)

    zero = jnp.zeros((SUBLANES, LANES), F32)
    ends = lax.fori_loop(0, ls, pass1, (zero,) * (2 * n_cs), unroll=S5_UNROLL)

    starts = []
    for k in range(n_cs):
        er, ei = ends[2 * k], ends[2 * k + 1]
        pr, pi = plr_ref[0, k], pli_ref[0, k]
        c_r, c_i = carry[0:1, cre[k]], carry[0:1, cim[k]]
        rows_r, rows_i = [], []
        for j in range(SUBLANES):
            rows_r.append(c_r)
            rows_i.append(c_i)
            c_r, c_i = (pr * c_r - pi * c_i + er[j:j + 1], pr * c_i + pi * c_r + ei[j:j + 1])
        carry[0:1, cre[k]] = c_r
        carry[0:1, cim[k]] = c_i
        starts += [jnp.concatenate(rows_r, axis=0), jnp.concatenate(rows_i, axis=0)]

    def pass2(tau, c):
        out = []
        for k in range(n_cs):
            nr, ni = step(tau, k, c[2 * k], c[2 * k + 1])
            st[_rows(tau), cre[k]] = c[2 * k]
            st[_rows(tau), cim[k]] = c[2 * k + 1]
            out += [nr, ni]
        return tuple(out)

    lax.fori_loop(0, ls, pass2, tuple(starts), unroll=S5_UNROLL)

    kin = lp.shape[1]
    lp_b, st_b = lp[...].astype(BF16), st[...].astype(BF16)
    tile = 2 * LANES
    for tcol in range(kin // tile):
        cols = slice(tile * tcol, tile * (tcol + 1))
        kk = tile * (tcol + 1)
        yv[:, cols] = (jnp.dot(lp_b[:, :kk], kc_ref[0, 0:kk, cols], preferred_element_type=F32)
                       + jnp.dot(st_b, kc_ref[0, kin:, cols], preferred_element_type=F32))
    for q in range(sc):
        y = yv[:, lanes[q]] + d_ref[0] * lp[:, lanes[q]]
        y = _gelu_tanh(y)
        yv[:, lanes[q]] = y * _sigmoid(jnp.dot(y.astype(BF16), wg_ref[0], preferred_element_type=F32)
                                       + bg_ref[0])

    def scatter_body(tau, c):
        for sg in range(sc):
            o_ref[0, 0, pl.ds(tau * sc + sg, SUBLANES, stride=stride), :] = yv[_rows(tau), lanes[sg]]
        return c

    lax.fori_loop(0, ls, scatter_body, 0, unroll=S5_UNROLL)


def s5(proj, first_slab, bend, kc, l8r, l8i, plr, pli, d, wg, bg, *, ls):
    b, _, tp, _ = proj.shape
    nblk, kin, nst = bend.shape
    n_cs = nst // (2 * LANES)
    tt = SUBLANES * ls * S5_SC
    nc = SUBLANES * ls
    per_blk = lambda shape: pl.BlockSpec((1,) + shape, lambda bi, gb, t: (gb,) + (0,) * len(shape))
    return pl.pallas_call(
        functools.partial(_s5_kernel, ls=ls),
        grid=(b, nblk, tp // tt),
        in_specs=[pl.BlockSpec((1, 1, tt, LANES), lambda bi, gb, t: (bi, first_slab + gb, t, 0)),
                  per_blk((kin, nst)), per_blk((kin + nst, kin)),
                  per_blk((n_cs, 1, LANES)), per_blk((n_cs, 1, LANES)),
                  per_blk((n_cs, 1, LANES)), per_blk((n_cs, 1, LANES)),
                  per_blk((1, LANES)), per_blk((LANES, LANES)), per_blk((1, LANES))],
        out_specs=pl.BlockSpec((1, 1, tt, LANES), lambda bi, gb, t: (bi, gb, t, 0)),
        out_shape=jax.ShapeDtypeStruct((b, nblk, tp, LANES), F32),
        scratch_shapes=[pltpu.VMEM((nc, kin), F32), pltpu.VMEM((nc, nst), F32),
                        pltpu.VMEM((nc, kin), F32), pltpu.VMEM((SUBLANES, nst), F32)],
        compiler_params=_cparams(3),
        name="s5",
    )(proj, bend, kc, l8r, l8i, plr, pli, d, wg, bg)


def _s5_params(a_re, a_im, b_re, b_im, c_re, c_im, d, log_dt, w_glu, b_glu, *, ls):
    g, p = a_re.shape
    sc = S5_SC
    gpb = LANES // S5_GROUP
    nblk = g // gpb
    dt = jnp.exp(log_dt)[:, None]
    mag = jnp.exp(a_re * dt)
    ar, ai = mag * jnp.cos(a_im * dt), mag * jnp.sin(a_im * dt)
    den = a_re * a_re + a_im * a_im
    fr = ((ar - 1.0) * a_re + ai * a_im) / den
    fi = (ai * a_re - (ar - 1.0) * a_im) / den
    bbar_re = fr[..., None] * b_re - fi[..., None] * b_im
    bbar_im = fr[..., None] * b_im + fi[..., None] * b_re

    def lam_pow(k):
        k = jnp.asarray(k, F32).reshape(-1, 1, 1)
        m = jnp.exp(a_re * dt * k)
        return m * jnp.cos(a_im * dt * k), m * jnp.sin(a_im * dt * k)

    pw_r, pw_i = lam_pow(jnp.arange(sc + 1))
    def group_diag(dense, rows_per_group):
        n = dense.shape[-1]
        tiled = jnp.tile(dense, (1,) * (dense.ndim - 1) + (gpb,))
        rg = lax.broadcasted_iota(jnp.int32, tiled.shape[-2:], 0) // rows_per_group
        cg = lax.broadcasted_iota(jnp.int32, tiled.shape[-2:], 1) // n
        return jnp.where(rg == cg, tiled, 0.0)

    def by_block(m):
        k, _, a, b2 = m.shape
        return jnp.transpose(m.reshape(k, nblk, gpb, a, b2), (0, 1, 2, 4, 3)).reshape(k, nblk, gpb * b2, a)

    wr = jnp.stack([pw_r[sc - 1 - s] for s in range(sc)])[..., None]
    wi = jnp.stack([pw_i[sc - 1 - s] for s in range(sc)])[..., None]
    e_re = wr * bbar_re - wi * bbar_im
    e_im = wr * bbar_im + wi * bbar_re
    rows_cat = lambda m: jnp.concatenate([m[k] for k in range(m.shape[0])], axis=-2)
    bend = jnp.concatenate([rows_cat(group_diag(by_block(e_re).astype(BF16), S5_GROUP)),
                            rows_cat(group_diag(by_block(e_im).astype(BF16), S5_GROUP))], axis=-1)

    cl_re = c_re[None] * pw_r[:, :, None, :] - c_im[None] * pw_i[:, :, None, :]
    cl_im = c_re[None] * pw_i[:, :, None, :] + c_im[None] * pw_r[:, :, None, :]
    kl = (jnp.einsum('kgip,gpj->kgij', cl_re[:sc], bbar_re)
          - jnp.einsum('kgip,gpj->kgij', cl_im[:sc], bbar_im))
    kd = group_diag(by_block(kl).astype(BF16), S5_GROUP)
    zero_blk = jnp.zeros_like(kd[0])
    kintra = jnp.concatenate(
        [jnp.concatenate([kd[t - s] if t >= s else zero_blk for t in range(sc)], axis=-1)
         for s in range(sc)], axis=-2)

    def out_bd(m):
        dense = jnp.transpose(m, (0, 1, 3, 2))
        d2 = group_diag(dense.reshape(sc, nblk, gpb * p, S5_GROUP).astype(BF16), p)
        return jnp.concatenate([d2[t] for t in range(sc)], axis=-1)

    kc = jnp.concatenate([kintra, out_bd(cl_re[1:]), out_bd(-cl_im[1:])], axis=-2)

    vec = lambda v: v.reshape(nblk, (gpb * p) // LANES, 1, LANES)
    pl_r, pl_i = lam_pow(jnp.asarray([sc * ls]))
    wg = group_diag(w_glu.reshape(nblk, LANES, S5_GROUP), S5_GROUP).astype(BF16)
    return (bend, kc, vec(pw_r[sc]), vec(pw_i[sc]), vec(pl_r[0]), vec(pl_i[0]),
            d.reshape(nblk, 1, LANES), wg, b_glu.reshape(nblk, 1, LANES))


def _ssd_kernel(xbc_ref, cs_ref, rowp_ref, wend_ref, cdec_ref, dskip_ref, o_ref, hst):
    @pl.when(pl.program_id(1) == 0)
    def _():
        hst[...] = jnp.zeros(hst.shape, F32)

    for c in range(xbc_ref.shape[2] // SSD_CHUNK):
        _ssd_chunk(xbc_ref, cs_ref, rowp_ref, wend_ref, cdec_ref, dskip_ref, o_ref, hst, c)


def _ssd_chunk(xbc_ref, cs_ref, rowp_ref, wend_ref, cdec_ref, dskip_ref, o_ref, hst, c):
    L = SSD_CHUNK
    rs = slice(L * c, L * (c + 1))
    n_groups = hst.shape[0]
    gw = SSD_HPG * SSD_HEAD_DIM
    spg = gw // LANES
    n_xs = n_groups * spg
    row = lax.broadcasted_iota(jnp.int32, (L, L), 0)
    col = lax.broadcasted_iota(jnp.int32, (L, L), 1)
    causal = row >= col
    left_half = col < SSD_HEAD_DIM
    head_of_lane = lax.broadcasted_iota(jnp.int32, (L, gw), 1) // SSD_HEAD_DIM

    def head_rows(ref, g):
        return jnp.concatenate(
            [jnp.broadcast_to(ref[0, c, SSD_HPG * g + r:SSD_HPG * g + r + 1, :], (SSD_HEAD_DIM, L))
             for r in range(SSD_HPG)], axis=0)

    def x_slabs(g):
        return jnp.concatenate([xbc_ref[0, spg * g + k, rs, :] for k in range(spg)], axis=-1)

    scores, y_offs = [], []
    for g in range(n_groups):
        bm = xbc_ref[0, n_xs + g, rs, :]
        cm = xbc_ref[0, n_xs + n_groups + g, rs, :]
        scores.append(lax.dot_general(cm, bm, (((1,), (1,)), ((), ())), preferred_element_type=F32))
        h_prev = hst[g]
        y_offs.append(lax.dot_general(cm, h_prev.astype(BF16), (((1,), (1,)), ((), ())),
                                      preferred_element_type=F32))
        xw_t = (x_slabs(g).astype(F32).T * head_rows(wend_ref, g)).astype(BF16)
        hst[g] = head_rows(cdec_ref, g) * h_prev + jnp.dot(xw_t, bm, preferred_element_type=F32)

    for g in range(n_groups):
        xsb = x_slabs(g)
        ms, bcs = [], []
        for r in range(SSD_HPG):
            h = SSD_HPG * g + r
            bcs.append(jnp.broadcast_to(cs_ref[0, c, :, h:h + 1], (L, L)))
            ms.append((scores[g] * jnp.exp2(jnp.where(causal, bcs[r] - rowp_ref[0, c, h:h + 1, :], -jnp.inf))
                       ).astype(BF16))
        x_bd = jnp.concatenate([jnp.where(head_of_lane == r, xsb, jnp.zeros_like(xsb))
                                for r in range(SSD_HPG)], axis=0)
        y_diag = jnp.dot(jnp.concatenate(ms, axis=1), x_bd, preferred_element_type=F32)
        f_start = jnp.concatenate([jnp.exp2(jnp.where(left_half, bcs[2 * k], bcs[2 * k + 1]))
                                   for k in range(spg)], axis=-1)
        o_ref[0, rs, gw * g:gw * (g + 1)] = (
            y_diag + y_offs[g] * f_start + dskip_ref[:, gw * g:gw * (g + 1)] * xsb.astype(F32)
        ).astype(o_ref.dtype)


def ssd_core(xbc, tables, dskip):
    b, n_conv, tp, _ = xbc.shape
    d_inner = dskip.shape[1]
    n_groups = d_inner // (SSD_HPG * SSD_HEAD_DIM)
    cps = SSD_CHUNKS_PER_STEP
    L = SSD_CHUNK * cps
    const = lambda shape: pl.BlockSpec(shape, lambda bi, c: (0,) * len(shape))
    tab = pl.BlockSpec((1, cps, SSD_CHUNK, LANES), lambda bi, c: (bi, c, 0, 0))
    return pl.pallas_call(
        _ssd_kernel,
        grid=(b, tp // L),
        in_specs=[pl.BlockSpec((1, n_conv, L, LANES), lambda bi, c: (bi, 0, c, 0)),
                  tab, tab, tab, tab, const((1, d_inner))],
        out_specs=pl.BlockSpec((1, L, d_inner), lambda bi, c: (bi, c, 0)),
        out_shape=jax.ShapeDtypeStruct((b, tp, d_inner), BF16),
        scratch_shapes=[pltpu.VMEM((n_groups, SSD_HPG * SSD_HEAD_DIM, SSD_STATE), F32)],
        compiler_params=_cparams(2),
        name="ssd_core",
    )(xbc, *tables, dskip)


def _block_diag(w):
    n, di, do = w.shape
    return jnp.einsum('gij,gh->gihj', w, jnp.eye(n, dtype=w.dtype)).reshape(n * di, n * do)


def _pad_lanes(v, fill=0.0):
    return jnp.pad(v, (0, LANES - v.shape[0]), constant_values=fill).reshape(1, LANES)


def kernel(x, meta_tokens, norm_mix, norm_mlp, norm_final, ev_w_in, lru_conv_w, lru_conv_b, lru_w_a, lru_b_a, lru_w_x, lru_b_x, lru_lambda, s5_a_re, s5_a_im, s5_b_re, s5_b_im, s5_c_re, s5_c_im, s5_d, s5_log_dt, s5_w_glu, s5_b_glu, ev_w_out, ssd_w_in, ssd_conv_w, ssd_conv_b, ssd_dt_bias, ssd_a_log, ssd_d, ssd_norm, ssd_w_out, mlp_w_up, mlp_w_down):
    bsz, seq, d = x.shape
    t = seq + N_META
    unit = math.lcm(SUBLANES * LRU_LS, SUBLANES * S5_LS * S5_SC, ROW_TILE, SSD_CHUNK)
    tp = -(-t // unit) * unit
    tm = ROW_TILE
    assert tp - t < tm <= seq
    meta = jnp.broadcast_to(meta_tokens[None].astype(x.dtype), (bsz, N_META, d))
    first = jnp.concatenate([meta, x[:, :tm - N_META]], axis=1)
    last = jnp.concatenate([x[:, tp - tm - N_META:], jnp.zeros((bsz, tp - t, d), x.dtype)], axis=1)

    lru_w = lru_conv_w.shape[-1]
    proj = norm_matmul(x, first, last, norm_mix[0], ev_w_in[0].astype(BF16), lru_conv_w[0], lru_conv_b[0],
                       tp=tp, tm=tm)
    y_lru = rglru(proj, _block_diag(lru_w_a[0]).astype(BF16), lru_b_a[0],
                  _block_diag(lru_w_x[0]).astype(BF16), lru_b_x[0], lru_lambda[0], ls=LRU_LS)
    y_s5 = s5(proj, 2 * lru_w // LANES,
              *_s5_params(s5_a_re[0], s5_a_im[0], s5_b_re[0], s5_b_im[0], s5_c_re[0], s5_c_im[0],
                          s5_d[0], s5_log_dt[0], s5_w_glu[0], s5_b_glu[0], ls=S5_LS), ls=S5_LS)
    h = outproj_slab(x, first, last, y_lru, y_s5, ev_w_out[0].astype(BF16), tm=tm)
    h = mlp_residual(h.reshape(bsz * tp, d), norm_mlp[0], mlp_w_up, mlp_w_down,
                     norm_final, layer=0, tm=MLP_ROW_TILE, tf=512,
                     final_norm=False).reshape(bsz, tp, d)

    d_inner = ssd_w_out.shape[1]
    conv_dim = ssd_conv_w.shape[-1]
    n_heads = ssd_dt_bias.shape[-1]
    w_in = ssd_w_in[0]
    w_zdt = jnp.concatenate([w_in[:, :d_inner], w_in[:, d_inner + conv_dim:],
                             jnp.zeros((d, LANES - n_heads), F32)], axis=1).astype(BF16)
    xbc = inproj_conv(h, norm_mix[1], w_in[:, d_inner:d_inner + conv_dim].astype(BF16),
                      ssd_conv_w[0], ssd_conv_b[0], tm=ROW_TILE, tn=2048)
    z, *tables = inproj_zdt(h, norm_mix[1], w_zdt, _pad_lanes(ssd_dt_bias[0]), _pad_lanes(ssd_a_log[0]),
                            tm=ROW_TILE)
    y = ssd_core(xbc, tables, jnp.repeat(ssd_d[0], SSD_HEAD_DIM).reshape(1, d_inner))
    h = gated_outproj(h, y, z, ssd_norm[0].reshape(1, d_inner), ssd_w_out[0].astype(BF16),
                      tm=GATED_ROW_TILE, group=SSD_HPG * SSD_HEAD_DIM)
    return mlp_residual_final(h, norm_mlp[1], mlp_w_up, mlp_w_down,
                              norm_final, layer=1, seq=seq, tm=FINAL_ROW_TILE, tf=512)
```

```python
import functools
import math

import jax
import jax.numpy as jnp
from jax import lax
from jax.experimental import pallas as pl
from jax.experimental.pallas import tpu as pltpu

F32 = jnp.float32
BF16 = jnp.bfloat16

LANES = 128
SUBLANES = 8
NORM_EPS = 1e-5
LOG2_E = 1.4426950408889634
N_META = 16
CONV_WIDTH = 4
LRU_C = 8.0
S5_GROUP = 16
S5_STATE = 64
SSD_HEAD_DIM = 64
SSD_STATE = 128
SSD_CHUNK = 128
SSD_HPG = 4
SSD_CHUNKS_PER_STEP = 5

LRU_LS = 130
LRU_UNROLL = 5
S5_SC = SUBLANES
S5_LS = 65
S5_UNROLL = 5
ROW_TILE = 1280
MLP_ROW_TILE = 2080
GATED_ROW_TILE = 640
FINAL_ROW_TILE = 2048
VMEM_LIMIT = 48 * 1024 * 1024
VMEM_LIMIT_LARGE = 58 * 1024 * 1024


def _cparams(n_axes, vmem=VMEM_LIMIT):
    return pltpu.CompilerParams(dimension_semantics=("arbitrary",) * n_axes,
                                vmem_limit_bytes=vmem)


def _sigmoid(x):
    return 0.5 + 0.5 * jnp.tanh(0.5 * x)


def _silu(x):
    hx = 0.5 * x
    return hx + hx * jnp.tanh(hx)


def _gelu_tanh(x):
    return 0.5 * x * (1.0 + jnp.tanh(math.sqrt(2.0 / math.pi) * (x + 0.044715 * (x * x * x))))


def _log1p(e):
    u = 1.0 + e
    return jnp.where(u == 1.0, e, jnp.log(u) * (e / (u - 1.0)))


def _softplus(x):
    return jnp.maximum(x, 0.0) + _log1p(jnp.exp(-jnp.abs(x)))


def _rmsnorm_rows(x, w):
    ms = jnp.mean(x * x, axis=-1, keepdims=True)
    return x * lax.rsqrt(ms + NORM_EPS) * w


def _rows(tau):
    return pl.ds(pl.multiple_of(tau * SUBLANES, SUBLANES), SUBLANES)


def _seq_specs(seq, tm, d):
    x_spec = pl.BlockSpec(
        (pl.Element(1), pl.Element(tm), pl.Element(d)),
        lambda bi, i: (bi, pl.multiple_of(jnp.clip(i * tm - N_META, 0, seq - tm), SUBLANES), 0))
    edge_spec = pl.BlockSpec((1, tm, d), lambda bi, i: (bi, 0, 0), pipeline_mode=pl.Buffered(1))
    return [x_spec, edge_spec, edge_spec]


def _seq_tile(x_ref, first_ref, last_ref):
    i = pl.program_id(1)
    return jnp.where(i == 0, first_ref[0], jnp.where(i == pl.num_programs(1) - 1, last_ref[0], x_ref[0]))


def _norm_matmul_kernel(x_ref, first_ref, last_ref, nw_ref, w_ref, cw_ref, cb_ref, o_ref, rbuf, hist):
    tm = x_ref.shape[1]
    n_conv = rbuf.shape[0]
    hrows = SUBLANES

    @pl.when(pl.program_id(1) == 0)
    def _():
        hist[...] = jnp.zeros(hist.shape, F32)

    x = _seq_tile(x_ref, first_ref, last_ref)
    rs = lax.rsqrt(jnp.mean(x * x, axis=-1, keepdims=True) + NORM_EPS)
    xb = (x * nw_ref[...]).astype(BF16)
    step = 4
    for c in range(o_ref.shape[1] // step):
        r = jnp.dot(xb, w_ref[:, LANES * step * c:LANES * step * (c + 1)], preferred_element_type=F32) * rs
        for k in range(step):
            s = step * c + k
            if s >= n_conv:
                o_ref[0, s] = r[:, LANES * k:LANES * (k + 1)]
                continue
            sl = slice(LANES * s, LANES * (s + 1))
            rbuf[s, 0:hrows, :] = hist[s]
            rbuf[s, hrows:hrows + tm, :] = r[:, LANES * k:LANES * (k + 1)]
            hist[s] = rbuf[s, tm:tm + hrows, :]
            taps = [cw_ref[m, :, sl] for m in range(CONV_WIDTH)]
            bias = cb_ref[:, sl]
            for g in range(tm // SUBLANES):
                r0 = hrows + SUBLANES * g
                acc = bias
                for m in range(CONV_WIDTH):
                    acc = acc + taps[m] * rbuf[s, r0 - m:r0 - m + SUBLANES, :]
                o_ref[0, s, SUBLANES * g:SUBLANES * (g + 1), :] = acc


def norm_matmul(x, first, last, nw, w, cw, cb, *, tp, tm):
    b, seq, d = x.shape
    n = w.shape[1]
    wc = cw.shape[1]
    cw8 = jnp.broadcast_to(jnp.stack([cw[CONV_WIDTH - 1 - m] for m in range(CONV_WIDTH)])[:, None, :],
                           (CONV_WIDTH, SUBLANES, wc))
    cb8 = jnp.broadcast_to(cb[None, :], (SUBLANES, wc))
    const = lambda shape: pl.BlockSpec(shape, lambda bi, i: (0,) * len(shape))
    return pl.pallas_call(
        _norm_matmul_kernel,
        grid=(b, tp // tm),
        in_specs=_seq_specs(seq, tm, d) + [
            const((1, d)), const((d, n)), const((CONV_WIDTH, SUBLANES, wc)), const((SUBLANES, wc))],
        out_specs=pl.BlockSpec((1, n // LANES, tm, LANES), lambda bi, i: (bi, 0, i, 0)),
        out_shape=jax.ShapeDtypeStruct((b, n // LANES, tp, LANES), F32),
        scratch_shapes=[pltpu.VMEM((wc // LANES, SUBLANES + tm, LANES), F32),
                        pltpu.VMEM((wc // LANES, SUBLANES, LANES), F32)],
        compiler_params=_cparams(2),
        name="norm_matmul",
    )(x, first, last, nw.reshape(1, d), w, cw8, cb8)


def _inproj_conv_kernel(x_ref, nw_ref, w_ref, cw_ref, cb_ref, o_ref, xn_ref, rs, hist):
    i, j = pl.program_id(1), pl.program_id(2)
    tm = x_ref.shape[1]
    n_slab = o_ref.shape[1]
    hrows = SUBLANES
    pack = 2 * SUBLANES

    @pl.when(j == 0)
    def _():
        xn_ref[...] = _rmsnorm_rows(x_ref[0], nw_ref[...]).astype(BF16)

    @pl.when(i == 0)
    def _():
        hist[j] = jnp.zeros(hist.shape[1:], F32)

    xn = xn_ref[...]
    proj = lambda c: jnp.dot(xn, w_ref[:, 2 * LANES * c:2 * LANES * (c + 1)], preferred_element_type=F32)
    r_next = proj(0)
    for c in range(n_slab // 2):
        r = r_next
        if c + 1 < n_slab // 2:
            r_next = proj(c + 1)
        for k in range(2):
            s = 2 * c + k
            sl = slice(LANES * s, LANES * (s + 1))
            rs[s, 0:hrows, :] = hist[j, s]
            rs[s, hrows:hrows + tm, :] = r[:, LANES * k:LANES * (k + 1)]
            hist[j, s] = rs[s, tm:tm + hrows, :]
            taps = [cw_ref[m, :, sl] for m in range(CONV_WIDTH)]
            bias = cb_ref[:, sl]
            for g in range(tm // pack):
                halves = []
                for hh in range(pack // SUBLANES):
                    r0 = hrows + pack * g + SUBLANES * hh
                    acc = bias
                    for m in range(CONV_WIDTH):
                        acc = acc + taps[m] * rs[s, r0 - m:r0 - m + SUBLANES, :]
                    halves.append(acc)
                o_ref[0, s, pack * g:pack * (g + 1), :] = _silu(jnp.concatenate(halves, axis=0).astype(BF16))


def inproj_conv(h, nw, w, cw, cb, *, tm, tn):
    b, tp, d = h.shape
    n = w.shape[1]
    ns = tn // LANES
    cw8 = jnp.broadcast_to(jnp.stack([cw[CONV_WIDTH - 1 - m] for m in range(CONV_WIDTH)])[:, None, :],
                           (CONV_WIDTH, SUBLANES, n))
    cb8 = jnp.broadcast_to(cb.reshape(1, n), (SUBLANES, n))
    return pl.pallas_call(
        _inproj_conv_kernel,
        grid=(b, tp // tm, n // tn),
        in_specs=[pl.BlockSpec((1, tm, d), lambda bi, i, j: (bi, i, 0)),
                  pl.BlockSpec((1, d), lambda bi, i, j: (0, 0)),
                  pl.BlockSpec((d, tn), lambda bi, i, j: (0, j)),
                  pl.BlockSpec((CONV_WIDTH, SUBLANES, tn), lambda bi, i, j: (0, 0, j)),
                  pl.BlockSpec((SUBLANES, tn), lambda bi, i, j: (0, j))],
        out_specs=pl.BlockSpec((1, ns, tm, LANES), lambda bi, i, j: (bi, j, i, 0)),
        out_shape=jax.ShapeDtypeStruct((b, n // LANES, tp, LANES), BF16),
        scratch_shapes=[pltpu.VMEM((tm, d), BF16),
                        pltpu.VMEM((ns, SUBLANES + tm, LANES), F32),
                        pltpu.VMEM((n // tn, ns, SUBLANES, LANES), F32)],
        compiler_params=_cparams(3),
        name="inproj_conv",
    )(h, nw.reshape(1, d), w, cw8, cb8)


def _ssd_tables(dt_raw, dtb, a, tri):
    L = SSD_CHUNK
    dt = _softplus(dt_raw + dtb)
    da = dt * a
    da_hi = da.astype(BF16)
    r1 = da - da_hi.astype(F32)
    da_mid = r1.astype(BF16)
    da_lo = (r1 - da_mid.astype(F32)).astype(BF16)
    cs = (jnp.dot(tri, da_hi, preferred_element_type=F32)
          + jnp.dot(tri, da_mid, preferred_element_type=F32)
          + jnp.dot(tri, da_lo, preferred_element_type=F32))
    dt_t = dt.T
    cs_t = cs.T
    last_t = jnp.broadcast_to(cs_t[:, L - 1:L], (L, L))
    return cs * LOG2_E, (cs_t - jnp.log(dt_t)) * LOG2_E, dt_t * jnp.exp(last_t - cs_t), jnp.exp(last_t)


def _inproj_zdt_kernel(x_ref, nw_ref, w_ref, dtb_ref, alog_ref, z_ref, cs_ref, rowp_ref, wend_ref, cdec_ref):
    x = x_ref[0]
    rs = lax.rsqrt(jnp.mean(x * x, axis=-1, keepdims=True) + NORM_EPS)
    xb = (x * nw_ref[...]).astype(BF16)
    n_z = z_ref.shape[1]
    dt_raw = jnp.dot(xb, w_ref[:, LANES * n_z:LANES * (n_z + 1)], preferred_element_type=F32) * rs
    L = SSD_CHUNK
    tri = (lax.broadcasted_iota(jnp.int32, (L, L), 0) >= lax.broadcasted_iota(jnp.int32, (L, L), 1)).astype(BF16)
    a = -jnp.exp(alog_ref[...])
    step = 4
    n_chunks = cs_ref.shape[1]
    for c in range(n_z // step):
        r = jnp.dot(xb, w_ref[:, LANES * step * c:LANES * step * (c + 1)], preferred_element_type=F32) * rs
        for k in range(step):
            z_ref[0, step * c + k] = r[:, LANES * k:LANES * (k + 1)].astype(BF16)
        for q in range(c * n_chunks // (n_z // step), (c + 1) * n_chunks // (n_z // step)):
            cs_ref[0, q], rowp_ref[0, q], wend_ref[0, q], cdec_ref[0, q] = _ssd_tables(
                dt_raw[L * q:L * (q + 1)], dtb_ref[...], a, tri)


def inproj_zdt(h, nw, w, dtb, alog, *, tm):
    b, tp, d = h.shape
    n_z = w.shape[1] // LANES - 1
    L = SSD_CHUNK
    const = lambda shape: pl.BlockSpec(shape, lambda bi, i: (0,) * len(shape))
    tab = pl.BlockSpec((1, tm // L, L, LANES), lambda bi, i: (bi, i, 0, 0))
    return pl.pallas_call(
        _inproj_zdt_kernel,
        grid=(b, tp // tm),
        in_specs=[pl.BlockSpec((1, tm, d), lambda bi, i: (bi, i, 0)),
                  const((1, d)), const(w.shape), const((1, LANES)), const((1, LANES))],
        out_specs=[pl.BlockSpec((1, n_z, tm, LANES), lambda bi, i: (bi, 0, i, 0)), tab, tab, tab, tab],
        out_shape=[jax.ShapeDtypeStruct((b, n_z, tp, LANES), BF16)]
        + [jax.ShapeDtypeStruct((b, tp // L, L, LANES), F32)] * 4,
        compiler_params=_cparams(2),
        name="inproj_zdt",
    )(h, nw.reshape(1, d), w, dtb, alog)


def _mlp_kernel(x_ref, nw_ref, wu_ref, wd_ref, fw_ref, o_ref, xn_ref, *, final_norm):
    j = pl.program_id(1)

    @pl.when(j == 0)
    def _():
        x = x_ref[...]
        xn_ref[...] = _rmsnorm_rows(x, nw_ref[...]).astype(BF16)
        o_ref[...] = x

    u = jnp.dot(xn_ref[...], wu_ref[...].astype(BF16), preferred_element_type=F32)
    a = jnp.square(jnp.maximum(u, 0.0)).astype(BF16)
    o_ref[...] += jnp.dot(a, wd_ref[...].astype(BF16), preferred_element_type=F32)

    if final_norm:
        @pl.when(j == pl.num_programs(1) - 1)
        def _():
            o_ref[...] = _rmsnorm_rows(o_ref[...], fw_ref[...])


def mlp_residual(h2, nw, w_up, w_down, fw, *, layer, tm, tf, final_norm):
    n, d = h2.shape
    dff = w_up.shape[2]
    return pl.pallas_call(
        functools.partial(_mlp_kernel, final_norm=final_norm),
        grid=(n // tm, dff // tf),
        in_specs=[pl.BlockSpec((tm, d), lambda i, j: (i, 0)),
                  pl.BlockSpec((1, d), lambda i, j: (0, 0)),
                  pl.BlockSpec((None, d, tf), lambda i, j: (layer, 0, j)),
                  pl.BlockSpec((None, tf, d), lambda i, j: (layer, j, 0)),
                  pl.BlockSpec((1, d), lambda i, j: (0, 0))],
        out_specs=pl.BlockSpec((tm, d), lambda i, j: (i, 0)),
        out_shape=jax.ShapeDtypeStruct((n, d), F32),
        scratch_shapes=[pltpu.VMEM((tm, d), BF16)],
        compiler_params=_cparams(2, vmem=VMEM_LIMIT_LARGE),
        name="mlp_final" if final_norm else "mlp",
    )(h2, nw.reshape(1, d), w_up, w_down, fw.reshape(1, d))


def _mlp_final_kernel(x_ref, nw_ref, wu_ref, wd_ref, fw_ref, o_ref, xn_ref):
    j = pl.program_id(2)

    @pl.when(j == 0)
    def _():
        x = x_ref[0]
        xn_ref[...] = _rmsnorm_rows(x, nw_ref[...]).astype(BF16)
        o_ref[0] = x

    u = jnp.dot(xn_ref[...], wu_ref[...].astype(BF16), preferred_element_type=F32)
    a = jnp.square(jnp.maximum(u, 0.0)).astype(BF16)
    o_ref[0] += jnp.dot(a, wd_ref[...].astype(BF16), preferred_element_type=F32)

    @pl.when(j == pl.num_programs(2) - 1)
    def _():
        o_ref[0] = _rmsnorm_rows(o_ref[0], fw_ref[...])


def mlp_residual_final(h, nw, w_up, w_down, fw, *, layer, seq, tm, tf):
    b, _, d = h.shape
    dff = w_up.shape[2]
    return pl.pallas_call(
        _mlp_final_kernel,
        grid=(b, seq // tm, dff // tf),
        in_specs=[pl.BlockSpec((pl.Element(1), pl.Element(tm), pl.Element(d)),
                               lambda bi, i, j: (bi, pl.multiple_of(N_META + i * tm, SUBLANES), 0)),
                  pl.BlockSpec((1, d), lambda bi, i, j: (0, 0)),
                  pl.BlockSpec((None, d, tf), lambda bi, i, j: (layer, 0, j)),
                  pl.BlockSpec((None, tf, d), lambda bi, i, j: (layer, j, 0)),
                  pl.BlockSpec((1, d), lambda bi, i, j: (0, 0))],
        out_specs=pl.BlockSpec((1, tm, d), lambda bi, i, j: (bi, i, 0)),
        out_shape=jax.ShapeDtypeStruct((b, seq, d), F32),
        scratch_shapes=[pltpu.VMEM((tm, d), BF16)],
        compiler_params=_cparams(3, vmem=VMEM_LIMIT_LARGE),
        name="mlp_final",
    )(h, nw.reshape(1, d), w_up, w_down, fw.reshape(1, d))


def _outproj_slab_kernel(x_ref, first_ref, last_ref, ya_ref, yb_ref, w_ref, o_ref):
    parts = ([ya_ref[0, s].astype(BF16) for s in range(ya_ref.shape[1])]
             + [yb_ref[0, s].astype(BF16) for s in range(yb_ref.shape[1])])
    y = jnp.concatenate(parts, axis=-1)
    o_ref[0] = _seq_tile(x_ref, first_ref, last_ref) + jnp.dot(y, w_ref[...], preferred_element_type=F32)


def outproj_slab(x, first, last, ya, yb, w, *, tm):
    b, seq, d = x.shape
    sa, sb, tp = ya.shape[1], yb.shape[1], ya.shape[2]
    return pl.pallas_call(
        _outproj_slab_kernel,
        grid=(b, tp // tm),
        in_specs=_seq_specs(seq, tm, d) + [
            pl.BlockSpec((1, sa, tm, LANES), lambda bi, i: (bi, 0, i, 0)),
            pl.BlockSpec((1, sb, tm, LANES), lambda bi, i: (bi, 0, i, 0)),
            pl.BlockSpec(w.shape, lambda bi, i: (0, 0))],
        out_specs=pl.BlockSpec((1, tm, d), lambda bi, i: (bi, i, 0)),
        out_shape=jax.ShapeDtypeStruct((b, tp, d), F32),
        compiler_params=_cparams(2),
        name="outproj_slab",
    )(x, first, last, ya, yb, w)


def _gated_outproj_kernel(h_ref, y_ref, z_ref, nw_ref, w_ref, o_ref, *, group):
    acc = h_ref[0]
    spg = group // LANES
    for g in range(y_ref.shape[2] // group):
        cols = slice(group * g, group * (g + 1))
        zg = jnp.concatenate([z_ref[0, spg * g + k] for k in range(spg)], axis=-1).astype(F32)
        gg = y_ref[0, :, cols].astype(F32) * _silu(zg)
        ms = jnp.mean(gg * gg, axis=-1, keepdims=True)
        part = (gg * lax.rsqrt(ms + NORM_EPS) * nw_ref[:, cols]).astype(BF16)
        acc = acc + jnp.dot(part, w_ref[cols, :], preferred_element_type=F32)
    o_ref[0] = acc


def gated_outproj(h, y, z, nw, w, *, tm, group):
    b, tp, d = h.shape
    k = y.shape[-1]
    return pl.pallas_call(
        functools.partial(_gated_outproj_kernel, group=group),
        grid=(b, tp // tm),
        in_specs=[pl.BlockSpec((1, tm, d), lambda bi, i: (bi, i, 0)),
                  pl.BlockSpec((1, tm, k), lambda bi, i: (bi, i, 0)),
                  pl.BlockSpec((1, k // LANES, tm, LANES), lambda bi, i: (bi, 0, i, 0)),
                  pl.BlockSpec((1, k), lambda bi, i: (0, 0)),
                  pl.BlockSpec(w.shape, lambda bi, i: (0, 0))],
        out_specs=pl.BlockSpec((1, tm, d), lambda bi, i: (bi, i, 0)),
        out_shape=jax.ShapeDtypeStruct(h.shape, F32),
        compiler_params=_cparams(2),
        name="gated_outproj",
    )(h, y, z, nw, w)


def _rglru_kernel(x_ref, g_ref, wa_ref, ba_ref, wx_ref, bx_ref, lam_ref, o_ref,
                  xcp, a_s, b_s, hn, carry, *, ls):
    n_slab = x_ref.shape[1]
    slabs = [slice(LANES * s, LANES * (s + 1)) for s in range(n_slab)]

    @pl.when(pl.program_id(1) == 0)
    def _():
        carry[...] = jnp.zeros(carry.shape, F32)

    def gather_body(tau, c):
        for s in range(n_slab):
            xcp[_rows(tau), slabs[s]] = x_ref[0, s, pl.ds(tau, SUBLANES, stride=ls), :]
        return c

    lax.fori_loop(0, ls, gather_body, 0, unroll=LRU_UNROLL)

    xc = xcp[...]
    xb = xc.astype(BF16)
    t_r = jnp.tanh(jnp.dot(xb, wa_ref[...], preferred_element_type=F32) + ba_ref[...])
    t_i = jnp.tanh(jnp.dot(xb, wx_ref[...], preferred_element_type=F32) + bx_ref[...])
    c1 = (-0.5 * LRU_C) * _softplus(-lam_ref[...])
    log_a = c1 + c1 * t_r
    a = jnp.exp(log_a)
    a_s[...] = a
    v = -jnp.tanh(log_a) * (a * a + 1.0)
    hxc = 0.5 * xc
    b_s[...] = jnp.where(v > 0.0, v * lax.rsqrt(v), 0.0) * (hxc + hxc * t_i)

    def pass1(tau, c):
        out = []
        for s in range(n_slab):
            p, e = c[2 * s], c[2 * s + 1]
            av = a_s[_rows(tau), slabs[s]]
            out += [p * av, av * e + b_s[_rows(tau), slabs[s]]]
        return tuple(out)

    one = jnp.ones((SUBLANES, LANES), F32)
    zero = jnp.zeros((SUBLANES, LANES), F32)
    pe = lax.fori_loop(0, ls, pass1, (one, zero) * n_slab, unroll=LRU_UNROLL)

    h0 = []
    for s in range(n_slab):
        p, e = pe[2 * s], pe[2 * s + 1]
        c = carry[0:1, slabs[s]]
        rows = []
        for j in range(SUBLANES):
            rows.append(c)
            c = p[j:j + 1] * c + e[j:j + 1]
        carry[0:1, slabs[s]] = c
        h0.append(jnp.concatenate(rows, axis=0))

    def pass2(tau, hs):
        out = []
        for s in range(n_slab):
            h = a_s[_rows(tau), slabs[s]] * hs[s] + b_s[_rows(tau), slabs[s]]
            hn[s, pl.ds(tau, SUBLANES, stride=ls), :] = h
            out.append(h)
        return tuple(out)

    lax.fori_loop(0, ls, pass2, tuple(h0), unroll=LRU_UNROLL)
    for s in range(n_slab):
        o_ref[0, s] = (hn[s] * _gelu_tanh(g_ref[0, s])).astype(o_ref.dtype)


def rglru(proj, wa_bd, ba, wx_bd, bx, lam, *, ls):
    b, _, tp, _ = proj.shape
    w = lam.shape[0]
    ns = w // LANES
    tt = SUBLANES * ls
    const = lambda shape: pl.BlockSpec(shape, lambda bi, t: (0,) * len(shape))
    return pl.pallas_call(
        functools.partial(_rglru_kernel, ls=ls),
        grid=(b, tp // tt),
        in_specs=[pl.BlockSpec((1, ns, tt, LANES), lambda bi, t: (bi, 0, t, 0)),
                  pl.BlockSpec((1, ns, tt, LANES), lambda bi, t: (bi, 1, t, 0)),
                  const((w, w)), const((1, w)), const((w, w)), const((1, w)), const((1, w))],
        out_specs=pl.BlockSpec((1, ns, tt, LANES), lambda bi, t: (bi, 0, t, 0)),
        out_shape=jax.ShapeDtypeStruct((b, ns, tp, LANES), BF16),
        scratch_shapes=[pltpu.VMEM((tt, w), F32), pltpu.VMEM((tt, w), F32), pltpu.VMEM((tt, w), F32),
                        pltpu.VMEM((ns, tt, LANES), F32), pltpu.VMEM((SUBLANES, w), F32)],
        compiler_params=_cparams(2),
        name="rglru",
    )(proj, proj, wa_bd, ba.reshape(1, w), wx_bd, bx.reshape(1, w), lam.reshape(1, w))


def _s5_kernel(u_ref, bend_ref, kc_ref, l8r_ref, l8i_ref, plr_ref, pli_ref, d_ref, wg_ref, bg_ref,
               o_ref, lp, st, yv, carry, *, ls):
    sc = S5_SC
    n_cs = l8r_ref.shape[1]
    stride = sc * ls
    cre = [slice(LANES * k, LANES * (k + 1)) for k in range(n_cs)]
    cim = [slice(LANES * (n_cs + k), LANES * (n_cs + k + 1)) for k in range(n_cs)]
    lanes = [slice(LANES * q, LANES * (q + 1)) for q in range(sc)]

    @pl.when(pl.program_id(2) == 0)
    def _():
        carry[...] = jnp.zeros(carry.shape, F32)

    def gather_body(tau, c):
        for sg in range(sc):
            lp[_rows(tau), lanes[sg]] = u_ref[0, 0, pl.ds(tau * sc + sg, SUBLANES, stride=stride), :]
        return c

    lax.fori_loop(0, ls, gather_body, 0, unroll=S5_UNROLL)
    st[...] =jnp.dot(lp[...].astype(BF16), bend_ref[0], preferred_element_type=F32)

    lam = [(jnp.broadcast_to(l8r_ref[0, k], (SUBLANES, LANES)),
            jnp.broadcast_to(l8i_ref[0, k], (SUBLANES, LANES))) for k in range(n_cs)]

    def step(tau, k, sr, si):
        lr, li = lam[k]
        return (lr * sr - li * si + st[_rows(tau), cre[k]], lr * si + li * sr + st[_rows(tau), cim[k]])

    def pass1(tau, c):
        out = []
        for k in range(n_cs):
            out += list(step(tau, k, c[2 * k], c[2 * k + 1]))
        return tuple(out)

    zero = jnp.zeros((SUBLANES, LANES), F32)
    ends = lax.fori_loop(0, ls, pass1, (zero,) * (2 * n_cs), unroll=S5_UNROLL)

    starts = []
    for k in range(n_cs):
        er, ei = ends[2 * k], ends[2 * k + 1]
        pr, pi = plr_ref[0, k], pli_ref[0, k]
        c_r, c_i = carry[0:1, cre[k]], carry[0:1, cim[k]]
        rows_r, rows_i = [], []
        for j in range(SUBLANES):
            rows_r.append(c_r)
            rows_i.append(c_i)
            c_r, c_i = (pr * c_r - pi * c_i + er[j:j + 1], pr * c_i + pi * c_r + ei[j:j + 1])
        carry[0:1, cre[k]] = c_r
        carry[0:1, cim[k]] = c_i
        starts += [jnp.concatenate(rows_r, axis=0), jnp.concatenate(rows_i, axis=0)]

    def pass2(tau, c):
        out = []
        for k in range(n_cs):
            nr, ni = step(tau, k, c[2 * k], c[2 * k + 1])
            st[_rows(tau), cre[k]] = c[2 * k]
            st[_rows(tau), cim[k]] = c[2 * k + 1]
            out += [nr, ni]
        return tuple(out)

    lax.fori_loop(0, ls, pass2, tuple(starts), unroll=S5_UNROLL)

    kin = lp.shape[1]
    lp_b, st_b = lp[...].astype(BF16), st[...].astype(BF16)
    tile = 2 * LANES
    for tcol in range(kin // tile):
        cols = slice(tile * tcol, tile * (tcol + 1))
        kk = tile * (tcol + 1)
        yv[:, cols] = (jnp.dot(lp_b[:, :kk], kc_ref[0, 0:kk, cols], preferred_element_type=F32)
                       + jnp.dot(st_b, kc_ref[0, kin:, cols], preferred_element_type=F32))
    for q in range(sc):
        y = yv[:, lanes[q]] + d_ref[0] * lp[:, lanes[q]]
        y = _gelu_tanh(y)
        yv[:, lanes[q]] = y * _sigmoid(jnp.dot(y.astype(BF16), wg_ref[0], preferred_element_type=F32)
                                       + bg_ref[0])

    def scatter_body(tau, c):
        for sg in range(sc):
            o_ref[0, 0, pl.ds(tau * sc + sg, SUBLANES, stride=stride), :] = yv[_rows(tau), lanes[sg]]
        return c

    lax.fori_loop(0, ls, scatter_body, 0, unroll=S5_UNROLL)


def s5(proj, first_slab, bend, kc, l8r, l8i, plr, pli, d, wg, bg, *, ls):
    b, _, tp, _ = proj.shape
    nblk, kin, nst = bend.shape
    n_cs = nst // (2 * LANES)
    tt = SUBLANES * ls * S5_SC
    nc = SUBLANES * ls
    per_blk = lambda shape: pl.BlockSpec((1,) + shape, lambda bi, gb, t: (gb,) + (0,) * len(shape))
    return pl.pallas_call(
        functools.partial(_s5_kernel, ls=ls),
        grid=(b, nblk, tp // tt),
        in_specs=[pl.BlockSpec((1, 1, tt, LANES), lambda bi, gb, t: (bi, first_slab + gb, t, 0)),
                  per_blk((kin, nst)), per_blk((kin + nst, kin)),
                  per_blk((n_cs, 1, LANES)), per_blk((n_cs, 1, LANES)),
                  per_blk((n_cs, 1, LANES)), per_blk((n_cs, 1, LANES)),
                  per_blk((1, LANES)), per_blk((LANES, LANES)), per_blk((1, LANES))],
        out_specs=pl.BlockSpec((1, 1, tt, LANES), lambda bi, gb, t: (bi, gb, t, 0)),
        out_shape=jax.ShapeDtypeStruct((b, nblk, tp, LANES), F32),
        scratch_shapes=[pltpu.VMEM((nc, kin), F32), pltpu.VMEM((nc, nst), F32),
                        pltpu.VMEM((nc, kin), F32), pltpu.VMEM((SUBLANES, nst), F32)],
        compiler_params=_cparams(3),
        name="s5",
    )(proj, bend, kc, l8r, l8i, plr, pli, d, wg, bg)


def _s5_params(a_re, a_im, b_re, b_im, c_re, c_im, d, log_dt, w_glu, b_glu, *, ls):
    g, p = a_re.shape
    sc = S5_SC
    gpb = LANES // S5_GROUP
    nblk = g // gpb
    dt = jnp.exp(log_dt)[:, None]
    mag = jnp.exp(a_re * dt)
    ar, ai = mag * jnp.cos(a_im * dt), mag * jnp.sin(a_im * dt)
    den = a_re * a_re + a_im * a_im
    fr = ((ar - 1.0) * a_re + ai * a_im) / den
    fi = (ai * a_re - (ar - 1.0) * a_im) / den
    bbar_re = fr[..., None] * b_re - fi[..., None] * b_im
    bbar_im = fr[..., None] * b_im + fi[..., None] * b_re

    def lam_pow(k):
        k = jnp.asarray(k, F32).reshape(-1, 1, 1)
        m = jnp.exp(a_re * dt * k)
        return m * jnp.cos(a_im * dt * k), m * jnp.sin(a_im * dt * k)

    pw_r, pw_i = lam_pow(jnp.arange(sc + 1))
    def group_diag(dense, rows_per_group):
        n = dense.shape[-1]
        tiled = jnp.tile(dense, (1,) * (dense.ndim - 1) + (gpb,))
        rg = lax.broadcasted_iota(jnp.int32, tiled.shape[-2:], 0) // rows_per_group
        cg = lax.broadcasted_iota(jnp.int32, tiled.shape[-2:], 1) // n
        return jnp.where(rg == cg, tiled, 0.0)

    def by_block(m):
        k, _, a, b2 = m.shape
        return jnp.transpose(m.reshape(k, nblk, gpb, a, b2), (0, 1, 2, 4, 3)).reshape(k, nblk, gpb * b2, a)

    wr = jnp.stack([pw_r[sc - 1 - s] for s in range(sc)])[..., None]
    wi = jnp.stack([pw_i[sc - 1 - s] for s in range(sc)])[..., None]
    e_re = wr * bbar_re - wi * bbar_im
    e_im = wr * bbar_im + wi * bbar_re
    rows_cat = lambda m: jnp.concatenate([m[k] for k in range(m.shape[0])], axis=-2)
    bend = jnp.concatenate([rows_cat(group_diag(by_block(e_re).astype(BF16), S5_GROUP)),
                            rows_cat(group_diag(by_block(e_im).astype(BF16), S5_GROUP))], axis=-1)

    cl_re = c_re[None] * pw_r[:, :, None, :] - c_im[None] * pw_i[:, :, None, :]
    cl_im = c_re[None] * pw_i[:, :, None, :] + c_im[None] * pw_r[:, :, None, :]
    kl = (jnp.einsum('kgip,gpj->kgij', cl_re[:sc], bbar_re)
          - jnp.einsum('kgip,gpj->kgij', cl_im[:sc], bbar_im))
    kd = group_diag(by_block(kl).astype(BF16), S5_GROUP)
    zero_blk = jnp.zeros_like(kd[0])
    kintra = jnp.concatenate(
        [jnp.concatenate([kd[t - s] if t >= s else zero_blk for t in range(sc)], axis=-1)
         for s in range(sc)], axis=-2)

    def out_bd(m):
        dense = jnp.transpose(m, (0, 1, 3, 2))
        d2 = group_diag(dense.reshape(sc, nblk, gpb * p, S5_GROUP).astype(BF16), p)
        return jnp.concatenate([d2[t] for t in range(sc)], axis=-1)

    kc = jnp.concatenate([kintra, out_bd(cl_re[1:]), out_bd(-cl_im[1:])], axis=-2)

    vec = lambda v: v.reshape(nblk, (gpb * p) // LANES, 1, LANES)
    pl_r, pl_i = lam_pow(jnp.asarray([sc * ls]))
    wg = group_diag(w_glu.reshape(nblk, LANES, S5_GROUP), S5_GROUP).astype(BF16)
    return (bend, kc, vec(pw_r[sc]), vec(pw_i[sc]), vec(pl_r[0]), vec(pl_i[0]),
            d.reshape(nblk, 1, LANES), wg, b_glu.reshape(nblk, 1, LANES))


def _ssd_kernel(xbc_ref, cs_ref, rowp_ref, wend_ref, cdec_ref, dskip_ref, o_ref, hst):
    @pl.when(pl.program_id(1) == 0)
    def _():
        hst[...] = jnp.zeros(hst.shape, F32)

    for c in range(xbc_ref.shape[2] // SSD_CHUNK):
        _ssd_chunk(xbc_ref, cs_ref, rowp_ref, wend_ref, cdec_ref, dskip_ref, o_ref, hst, c)


def _ssd_chunk(xbc_ref, cs_ref, rowp_ref, wend_ref, cdec_ref, dskip_ref, o_ref, hst, c):
    L = SSD_CHUNK
    rs = slice(L * c, L * (c + 1))
    n_groups = hst.shape[0]
    gw = SSD_HPG * SSD_HEAD_DIM
    spg = gw // LANES
    n_xs = n_groups * spg
    row = lax.broadcasted_iota(jnp.int32, (L, L), 0)
    col = lax.broadcasted_iota(jnp.int32, (L, L), 1)
    causal = row >= col
    left_half = col < SSD_HEAD_DIM
    head_of_lane = lax.broadcasted_iota(jnp.int32, (L, gw), 1) // SSD_HEAD_DIM

    def head_rows(ref, g):
        return jnp.concatenate(
            [jnp.broadcast_to(ref[0, c, SSD_HPG * g + r:SSD_HPG * g + r + 1, :], (SSD_HEAD_DIM, L))
             for r in range(SSD_HPG)], axis=0)

    def x_slabs(g):
        return jnp.concatenate([xbc_ref[0, spg * g + k, rs, :] for k in range(spg)], axis=-1)

    scores, y_offs = [], []
    for g in range(n_groups):
        bm = xbc_ref[0, n_xs + g, rs, :]
        cm = xbc_ref[0, n_xs + n_groups + g, rs, :]
        scores.append(lax.dot_general(cm, bm, (((1,), (1,)), ((), ())), preferred_element_type=F32))
        h_prev = hst[g]
        y_offs.append(lax.dot_general(cm, h_prev.astype(BF16), (((1,), (1,)), ((), ())),
                                      preferred_element_type=F32))
        xw_t = (x_slabs(g).astype(F32).T * head_rows(wend_ref, g)).astype(BF16)
        hst[g] = head_rows(cdec_ref, g) * h_prev + jnp.dot(xw_t, bm, preferred_element_type=F32)

    for g in range(n_groups):
        xsb = x_slabs(g)
        ms, bcs = [], []
        for r in range(SSD_HPG):
            h = SSD_HPG * g + r
            bcs.append(jnp.broadcast_to(cs_ref[0, c, :, h:h + 1], (L, L)))
            ms.append((scores[g] * jnp.exp2(jnp.where(causal, bcs[r] - rowp_ref[0, c, h:h + 1, :], -jnp.inf))
                       ).astype(BF16))
        x_bd = jnp.concatenate([jnp.where(head_of_lane == r, xsb, jnp.zeros_like(xsb))
                                for r in range(SSD_HPG)], axis=0)
        y_diag = jnp.dot(jnp.concatenate(ms, axis=1), x_bd, preferred_element_type=F32)
        f_start = jnp.concatenate([jnp.exp2(jnp.where(left_half, bcs[2 * k], bcs[2 * k + 1]))
                                   for k in range(spg)], axis=-1)
        o_ref[0, rs, gw * g:gw * (g + 1)] = (
            y_diag + y_offs[g] * f_start + dskip_ref[:, gw * g:gw * (g + 1)] * xsb.astype(F32)
        ).astype(o_ref.dtype)


def ssd_core(xbc, tables, dskip):
    b, n_conv, tp, _ = xbc.shape
    d_inner = dskip.shape[1]
    n_groups = d_inner // (SSD_HPG * SSD_HEAD_DIM)
    cps = SSD_CHUNKS_PER_STEP
    L = SSD_CHUNK * cps
    const = lambda shape: pl.BlockSpec(shape, lambda bi, c: (0,) * len(shape))
    tab = pl.BlockSpec((1, cps, SSD_CHUNK, LANES), lambda bi, c: (bi, c, 0, 0))
    return pl.pallas_call(
        _ssd_kernel,
        grid=(b, tp // L),
        in_specs=[pl.BlockSpec((1, n_conv, L, LANES), lambda bi, c: (bi, 0, c, 0)),
                  tab, tab, tab, tab, const((1, d_inner))],
        out_specs=pl.BlockSpec((1, L, d_inner), lambda bi, c: (bi, c, 0)),
        out_shape=jax.ShapeDtypeStruct((b, tp, d_inner), BF16),
        scratch_shapes=[pltpu.VMEM((n_groups, SSD_HPG * SSD_HEAD_DIM, SSD_STATE), F32)],
        compiler_params=_cparams(2),
        name="ssd_core",
    )(xbc, *tables, dskip)


def _block_diag(w):
    n, di, do = w.shape
    return jnp.einsum('gij,gh->gihj', w, jnp.eye(n, dtype=w.dtype)).reshape(n * di, n * do)


def _pad_lanes(v, fill=0.0):
    return jnp.pad(v, (0, LANES - v.shape[0]), constant_values=fill).reshape(1, LANES)


def kernel(x, meta_tokens, norm_mix, norm_mlp, norm_final, ev_w_in, lru_conv_w, lru_conv_b, lru_w_a, lru_b_a, lru_w_x, lru_b_x, lru_lambda, s5_a_re, s5_a_im, s5_b_re, s5_b_im, s5_c_re, s5_c_im, s5_d, s5_log_dt, s5_w_glu, s5_b_glu, ev_w_out, ssd_w_in, ssd_conv_w, ssd_conv_b, ssd_dt_bias, ssd_a_log, ssd_d, ssd_norm, ssd_w_out, mlp_w_up, mlp_w_down):
    bsz, seq, d = x.shape
    t = seq + N_META
    unit = math.lcm(SUBLANES * LRU_LS, SUBLANES * S5_LS * S5_SC, ROW_TILE, SSD_CHUNK)
    tp = -(-t // unit) * unit
    tm = ROW_TILE
    assert tp - t < tm <= seq
    meta = jnp.broadcast_to(meta_tokens[None].astype(x.dtype), (bsz, N_META, d))
    first = jnp.concatenate([meta, x[:, :tm - N_META]], axis=1)
    last = jnp.concatenate([x[:, tp - tm - N_META:], jnp.zeros((bsz, tp - t, d), x.dtype)], axis=1)

    lru_w = lru_conv_w.shape[-1]
    proj = norm_matmul(x, first, last, norm_mix[0], ev_w_in[0].astype(BF16), lru_conv_w[0], lru_conv_b[0],
                       tp=tp, tm=tm)
    y_lru = rglru(proj, (0.5 * _block_diag(lru_w_a[0])).astype(BF16), 0.5 * lru_b_a[0],
                  (0.5 * _block_diag(lru_w_x[0])).astype(BF16), 0.5 * lru_b_x[0], lru_lambda[0], ls=LRU_LS)
    y_s5 = s5(proj, 2 * lru_w // LANES,
              *_s5_params(s5_a_re[0], s5_a_im[0], s5_b_re[0], s5_b_im[0], s5_c_re[0], s5_c_im[0],
                          s5_d[0], s5_log_dt[0], s5_w_glu[0], s5_b_glu[0], ls=S5_LS), ls=S5_LS)
    h = outproj_slab(x, first, last, y_lru, y_s5, ev_w_out[0].astype(BF16), tm=tm)
    h = mlp_residual(h.reshape(bsz * tp, d), norm_mlp[0], mlp_w_up, mlp_w_down,
                     norm_final, layer=0, tm=MLP_ROW_TILE, tf=512,
                     final_norm=False).reshape(bsz, tp, d)

    d_inner = ssd_w_out.shape[1]
    conv_dim = ssd_conv_w.shape[-1]
    n_heads = ssd_dt_bias.shape[-1]
    w_in = ssd_w_in[0]
    w_zdt = jnp.concatenate([w_in[:, :d_inner], w_in[:, d_inner + conv_dim:],
                             jnp.zeros((d, LANES - n_heads), F32)], axis=1).astype(BF16)
    xbc = inproj_conv(h, norm_mix[1], w_in[:, d_inner:d_inner + conv_dim].astype(BF16),
                      ssd_conv_w[0], ssd_conv_b[0], tm=ROW_TILE, tn=2048)
    z, *tables = inproj_zdt(h, norm_mix[1], w_zdt, _pad_lanes(ssd_dt_bias[0]), _pad_lanes(ssd_a_log[0]),
                            tm=ROW_TILE)
    y = ssd_core(xbc, tables, jnp.repeat(ssd_d[0], SSD_HEAD_DIM).reshape(1, d_inner))
    h = gated_outproj(h, y, z, ssd_norm[0].reshape(1, d_inner), ssd_w_out[0].astype(BF16),
                      tm=GATED_ROW_TILE, group=SSD_HPG * SSD_HEAD_DIM)
    return mlp_residual_final(h, norm_mlp[1], mlp_w_up, mlp_w_down,
                              norm_final, layer=1, seq=seq, tm=FINAL_ROW_TILE, tf=512)
```
